```python
import math
import jax, jax.numpy as jnp
from jax import lax
import numpy as np

D_MODEL = 1024
BATCH = 8
SEQ = 4096
DEPTH = 1

N_META = 16
GRID_W = 64
Q_BLOCK = 128
HEAD_DIM = 64
ROPE_THETA = 10000.0
EPS = 1e-6
A_HEADS = 8
A_KV_HEADS = 2
A_GROUP = A_HEADS // A_KV_HEADS
A_WIDTH = A_HEADS * HEAD_DIM
A_KV_WIDTH = A_KV_HEADS * HEAD_DIM
B_HEADS = 4
B_VDIM = 2 * HEAD_DIM
B_WIDTH = B_HEADS * B_VDIM
MIX_WIDTH = A_WIDTH + B_WIDTH
IN_WIDTH = A_WIDTH + 2 * A_KV_WIDTH + 3 * B_WIDTH
D_FF = -(-8 * D_MODEL // (3 * 256)) * 256

kernel_name = "hymba_axial_gqa_diffattn_swiglu"


def rms_norm(x, g):
    xf = x.astype(jnp.float32)
    y = xf * lax.rsqrt(jnp.mean(xf * xf, axis=-1, keepdims=True) + EPS)
    return (y * g.astype(jnp.float32)).astype(x.dtype)


def apply_rope(x, ang):
    xf = x.astype(jnp.float32).reshape(x.shape[:-1] + (x.shape[-1] // 2, 2))
    c = jnp.cos(ang)[None, :, None, :]
    s = jnp.sin(ang)[None, :, None, :]
    x1, x2 = xf[..., 0], xf[..., 1]
    out = jnp.stack([x1 * c - x2 * s, x1 * s + x2 * c], axis=-1)
    return out.reshape(x.shape).astype(x.dtype)


def rope_angles_1d(length):
    pos = jnp.arange(length, dtype=jnp.float32)
    inv = ROPE_THETA ** (-jnp.arange(0, HEAD_DIM, 2, dtype=jnp.float32) / HEAD_DIM)
    return pos[:, None] * inv[None, :]


def rope_angles_axial(n_tokens):
    rows = n_tokens // GRID_W
    r = jnp.repeat(jnp.arange(rows, dtype=jnp.float32), GRID_W)
    c = jnp.tile(jnp.arange(GRID_W, dtype=jnp.float32), rows)
    half = HEAD_DIM // 2
    inv = ROPE_THETA ** (-jnp.arange(0, half, 2, dtype=jnp.float32) / half)
    ang = jnp.concatenate([r[:, None] * inv[None, :], c[:, None] * inv[None, :]], axis=-1)
    meta = jnp.zeros((N_META, HEAD_DIM // 2), jnp.float32)
    return jnp.concatenate([meta, ang], axis=0)


def sweep_queries(fn, q):
    out_meta = fn(q[:, :N_META])
    real = q[:, N_META:]
    bsz, s = real.shape[0], real.shape[1]
    nb = s // Q_BLOCK
    blocks = jnp.moveaxis(real.reshape((bsz, nb, Q_BLOCK) + real.shape[2:]), 1, 0)
    out = lax.map(fn, blocks)
    out = jnp.moveaxis(out, 0, 1).reshape((bsz, s) + out.shape[3:])
    return jnp.concatenate([out_meta, out], axis=1)


def setup_inputs(seed: int = 0) -> dict:
    key = jax.random.key(seed)
    ks = jax.random.split(key, 20)
    f32 = jnp.float32

    def gain(k, shape):
        return 1.0 + 0.1 * jax.random.normal(k, shape, f32)

    return {
        "x": jax.random.normal(ks[0], (BATCH, SEQ, D_MODEL), f32),
        "meta_tokens": jax.random.normal(ks[1], (N_META, D_MODEL), f32),
        "attn_norm_g": gain(ks[2], (DEPTH, D_MODEL)),
        "w_in": jax.random.normal(ks[3], (DEPTH, D_MODEL, IN_WIDTH), f32) * D_MODEL ** -0.5,
        "a_q_norm_g": gain(ks[4], (DEPTH, HEAD_DIM)),
        "a_k_norm_g": gain(ks[5], (DEPTH, HEAD_DIM)),
        "b_q_norm_g": gain(ks[6], (DEPTH, HEAD_DIM)),
        "b_k_norm_g": gain(ks[7], (DEPTH, HEAD_DIM)),
        "b_lambda_q1": 0.1 * jax.random.normal(ks[8], (DEPTH, HEAD_DIM), f32),
        "b_lambda_k1": 0.1 * jax.random.normal(ks[9], (DEPTH, HEAD_DIM), f32),
        "b_lambda_q2": 0.1 * jax.random.normal(ks[10], (DEPTH, HEAD_DIM), f32),
        "b_lambda_k2": 0.1 * jax.random.normal(ks[11], (DEPTH, HEAD_DIM), f32),
        "b_subln_g": gain(ks[12], (DEPTH, B_VDIM)),
        "w_out": jax.random.normal(ks[13], (DEPTH, MIX_WIDTH, D_MODEL), f32) * MIX_WIDTH ** -0.5,
        "ffn_norm_g": gain(ks[14], (DEPTH, D_MODEL)),
        "w_gate": jax.random.normal(ks[15], (DEPTH, D_MODEL, D_FF), f32) * D_MODEL ** -0.5,
        "w_up": jax.random.normal(ks[16], (DEPTH, D_MODEL, D_FF), f32) * D_MODEL ** -0.5,
        "w_down": jax.random.normal(ks[17], (DEPTH, D_FF, D_MODEL), f32) * D_FF ** -0.5,
    }


def reference(x, meta_tokens, attn_norm_g, w_in, a_q_norm_g, a_k_norm_g, b_q_norm_g, b_k_norm_g,
              b_lambda_q1, b_lambda_k1, b_lambda_q2, b_lambda_k2, b_subln_g, w_out,
              ffn_norm_g, w_gate, w_up, w_down):
    bsz, s, _ = x.shape
    meta = jnp.broadcast_to(meta_tokens.astype(x.dtype)[None], (bsz, N_META, D_MODEL))
    x = jnp.concatenate([meta, x], axis=1)
    length = N_META + s

    ang_a = rope_angles_axial(s)
    ang_b = rope_angles_1d(length)
    scale = HEAD_DIM ** -0.5
    splits = np.cumsum([A_WIDTH, A_KV_WIDTH, A_KV_WIDTH, B_WIDTH, B_WIDTH]).tolist()

    for l in range(DEPTH):
        lam_init = 0.8 - 0.6 * math.exp(-0.3 * l)
        h = rms_norm(x, attn_norm_g[l])
        proj = jnp.einsum('bld,de->ble', h, w_in[l])
        qa, ka, va, qb, kb, vb = jnp.split(proj, splits, axis=-1)

        qa = apply_rope(rms_norm(qa.reshape(bsz, length, A_HEADS, HEAD_DIM), a_q_norm_g[l]), ang_a)
        ka = apply_rope(rms_norm(ka.reshape(bsz, length, A_KV_HEADS, HEAD_DIM), a_k_norm_g[l]), ang_a)
        va = va.reshape(bsz, length, A_KV_HEADS, HEAD_DIM)

        def gqa_block(q):
            nq = q.shape[1]
            qg = q.reshape(q.shape[0], nq, A_KV_HEADS, A_GROUP, HEAD_DIM)
            sc = jnp.einsum('bqkgd,blkd->bkgql', qg, ka).astype(jnp.float32) * scale
            p = jax.nn.softmax(sc, axis=-1).astype(va.dtype)
            o = jnp.einsum('bkgql,blkd->bqkgd', p, va)
            return o.reshape(q.shape[0], nq, A_WIDTH)

        out_a = sweep_queries(gqa_block, qa)

        qb = apply_rope(rms_norm(qb.reshape(bsz, length, 2 * B_HEADS, HEAD_DIM), b_q_norm_g[l]), ang_b)
        kb = apply_rope(rms_norm(kb.reshape(bsz, length, 2 * B_HEADS, HEAD_DIM), b_k_norm_g[l]), ang_b)
        qb = qb.reshape(bsz, length, B_HEADS, 2, HEAD_DIM)
        kb = kb.reshape(bsz, length, B_HEADS, 2, HEAD_DIM)
        vb = vb.reshape(bsz, length, B_HEADS, B_VDIM)
        lam = (jnp.exp(jnp.sum(b_lambda_q1[l].astype(jnp.float32) * b_lambda_k1[l].astype(jnp.float32)))
               - jnp.exp(jnp.sum(b_lambda_q2[l].astype(jnp.float32) * b_lambda_k2[l].astype(jnp.float32)))
               + lam_init)
        subln_g = b_subln_g[l]

        def diff_block(q):
            nq = q.shape[1]
            sc = jnp.einsum('bqhjd,blhjd->bhjql', q, kb).astype(jnp.float32) * scale
            p = jax.nn.softmax(sc, axis=-1)
            w = (p[:, :, 0] - lam * p[:, :, 1]).astype(vb.dtype)
            o = jnp.einsum('bhql,blhe->bqhe', w, vb)
            o = rms_norm(o, subln_g) * (1.0 - lam_init)
            return o.reshape(q.shape[0], nq, B_WIDTH)

        out_b = sweep_queries(diff_block, qb)

        mixed = jnp.concatenate([out_a, out_b], axis=-1)
        x = x + jnp.einsum('ble,ed->bld', mixed, w_out[l])

        h = rms_norm(x, ffn_norm_g[l])
        g = jnp.einsum('bld,df->blf', h, w_gate[l])
        u = jnp.einsum('bld,df->blf', h, w_up[l])
        x = x + jnp.einsum('blf,fd->bld', jax.nn.silu(g) * u, w_down[l])

    return x[:, N_META:]
```

```python
import functools
import math

import jax
import jax.numpy as jnp
from jax import lax
from jax.experimental import pallas as pl
from jax.experimental.pallas import tpu as pltpu

N_META = 16
GRID_W = 64
HEAD_DIM = 64
ROPE_THETA = 10000.0
EPS = 1e-6
A_HEADS = 8
A_KV_HEADS = 2
A_GROUP = A_HEADS // A_KV_HEADS
A_WIDTH = A_HEADS * HEAD_DIM
A_KV_WIDTH = A_KV_HEADS * HEAD_DIM
B_HEADS = 4
B_VDIM = 2 * HEAD_DIM
B_WIDTH = B_HEADS * B_VDIM
LAM_INIT = 0.8 - 0.6 * math.exp(-0.3 * 0)

LANES = 128
VMEM_LIMIT_BYTES = 56 * 1024 * 1024

Q_PRESCALE = (HEAD_DIM ** -0.5) * math.log2(math.e)
MASK_BIAS = -1e30

PROJ_ROWS = 512
A_Q_ROWS = 128
B_Q_ROWS = 256
KV_CHUNK = 512
POST_ROWS = 512

BF16 = jnp.bfloat16
F32 = jnp.float32


def _dot(a, b):
    return jnp.dot(a, b, preferred_element_type=F32)


def _dot_nt(a, b):
    return lax.dot_general(a, b, (((1,), (1,)), ((), ())), preferred_element_type=F32)


def _lane_index(shape):
    return lax.broadcasted_iota(jnp.int32, shape, len(shape) - 1)


def _proj_kernel(x_ref, g_ref, w_ref, cos_a_ref, sin_a_ref, cos_b_ref, sin_b_ref,
                 gains_ref, seg_ref,
                 qa_ref, ka_ref, va_ref, qb_ref, kb_ref, vb_ref):
    x = x_ref[...]
    ms = jnp.mean(x * x, axis=-1, keepdims=True)
    h = (x * lax.rsqrt(ms + EPS) * g_ref[...]).astype(BF16)
    proj = _dot(h, w_ref[...])

    rows = x.shape[0]
    lane = _lane_index((rows, LANES))
    even = (lane % 2) == 0
    low_half = lane < HEAD_DIM
    seg = seg_ref[...]

    def norm_rope(col, gain, cos, sin):
        ss = _dot((col * col).astype(BF16), seg)
        y = col * lax.rsqrt(ss * (1.0 / HEAD_DIM) + EPS) * gain
        swapped = jnp.where(even, pltpu.roll(y, LANES - 1, 1), pltpu.roll(y, 1, 1))
        return y * cos + swapped * sin

    cos_a, sin_a = cos_a_ref[...], sin_a_ref[...]
    cos_b, sin_b = cos_b_ref[...], sin_b_ref[...]
    g_aq = gains_ref[0:1, :] * Q_PRESCALE
    g_ak = gains_ref[1:2, :]
    g_bq = gains_ref[2:3, :] * Q_PRESCALE
    g_bk = gains_ref[3:4, :]

    off = 0
    for j in range(A_WIDTH // LANES):
        col = norm_rope(proj[:, off + j * LANES: off + (j + 1) * LANES], g_aq, cos_a, sin_a)
        flipped = pltpu.roll(col, HEAD_DIM, 1)
        for half in range(2):
            head = 2 * j + half
            kv = head // A_GROUP
            src = col if half == kv else flipped
            keep = low_half if kv == 0 else jnp.logical_not(low_half)
            qa_ref[:, head * LANES:(head + 1) * LANES] = jnp.where(keep, src, 0.0).astype(BF16)
    off += A_WIDTH
    ka_ref[...] = norm_rope(proj[:, off:off + LANES], g_ak, cos_a, sin_a).astype(BF16)
    off += A_KV_WIDTH
    va_ref[...] = proj[:, off:off + LANES].astype(BF16)
    off += A_KV_WIDTH
    for j in range(B_WIDTH // LANES):
        col = proj[:, off + j * LANES: off + (j + 1) * LANES]
        qb_ref[:, j * LANES:(j + 1) * LANES] = norm_rope(col, g_bq, cos_b, sin_b).astype(BF16)
    off += B_WIDTH
    for j in range(B_WIDTH // LANES):
        col = proj[:, off + j * LANES: off + (j + 1) * LANES]
        kb_ref[:, j * LANES:(j + 1) * LANES] = norm_rope(col, g_bk, cos_b, sin_b).astype(BF16)
    off += B_WIDTH
    vb_ref[...] = proj[:, off:off + B_WIDTH].astype(BF16)


def _project(x_rows, g, w_bf16, tables, gains, seg, rows_per_step, table_steps):
    n_rows, d_model = x_rows.shape
    in_width = w_bf16.shape[1]
    steps = n_rows // rows_per_step
    row_blk = lambda width: pl.BlockSpec((rows_per_step, width), lambda i: (i, 0))
    tab_blk = pl.BlockSpec((rows_per_step, LANES), lambda i: (i % table_steps, 0))
    const = lambda shape: pl.BlockSpec(shape, lambda i: (0, 0))
    widths = (A_HEADS * LANES, LANES, LANES, B_WIDTH, B_WIDTH, B_WIDTH)
    return pl.pallas_call(
        _proj_kernel,
        grid=(steps,),
        in_specs=[row_blk(d_model), const((1, d_model)), const((d_model, in_width)),
                  tab_blk, tab_blk, tab_blk, tab_blk, const(gains.shape), const(seg.shape)],
        out_specs=[row_blk(w) for w in widths],
        out_shape=[jax.ShapeDtypeStruct((n_rows, w), BF16) for w in widths],
        compiler_params=pltpu.CompilerParams(
            dimension_semantics=("arbitrary",), vmem_limit_bytes=VMEM_LIMIT_BYTES),
        name="proj",
    )(x_rows, g, w_bf16, *tables, gains, seg)


def _attend(qs, k_ref, v_ref, km_ref, vm_ref, bias_ref):
    n_chunks = k_ref.shape[0] // KV_CHUNK
    km, vm, bias = km_ref[...], vm_ref[...], bias_ref[...]
    init = []
    for q in qs:
        s = _dot_nt(q, km) + bias
        m = jnp.max(s, axis=-1, keepdims=True)
        p = jnp.exp2(s - m)
        init.append((m, jnp.sum(p, axis=-1, keepdims=True), _dot(p.astype(BF16), vm)))

    def body(c, carry):
        start = pl.multiple_of(c * KV_CHUNK, KV_CHUNK)
        k = k_ref[pl.ds(start, KV_CHUNK), :]
        v = v_ref[pl.ds(start, KV_CHUNK), :]
        out = []
        for q, (m, l, acc) in zip(qs, carry):
            s = _dot_nt(q, k)
            m_new = jnp.maximum(m, jnp.max(s, axis=-1, keepdims=True))
            alpha = jnp.exp2(m - m_new)
            p = jnp.exp2(s - m_new)
            l = alpha * l + jnp.sum(p, axis=-1, keepdims=True)
            acc = alpha * acc + _dot(p.astype(BF16), v)
            out.append((m_new, l, acc))
        return tuple(out)

    final = lax.fori_loop(0, n_chunks, body, tuple(init))
    return [acc / l for (_, l, acc) in final]


def _attn_a_kernel(q_ref, k_ref, v_ref, km_ref, vm_ref, bias_ref, o_ref):
    tq = q_ref.shape[0]
    qs = []
    for kv in range(A_KV_HEADS):
        heads = range(kv * A_GROUP, (kv + 1) * A_GROUP)
        qs.append(jnp.concatenate([q_ref[:, h * LANES:(h + 1) * LANES] for h in heads], axis=0))
    outs = _attend(qs, k_ref, v_ref, km_ref, vm_ref, bias_ref)
    pieces = []
    for kv in range(A_KV_HEADS):
        o = outs[kv][:, kv * HEAD_DIM:(kv + 1) * HEAD_DIM]
        pieces += [o[i * tq:(i + 1) * tq] for i in range(A_GROUP)]
    o_ref[...] = jnp.concatenate(pieces, axis=1).astype(o_ref.dtype)


def _attn_b_kernel(q_ref, k_ref, v_ref, km_ref, vm_ref, bias_ref, lam_ref, subln_ref, o_ref):
    tq = q_ref.shape[0]
    q = q_ref[...]
    low_half = _lane_index(q.shape) < HEAD_DIM
    zero = jnp.zeros_like(q)
    q_stack = jnp.concatenate([jnp.where(low_half, q, zero), jnp.where(low_half, zero, q)], axis=0)
    (o,) = _attend([q_stack], k_ref, v_ref, km_ref, vm_ref, bias_ref)
    lam_p = lam_ref[...]
    lam = (jnp.exp(jnp.sum(lam_p[0:1] * lam_p[1:2], axis=-1, keepdims=True))
           - jnp.exp(jnp.sum(lam_p[2:3] * lam_p[3:4], axis=-1, keepdims=True)) + LAM_INIT)
    d = o[:tq] - lam * o[tq:]
    ms = jnp.mean(d * d, axis=-1, keepdims=True)
    y = d * lax.rsqrt(ms + EPS) * subln_ref[...] * (1.0 - LAM_INIT)
    o_ref[...] = y.astype(o_ref.dtype)


def _attention_a(qa, ka, va, kam, vam, bias):
    bsz, seq, _ = qa.shape
    kv_blk = pl.BlockSpec((None, seq, LANES), lambda b, i: (b, 0, 0))
    const = lambda shape: pl.BlockSpec(shape, lambda b, i: (0, 0))
    return pl.pallas_call(
        _attn_a_kernel,
        grid=(bsz, seq // A_Q_ROWS),
        in_specs=[pl.BlockSpec((None, A_Q_ROWS, A_HEADS * LANES), lambda b, i: (b, i, 0)),
                  kv_blk, kv_blk, const(kam.shape), const(vam.shape), const(bias.shape)],
        out_specs=pl.BlockSpec((None, A_Q_ROWS, A_WIDTH), lambda b, i: (b, i, 0)),
        out_shape=jax.ShapeDtypeStruct((bsz, seq, A_WIDTH), BF16),
        compiler_params=pltpu.CompilerParams(
            dimension_semantics=("arbitrary", "arbitrary"), vmem_limit_bytes=VMEM_LIMIT_BYTES),
        name="attn_a",
    )(qa, ka, va, kam, vam, bias)


def _attention_b(qb, kb, vb, kbm, vbm, bias, lam_params, subln):
    bsz, seq, _ = qb.shape
    head_blk = lambda rows: pl.BlockSpec((None, rows, LANES), lambda b, h, i: (b, 0, h))
    meta_blk = pl.BlockSpec((LANES, LANES), lambda b, h, i: (0, h))
    const = lambda shape: pl.BlockSpec(shape, lambda b, h, i: (0, 0))
    q_blk = pl.BlockSpec((None, B_Q_ROWS, LANES), lambda b, h, i: (b, i, h))
    return pl.pallas_call(
        _attn_b_kernel,
        grid=(bsz, B_HEADS, seq // B_Q_ROWS),
        in_specs=[q_blk, head_blk(seq), head_blk(seq), meta_blk, meta_blk,
                  const(bias.shape), const(lam_params.shape), const(subln.shape)],
        out_specs=q_blk,
        out_shape=jax.ShapeDtypeStruct((bsz, seq, B_WIDTH), BF16),
        compiler_params=pltpu.CompilerParams(
            dimension_semantics=("arbitrary", "arbitrary", "arbitrary"),
            vmem_limit_bytes=VMEM_LIMIT_BYTES),
        name="attn_b",
    )(qb, kb, vb, kbm, vbm, bias, lam_params, subln)


def _post_kernel(x_ref, oa_ref, ob_ref, woa_ref, wob_ref, g_ref, wg_ref, wu_ref, wd_ref, y_ref):
    x1 = x_ref[...] + _dot(oa_ref[...], woa_ref[...]) + _dot(ob_ref[...], wob_ref[...])
    ms = jnp.mean(x1 * x1, axis=-1, keepdims=True)
    h = (x1 * lax.rsqrt(ms + EPS) * g_ref[...]).astype(BF16)
    gate = _dot(h, wg_ref[...])
    up = _dot(h, wu_ref[...])
    act = (gate * jax.nn.sigmoid(gate) * up).astype(BF16)
    y_ref[...] = x1 + _dot(act, wd_ref[...])


def _post(x_rows, oa, ob, wo_a, wo_b, g, wg, wu, wd):
    n_rows, d_model = x_rows.shape
    row_blk = lambda width: pl.BlockSpec((POST_ROWS, width), lambda i: (i, 0))
    resident = lambda a: pl.BlockSpec(a.shape, lambda i: (0, 0), pipeline_mode=pl.Buffered(1))
    return pl.pallas_call(
        _post_kernel,
        grid=(n_rows // POST_ROWS,),
        in_specs=[row_blk(d_model), row_blk(A_WIDTH), row_blk(B_WIDTH),
                  resident(wo_a), resident(wo_b), resident(g),
                  resident(wg), resident(wu), resident(wd)],
        out_specs=row_blk(d_model),
        out_shape=jax.ShapeDtypeStruct((n_rows, d_model), F32),
        compiler_params=pltpu.CompilerParams(
            dimension_semantics=("arbitrary",), vmem_limit_bytes=VMEM_LIMIT_BYTES),
        name="post",
    )(x_rows, oa, ob, wo_a, wo_b, g, wg, wu, wd)


def _pair_tables(ang):
    c, s = jnp.cos(ang), jnp.sin(ang)
    cos64 = jnp.repeat(c, 2, axis=-1)
    sin64 = jnp.stack([-s, s], axis=-1).reshape(ang.shape[0], HEAD_DIM)
    return jnp.tile(cos64, (1, 2)), jnp.tile(sin64, (1, 2))


def _rope_tables(seq):
    half = HEAD_DIM // 2
    t = jnp.arange(seq)
    inv_a = ROPE_THETA ** (-jnp.arange(0, half, 2, dtype=F32) / half)
    r = (t // GRID_W).astype(F32)
    c = (t % GRID_W).astype(F32)
    ang_a = jnp.concatenate([r[:, None] * inv_a[None, :], c[:, None] * inv_a[None, :]], axis=-1)
    inv_b = ROPE_THETA ** (-jnp.arange(0, HEAD_DIM, 2, dtype=F32) / HEAD_DIM)
    pos = jnp.arange(N_META + seq, dtype=F32)
    ang_b = pos[:, None] * inv_b[None, :]
    seq_tabs = _pair_tables(ang_a) + _pair_tables(ang_b[N_META:])
    meta_tabs = _pair_tables(jnp.zeros((N_META, half), F32)) + _pair_tables(ang_b[:N_META])
    return seq_tabs, meta_tabs


def _pad_rows(a, rows):
    return jnp.pad(a, ((0, rows - a.shape[0]), (0, 0)))


def kernel(x, meta_tokens, attn_norm_g, w_in, a_q_norm_g, a_k_norm_g, b_q_norm_g, b_k_norm_g,
           b_lambda_q1, b_lambda_k1, b_lambda_q2, b_lambda_k2, b_subln_g, w_out,
           ffn_norm_g, w_gate, w_up, w_down):
    bsz, seq, d_model = x.shape
    assert w_in.shape[0] == 1, "single-layer block"
    assert seq % PROJ_ROWS == 0 and seq % KV_CHUNK == 0 and seq % A_Q_ROWS == 0 and seq % B_Q_ROWS == 0
    layer = 0
    x_rows = x.reshape(bsz * seq, d_model)

    seq_tabs, meta_tabs = _rope_tables(seq)
    gains = jnp.stack([jnp.tile(g[layer].astype(F32), 2)
                       for g in (a_q_norm_g, a_k_norm_g, b_q_norm_g, b_k_norm_g)])
    idx = jnp.arange(LANES) // HEAD_DIM
    seg = (idx[:, None] == idx[None, :]).astype(BF16)
    w_in_b = w_in[layer].astype(BF16)
    g_attn = attn_norm_g[layer].astype(F32)[None, :]

    qa, ka, va, qb, kb, vb = _project(x_rows, g_attn, w_in_b, seq_tabs, gains, seg,
                                      PROJ_ROWS, seq // PROJ_ROWS)
    _, kam, vam, _, kbm, vbm = _project(meta_tokens.astype(F32), g_attn, w_in_b, meta_tabs,
                                        gains, seg, N_META, 1)
    kam, vam, kbm, vbm = (_pad_rows(a, LANES) for a in (kam, vam, kbm, vbm))
    bias = jnp.where(jnp.arange(LANES) < N_META, 0.0, MASK_BIAS).astype(F32)[None, :]

    shaped = lambda a: a.reshape(bsz, seq, a.shape[-1])
    out_a = _attention_a(shaped(qa), shaped(ka), shaped(va), kam, vam, bias)

    lam_params = jnp.stack([jnp.pad(p[layer].astype(F32), (0, LANES - HEAD_DIM))
                            for p in (b_lambda_q1, b_lambda_k1, b_lambda_q2, b_lambda_k2)])
    subln = b_subln_g[layer].astype(F32)[None, :]
    out_b = _attention_b(shaped(qb), shaped(kb), shaped(vb), kbm, vbm, bias, lam_params, subln)

    w_out_b = w_out[layer].astype(BF16)
    y = _post(x_rows, out_a.reshape(bsz * seq, A_WIDTH), out_b.reshape(bsz * seq, B_WIDTH),
              w_out_b[:A_WIDTH], w_out_b[A_WIDTH:], ffn_norm_g[layer].astype(F32)[None, :],
              w_gate[layer].astype(BF16), w_up[layer].astype(BF16), w_down[layer].astype(BF16))
    return y.reshape(bsz, seq, d_model)
```

```python
import math

import jax
import jax.numpy as jnp
from jax import lax
from jax.experimental import pallas as pl
from jax.experimental.pallas import tpu as pltpu

N_META = 16
GRID_W = 64
HEAD_DIM = 64
ROPE_THETA = 10000.0
EPS = 1e-6
A_HEADS = 8
A_KV_HEADS = 2
A_GROUP = A_HEADS // A_KV_HEADS
A_WIDTH = A_HEADS * HEAD_DIM
A_KV_WIDTH = A_KV_HEADS * HEAD_DIM
B_HEADS = 4
B_VDIM = 2 * HEAD_DIM
B_WIDTH = B_HEADS * B_VDIM
LAM_INIT = 0.8 - 0.6 * math.exp(-0.3 * 0)

LANES = 128
VMEM_LIMIT_BYTES = 56 * 1024 * 1024

Q_PRESCALE = (HEAD_DIM ** -0.5) * math.log2(math.e)
MASK_BIAS = -1e30

PROJ_ROWS = 512
META_ROWS = LANES
A_Q_COLS = 128
B_Q_COLS = 256
KV_CHUNK = 512
POST_ROWS = 512

BF16 = jnp.bfloat16
F32 = jnp.float32


def _dot(a, b):
    return jnp.dot(a, b, preferred_element_type=F32)


def _lane_index(shape):
    return lax.broadcasted_iota(jnp.int32, shape, len(shape) - 1)


def _row_index(shape):
    return lax.broadcasted_iota(jnp.int32, shape, 0)


def _proj_kernel(x_ref, g_ref, w_ref, cos_a_ref, sin_a_ref, cos_b_ref, sin_b_ref,
                 gains_ref, seg_ref,
                 qa_ref, ka_ref, va_ref, qb_ref, kb_ref, vb_ref):
    x = x_ref[...]
    ms = jnp.mean(x * x, axis=-1, keepdims=True)
    h = (x * lax.rsqrt(ms + EPS) * g_ref[...]).astype(BF16)
    proj = _dot(h, w_ref[...])

    rows = x.shape[0]
    lane = _lane_index((rows, LANES))
    even = (lane % 2) == 0
    low_half = lane < HEAD_DIM
    seg = seg_ref[...]

    def norm_rope(col, gain, cos, sin):
        ss = _dot((col * col).astype(BF16), seg)
        y = col * lax.rsqrt(ss * (1.0 / HEAD_DIM) + EPS) * gain
        swapped = jnp.where(even, pltpu.roll(y, LANES - 1, 1), pltpu.roll(y, 1, 1))
        return y * cos + swapped * sin

    cos_a, sin_a = cos_a_ref[...], sin_a_ref[...]
    cos_b, sin_b = cos_b_ref[...], sin_b_ref[...]
    g_aq = gains_ref[0:1, :] * Q_PRESCALE
    g_ak = gains_ref[1:2, :]
    g_bq = gains_ref[2:3, :] * Q_PRESCALE
    g_bk = gains_ref[3:4, :]

    off = 0
    for j in range(A_WIDTH // LANES):
        col = norm_rope(proj[:, off + j * LANES: off + (j + 1) * LANES], g_aq, cos_a, sin_a)
        flipped = pltpu.roll(col, HEAD_DIM, 1)
        for half in range(2):
            head = 2 * j + half
            kv = head // A_GROUP
            src = col if half == kv else flipped
            keep = low_half if kv == 0 else jnp.logical_not(low_half)
            qa_ref[head] = jnp.where(keep, src, 0.0).T.astype(BF16)
    off += A_WIDTH
    ka_ref[...] = norm_rope(proj[:, off:off + LANES], g_ak, cos_a, sin_a).astype(BF16)
    off += A_KV_WIDTH
    va_ref[...] = proj[:, off:off + LANES].T.astype(BF16)
    off += A_KV_WIDTH
    for j in range(B_HEADS):
        col = proj[:, off + j * LANES: off + (j + 1) * LANES]
        qb_ref[j] = norm_rope(col, g_bq, cos_b, sin_b).T.astype(BF16)
    off += B_WIDTH
    for j in range(B_HEADS):
        col = proj[:, off + j * LANES: off + (j + 1) * LANES]
        kb_ref[:, j * LANES:(j + 1) * LANES] = norm_rope(col, g_bk, cos_b, sin_b).astype(BF16)
    off += B_WIDTH
    for j in range(B_HEADS):
        col = proj[:, off + j * LANES: off + (j + 1) * LANES]
        vb_ref[j * LANES:(j + 1) * LANES, :] = col.T.astype(BF16)


def _project(x3, g, w_bf16, tables, gains, seg, rows_per_step):
    bsz, n_rows, d_model = x3.shape
    in_width = w_bf16.shape[1]
    t = rows_per_step
    tab_blk = pl.BlockSpec((t, LANES), lambda b, i: (i, 0))
    const = lambda shape: pl.BlockSpec(shape, lambda b, i: (0, 0))
    out_specs = [
        pl.BlockSpec((None, A_HEADS, LANES, t), lambda b, i: (b, 0, 0, i)),
        pl.BlockSpec((None, t, LANES), lambda b, i: (b, i, 0)),
        pl.BlockSpec((None, LANES, t), lambda b, i: (b, 0, i)),
        pl.BlockSpec((None, B_HEADS, LANES, t), lambda b, i: (b, 0, 0, i)),
        pl.BlockSpec((None, t, B_WIDTH), lambda b, i: (b, i, 0)),
        pl.BlockSpec((None, B_WIDTH, t), lambda b, i: (b, 0, i)),
    ]
    out_shapes = [(bsz, A_HEADS, LANES, n_rows), (bsz, n_rows, LANES), (bsz, LANES, n_rows),
                  (bsz, B_HEADS, LANES, n_rows), (bsz, n_rows, B_WIDTH), (bsz, B_WIDTH, n_rows)]
    return pl.pallas_call(
        _proj_kernel,
        grid=(bsz, n_rows // t),
        in_specs=[pl.BlockSpec((None, t, d_model), lambda b, i: (b, i, 0)),
                  const((1, d_model)), const((d_model, in_width)),
                  tab_blk, tab_blk, tab_blk, tab_blk, const(gains.shape), const(seg.shape)],
        out_specs=out_specs,
        out_shape=[jax.ShapeDtypeStruct(s, BF16) for s in out_shapes],
        compiler_params=pltpu.CompilerParams(
            dimension_semantics=("arbitrary", "arbitrary"), vmem_limit_bytes=VMEM_LIMIT_BYTES),
        name="proj",
    )(x3, g, w_bf16, *tables, gains, seg)


def _attend_t(streams, k_ref, km_ref, bias_ref):
    n_chunks = k_ref.shape[0] // KV_CHUNK
    items = [(i, None) for i in range(len(streams))]
    items += [(i, c) for c in range(n_chunks) for i in range(len(streams))]

    def scores(item):
        i, c = item
        qt = streams[i][0]
        if c is None:
            bias = jnp.concatenate([bias_ref[...]] * (qt.shape[1] // LANES), axis=1)
            return _dot(km_ref[...], qt) + bias
        return _dot(k_ref[c * KV_CHUNK:(c + 1) * KV_CHUNK, :], qt)

    state = [None] * len(streams)
    s_next = scores(items[0])
    for pos, (i, c) in enumerate(items):
        s = s_next
        if pos + 1 < len(items):
            s_next = scores(items[pos + 1])
        _, vt_ref, vmt = streams[i]
        vt = vmt if c is None else vt_ref[:, c * KV_CHUNK:(c + 1) * KV_CHUNK]
        s_max = jnp.max(s, axis=0, keepdims=True)
        if state[i] is None:
            p = jnp.exp2(s - s_max)
            state[i] = (s_max, jnp.sum(p, axis=0, keepdims=True), _dot(vt, p.astype(BF16)))
        else:
            m, l, acc = state[i]
            m_new = jnp.maximum(m, s_max)
            alpha = jnp.exp2(m - m_new)
            p = jnp.exp2(s - m_new)
            l = alpha * l + jnp.sum(p, axis=0, keepdims=True)
            acc = alpha * acc + _dot(vt, p.astype(BF16))
            state[i] = (m_new, l, acc)
    return [acc / l for (_, l, acc) in state]


def _attn_a_kernel(q_ref, k_ref, vt_ref, km_ref, vmt_ref, bias_ref, o_ref):
    tq = q_ref.shape[2]
    streams = []
    for kv in range(A_KV_HEADS):
        heads = range(kv * A_GROUP, (kv + 1) * A_GROUP)
        qt = jnp.concatenate([q_ref[h] for h in heads], axis=1)
        rows = slice(kv * HEAD_DIM, (kv + 1) * HEAD_DIM)
        streams.append((qt, vt_ref.at[rows, :], vmt_ref[rows, :]))
    outs = _attend_t(streams, k_ref, km_ref, bias_ref)
    pieces = [outs[kv][:, i * tq:(i + 1) * tq] for kv in range(A_KV_HEADS) for i in range(A_GROUP)]
    o_ref[...] = jnp.concatenate(pieces, axis=0).T.astype(o_ref.dtype)


def _attn_b_kernel(q_ref, k_ref, vt_ref, km_ref, vmt_ref, bias_ref, lam_ref, subln_ref, o_ref):
    tq = q_ref.shape[1]
    q = q_ref[...]
    first = _row_index(q.shape) < HEAD_DIM
    zero = jnp.zeros_like(q)
    qt = jnp.concatenate([jnp.where(first, q, zero), jnp.where(first, zero, q)], axis=1)
    (o,) = _attend_t([(qt, vt_ref, vmt_ref[...])], k_ref, km_ref, bias_ref)
    lam_p = lam_ref[...]
    lam = (jnp.exp(jnp.sum(lam_p[0:1] * lam_p[1:2], axis=-1, keepdims=True))
           - jnp.exp(jnp.sum(lam_p[2:3] * lam_p[3:4], axis=-1, keepdims=True)) + LAM_INIT)
    d = o[:, :tq] - lam * o[:, tq:]
    ms = jnp.mean(d * d, axis=0, keepdims=True)
    y = d * lax.rsqrt(ms + EPS) * subln_ref[...] * (1.0 - LAM_INIT)
    o_ref[...] = y.T.astype(o_ref.dtype)


def _attention_a(qat, ka, vat, kam, vamt, bias):
    bsz, _, _, seq = qat.shape
    const = lambda shape: pl.BlockSpec(shape, lambda b, i: (0,) * len(shape))
    return pl.pallas_call(
        _attn_a_kernel,
        grid=(bsz, seq // A_Q_COLS),
        in_specs=[pl.BlockSpec((None, A_HEADS, LANES, A_Q_COLS), lambda b, i: (b, 0, 0, i)),
                  pl.BlockSpec((None, seq, LANES), lambda b, i: (b, 0, 0)),
                  pl.BlockSpec((None, LANES, seq), lambda b, i: (b, 0, 0)),
                  const(kam.shape), const(vamt.shape), const(bias.shape)],
        out_specs=pl.BlockSpec((None, A_Q_COLS, A_WIDTH), lambda b, i: (b, i, 0)),
        out_shape=jax.ShapeDtypeStruct((bsz, seq, A_WIDTH), BF16),
        compiler_params=pltpu.CompilerParams(
            dimension_semantics=("arbitrary", "arbitrary"), vmem_limit_bytes=VMEM_LIMIT_BYTES),
        name="attn_a",
    )(qat, ka, vat, kam, vamt, bias)


def _attention_b(qbt, kb, vbt, kbm, vbmt, bias, lam_params, subln_col):
    bsz, _, _, seq = qbt.shape
    const = lambda shape: pl.BlockSpec(shape, lambda b, h, i: (0,) * len(shape))
    return pl.pallas_call(
        _attn_b_kernel,
        grid=(bsz, B_HEADS, seq // B_Q_COLS),
        in_specs=[pl.BlockSpec((None, None, LANES, B_Q_COLS), lambda b, h, i: (b, h, 0, i)),
                  pl.BlockSpec((None, seq, LANES), lambda b, h, i: (b, 0, h)),
                  pl.BlockSpec((None, LANES, seq), lambda b, h, i: (b, h, 0)),
                  pl.BlockSpec((LANES, LANES), lambda b, h, i: (0, h)),
                  pl.BlockSpec((LANES, LANES), lambda b, h, i: (h, 0)),
                  const(bias.shape), const(lam_params.shape), const(subln_col.shape)],
        out_specs=pl.BlockSpec((None, B_Q_COLS, LANES), lambda b, h, i: (b, i, h)),
        out_shape=jax.ShapeDtypeStruct((bsz, seq, B_WIDTH), BF16),
        compiler_params=pltpu.CompilerParams(
            dimension_semantics=("arbitrary", "arbitrary", "arbitrary"),
            vmem_limit_bytes=VMEM_LIMIT_BYTES),
        name="attn_b",
    )(qbt, kb, vbt, kbm, vbmt, bias, lam_params, subln_col)


def _post_kernel(x_ref, oa_ref, ob_ref, woa_ref, wob_ref, g_ref, wg_ref, wu_ref, wd_ref, y_ref):
    x1 = x_ref[...] + _dot(oa_ref[...], woa_ref[...]) + _dot(ob_ref[...], wob_ref[...])
    ms = jnp.mean(x1 * x1, axis=-1, keepdims=True)
    h = (x1 * lax.rsqrt(ms + EPS) * g_ref[...]).astype(BF16)
    gate = _dot(h, wg_ref[...])
    up = _dot(h, wu_ref[...])
    act = (gate * jax.nn.sigmoid(gate) * up).astype(BF16)
    y_ref[...] = x1 + _dot(act, wd_ref[...])


def _post(x_rows, oa, ob, wo_a, wo_b, g, wg, wu, wd):
    n_rows, d_model = x_rows.shape
    row_blk = lambda width: pl.BlockSpec((POST_ROWS, width), lambda i: (i, 0))
    resident = lambda a: pl.BlockSpec(a.shape, lambda i: (0, 0), pipeline_mode=pl.Buffered(1))
    return pl.pallas_call(
        _post_kernel,
        grid=(n_rows // POST_ROWS,),
        in_specs=[row_blk(d_model), row_blk(A_WIDTH), row_blk(B_WIDTH),
                  resident(wo_a), resident(wo_b), resident(g),
                  resident(wg), resident(wu), resident(wd)],
        out_specs=row_blk(d_model),
        out_shape=jax.ShapeDtypeStruct((n_rows, d_model), F32),
        compiler_params=pltpu.CompilerParams(
            dimension_semantics=("arbitrary",), vmem_limit_bytes=VMEM_LIMIT_BYTES),
        name="post",
    )(x_rows, oa, ob, wo_a, wo_b, g, wg, wu, wd)


def _pair_tables(ang):
    c, s = jnp.cos(ang), jnp.sin(ang)
    cos64 = jnp.repeat(c, 2, axis=-1)
    sin64 = jnp.stack([-s, s], axis=-1).reshape(ang.shape[0], HEAD_DIM)
    return jnp.tile(cos64, (1, 2)), jnp.tile(sin64, (1, 2))


def _rope_tables(seq):
    half = HEAD_DIM // 2
    t = jnp.arange(seq)
    inv_a = ROPE_THETA ** (-jnp.arange(0, half, 2, dtype=F32) / half)
    r = (t // GRID_W).astype(F32)
    c = (t % GRID_W).astype(F32)
    ang_a = jnp.concatenate([r[:, None] * inv_a[None, :], c[:, None] * inv_a[None, :]], axis=-1)
    inv_b = ROPE_THETA ** (-jnp.arange(0, HEAD_DIM, 2, dtype=F32) / HEAD_DIM)
    pos = jnp.arange(N_META + seq, dtype=F32)
    ang_b = pos[:, None] * inv_b[None, :]
    seq_tabs = _pair_tables(ang_a) + _pair_tables(ang_b[N_META:])
    meta_tabs = _pair_tables(jnp.zeros((N_META, half), F32)) + _pair_tables(ang_b[:N_META])
    return seq_tabs, tuple(_pad_rows(t, META_ROWS) for t in meta_tabs)


def _pad_rows(a, rows):
    return jnp.pad(a, ((0, rows - a.shape[0]), (0, 0)))


def kernel(x, meta_tokens, attn_norm_g, w_in, a_q_norm_g, a_k_norm_g, b_q_norm_g, b_k_norm_g,
           b_lambda_q1, b_lambda_k1, b_lambda_q2, b_lambda_k2, b_subln_g, w_out,
           ffn_norm_g, w_gate, w_up, w_down):
    bsz, seq, d_model = x.shape
    assert w_in.shape[0] == 1, "single-layer block"
    assert seq % PROJ_ROWS == 0 and seq % KV_CHUNK == 0 and seq % A_Q_COLS == 0 and seq % B_Q_COLS == 0
    layer = 0

    seq_tabs, meta_tabs = _rope_tables(seq)
    gains = jnp.stack([jnp.tile(g[layer].astype(F32), 2)
                       for g in (a_q_norm_g, a_k_norm_g, b_q_norm_g, b_k_norm_g)])
    idx = jnp.arange(LANES) // HEAD_DIM
    seg = (idx[:, None] == idx[None, :]).astype(BF16)
    w_in_b = w_in[layer].astype(BF16)
    g_attn = attn_norm_g[layer].astype(F32)[None, :]

    qat, ka, vat, qbt, kb, vbt = _project(x, g_attn, w_in_b, seq_tabs, gains, seg, PROJ_ROWS)
    meta3 = _pad_rows(meta_tokens.astype(F32), META_ROWS)[None]
    _, kam, vamt, _, kbm, vbmt = _project(meta3, g_attn, w_in_b, meta_tabs, gains, seg, META_ROWS)
    kam, vamt, kbm, vbmt = kam[0], vamt[0], kbm[0], vbmt[0]
    bias = jnp.where(jnp.arange(META_ROWS) < N_META, 0.0, MASK_BIAS).astype(F32)
    bias = jnp.broadcast_to(bias[:, None], (META_ROWS, LANES))

    out_a = _attention_a(qat, ka, vat, kam, vamt, bias)

    lam_params = jnp.stack([jnp.pad(p[layer].astype(F32), (0, LANES - HEAD_DIM))
                            for p in (b_lambda_q1, b_lambda_k1, b_lambda_q2, b_lambda_k2)])
    subln_col = b_subln_g[layer].astype(F32)[:, None]
    out_b = _attention_b(qbt, kb, vbt, kbm, vbmt, bias, lam_params, subln_col)

    w_out_b = w_out[layer].astype(BF16)
    y = _post(x.reshape(bsz * seq, d_model), out_a.reshape(bsz * seq, A_WIDTH),
              out_b.reshape(bsz * seq, B_WIDTH), w_out_b[:A_WIDTH], w_out_b[A_WIDTH:],
              ffn_norm_g[layer].astype(F32)[None, :],
              w_gate[layer].astype(BF16), w_up[layer].astype(BF16), w_down[layer].astype(BF16))
    return y.reshape(bsz, seq, d_model)
```

```python
import math

import jax
import jax.numpy as jnp
from jax import lax
from jax.experimental import pallas as pl
from jax.experimental.pallas import tpu as pltpu

N_META = 16
GRID_W = 64
HEAD_DIM = 64
ROPE_THETA = 10000.0
EPS = 1e-6
A_HEADS = 8
A_KV_HEADS = 2
A_GROUP = A_HEADS // A_KV_HEADS
A_WIDTH = A_HEADS * HEAD_DIM
A_KV_WIDTH = A_KV_HEADS * HEAD_DIM
B_HEADS = 4
B_VDIM = 2 * HEAD_DIM
B_WIDTH = B_HEADS * B_VDIM
LAM_INIT = 0.8 - 0.6 * math.exp(-0.3 * 0)

LANES = 128
VMEM_LIMIT_BYTES = 56 * 1024 * 1024

Q_PRESCALE = (HEAD_DIM ** -0.5) * math.log2(math.e)
MASK_BIAS = -1e30

PROJ_ROWS = 512
META_ROWS = LANES
A_Q_COLS = 128
B_Q_COLS = 256
B_HEADS_PER_STEP = 2
KV_CHUNK = 512
POST_ROWS = 512

BF16 = jnp.bfloat16
F32 = jnp.float32


def _dot(a, b):
    return jnp.dot(a, b, preferred_element_type=F32)


def _lane_index(shape):
    return lax.broadcasted_iota(jnp.int32, shape, len(shape) - 1)


def _row_index(shape):
    return lax.broadcasted_iota(jnp.int32, shape, 0)


def _proj_kernel(x_ref, g_ref, w_ref, cos_a_ref, sin_a_ref, cos_b_ref, sin_b_ref,
                 gains_ref, seg_ref,
                 qa_ref, ka_ref, va_ref, qb_ref, kb_ref, vb_ref):
    x = x_ref[...]
    ms = jnp.mean(x * x, axis=-1, keepdims=True)
    h = (x * lax.rsqrt(ms + EPS) * g_ref[...]).astype(BF16)
    proj = _dot(h, w_ref[...])

    rows = x.shape[0]
    lane = _lane_index((rows, LANES))
    even = (lane % 2) == 0
    low_half = lane < HEAD_DIM
    seg = seg_ref[...]

    def norm_rope(col, gain, cos, sin):
        ss = _dot((col * col).astype(BF16), seg)
        y = col * lax.rsqrt(ss * (1.0 / HEAD_DIM) + EPS) * gain
        swapped = jnp.where(even, pltpu.roll(y, LANES - 1, 1), pltpu.roll(y, 1, 1))
        return y * cos + swapped * sin

    cos_a, sin_a = cos_a_ref[...], sin_a_ref[...]
    cos_b, sin_b = cos_b_ref[...], sin_b_ref[...]
    g_aq = gains_ref[0:1, :] * Q_PRESCALE
    g_ak = gains_ref[1:2, :]
    g_bq = gains_ref[2:3, :] * Q_PRESCALE
    g_bk = gains_ref[3:4, :]

    off = 0
    for j in range(A_WIDTH // LANES):
        col = norm_rope(proj[:, off + j * LANES: off + (j + 1) * LANES], g_aq, cos_a, sin_a)
        flipped = pltpu.roll(col, HEAD_DIM, 1)
        for half in range(2):
            head = 2 * j + half
            kv = head // A_GROUP
            src = col if half == kv else flipped
            keep = low_half if kv == 0 else jnp.logical_not(low_half)
            qa_ref[head] = jnp.where(keep, src, 0.0).T.astype(BF16)
    off += A_WIDTH
    ka_ref[...] = norm_rope(proj[:, off:off + LANES], g_ak, cos_a, sin_a).astype(BF16)
    off += A_KV_WIDTH
    va_ref[...] = proj[:, off:off + LANES].T.astype(BF16)
    off += A_KV_WIDTH
    for j in range(B_HEADS):
        col = proj[:, off + j * LANES: off + (j + 1) * LANES]
        qb_ref[j] = norm_rope(col, g_bq, cos_b, sin_b).T.astype(BF16)
    off += B_WIDTH
    for j in range(B_HEADS):
        col = proj[:, off + j * LANES: off + (j + 1) * LANES]
        kb_ref[:, j * LANES:(j + 1) * LANES] = norm_rope(col, g_bk, cos_b, sin_b).astype(BF16)
    off += B_WIDTH
    for j in range(B_HEADS):
        col = proj[:, off + j * LANES: off + (j + 1) * LANES]
        vb_ref[j * LANES:(j + 1) * LANES, :] = col.T.astype(BF16)


def _project(x3, g, w_bf16, tables, gains, seg, rows_per_step):
    bsz, n_rows, d_model = x3.shape
    in_width = w_bf16.shape[1]
    t = rows_per_step
    tab_blk = pl.BlockSpec((t, LANES), lambda b, i: (i, 0))
    const = lambda shape: pl.BlockSpec(shape, lambda b, i: (0, 0))
    out_specs = [
        pl.BlockSpec((None, A_HEADS, LANES, t), lambda b, i: (b, 0, 0, i)),
        pl.BlockSpec((None, t, LANES), lambda b, i: (b, i, 0)),
        pl.BlockSpec((None, LANES, t), lambda b, i: (b, 0, i)),
        pl.BlockSpec((None, B_HEADS, LANES, t), lambda b, i: (b, 0, 0, i)),
        pl.BlockSpec((None, t, B_WIDTH), lambda b, i: (b, i, 0)),
        pl.BlockSpec((None, B_WIDTH, t), lambda b, i: (b, 0, i)),
    ]
    out_shapes = [(bsz, A_HEADS, LANES, n_rows), (bsz, n_rows, LANES), (bsz, LANES, n_rows),
                  (bsz, B_HEADS, LANES, n_rows), (bsz, n_rows, B_WIDTH), (bsz, B_WIDTH, n_rows)]
    return pl.pallas_call(
        _proj_kernel,
        grid=(bsz, n_rows // t),
        in_specs=[pl.BlockSpec((None, t, d_model), lambda b, i: (b, i, 0)),
                  const((1, d_model)), const((d_model, in_width)),
                  tab_blk, tab_blk, tab_blk, tab_blk, const(gains.shape), const(seg.shape)],
        out_specs=out_specs,
        out_shape=[jax.ShapeDtypeStruct(s, BF16) for s in out_shapes],
        compiler_params=pltpu.CompilerParams(
            dimension_semantics=("arbitrary", "arbitrary"), vmem_limit_bytes=VMEM_LIMIT_BYTES),
        name="proj",
    )(x3, g, w_bf16, *tables, gains, seg)


def _attend_t(streams, bias_ref):
    n_chunks = streams[0][1].shape[0] // KV_CHUNK
    items = [(i, None) for i in range(len(streams))]
    items += [(i, c) for c in range(n_chunks) for i in range(len(streams))]

    def scores(item):
        i, c = item
        qt, k_ref, _, km, _ = streams[i]
        if c is None:
            bias = jnp.concatenate([bias_ref[...]] * (qt.shape[1] // LANES), axis=1)
            return _dot(km, qt) + bias
        return _dot(k_ref[c * KV_CHUNK:(c + 1) * KV_CHUNK, :], qt)

    state = [None] * len(streams)
    s_next = scores(items[0])
    for pos, (i, c) in enumerate(items):
        s = s_next
        if pos + 1 < len(items):
            s_next = scores(items[pos + 1])
        _, _, vt_ref, _, vmt = streams[i]
        vt = vmt if c is None else vt_ref[:, c * KV_CHUNK:(c + 1) * KV_CHUNK]
        s_max = jnp.max(s, axis=0, keepdims=True)
        if state[i] is None:
            p = jnp.exp2(s - s_max)
            state[i] = (s_max, jnp.sum(p, axis=0, keepdims=True), _dot(vt, p.astype(BF16)))
        else:
            m, l, acc = state[i]
            m_new = jnp.maximum(m, s_max)
            alpha = jnp.exp2(m - m_new)
            p = jnp.exp2(s - m_new)
            l = alpha * l + jnp.sum(p, axis=0, keepdims=True)
            acc = alpha * acc + _dot(vt, p.astype(BF16))
            state[i] = (m_new, l, acc)
    return [acc / l for (_, l, acc) in state]


def _attn_a_kernel(q_ref, k_ref, vt_ref, km_ref, vmt_ref, bias_ref, o_ref):
    tq = q_ref.shape[2]
    streams = []
    for kv in range(A_KV_HEADS):
        heads = range(kv * A_GROUP, (kv + 1) * A_GROUP)
        qt = jnp.concatenate([q_ref[h] for h in heads], axis=1)
        rows = slice(kv * HEAD_DIM, (kv + 1) * HEAD_DIM)
        streams.append((qt, k_ref, vt_ref.at[rows, :], km_ref[...], vmt_ref[rows, :]))
    outs = _attend_t(streams, bias_ref)
    pieces = [outs[kv][:, i * tq:(i + 1) * tq] for kv in range(A_KV_HEADS) for i in range(A_GROUP)]
    o_ref[...] = jnp.concatenate(pieces, axis=0).T.astype(o_ref.dtype)


def _attn_b_kernel(q_ref, k_ref, vt_ref, km_ref, vmt_ref, bias_ref, lam_ref, subln_ref, o_ref):
    tq = q_ref.shape[2]
    first = _row_index((LANES, tq)) < HEAD_DIM
    zero = jnp.zeros((LANES, tq), BF16)
    streams = []
    for h in range(B_HEADS_PER_STEP):
        q = q_ref[h]
        cols = slice(h * LANES, (h + 1) * LANES)
        qt = jnp.concatenate([jnp.where(first, q, zero), jnp.where(first, zero, q)], axis=1)
        streams.append((qt, k_ref.at[:, cols], vt_ref.at[cols, :], km_ref[:, cols], vmt_ref[cols, :]))
    outs = _attend_t(streams, bias_ref)
    lam_p = lam_ref[...]
    lam = (jnp.exp(jnp.sum(lam_p[0:1] * lam_p[1:2], axis=-1, keepdims=True))
           - jnp.exp(jnp.sum(lam_p[2:3] * lam_p[3:4], axis=-1, keepdims=True)) + LAM_INIT)
    ys = []
    for o in outs:
        d = o[:, :tq] - lam * o[:, tq:]
        ms = jnp.mean(d * d, axis=0, keepdims=True)
        ys.append(d * lax.rsqrt(ms + EPS) * subln_ref[...] * (1.0 - LAM_INIT))
    o_ref[...] = jnp.concatenate(ys, axis=0).T.astype(o_ref.dtype)


def _attention_a(qat, ka, vat, kam, vamt, bias):
    bsz, _, _, seq = qat.shape
    const = lambda shape: pl.BlockSpec(shape, lambda b, i: (0,) * len(shape))
    return pl.pallas_call(
        _attn_a_kernel,
        grid=(bsz, seq // A_Q_COLS),
        in_specs=[pl.BlockSpec((None, A_HEADS, LANES, A_Q_COLS), lambda b, i: (b, 0, 0, i)),
                  pl.BlockSpec((None, seq, LANES), lambda b, i: (b, 0, 0)),
                  pl.BlockSpec((None, LANES, seq), lambda b, i: (b, 0, 0)),
                  const(kam.shape), const(vamt.shape), const(bias.shape)],
        out_specs=pl.BlockSpec((None, A_Q_COLS, A_WIDTH), lambda b, i: (b, i, 0)),
        out_shape=jax.ShapeDtypeStruct((bsz, seq, A_WIDTH), BF16),
        compiler_params=pltpu.CompilerParams(
            dimension_semantics=("arbitrary", "arbitrary"), vmem_limit_bytes=VMEM_LIMIT_BYTES),
        name="attn_a",
    )(qat, ka, vat, kam, vamt, bias)


def _attention_b(qbt, kb, vbt, kbm, vbmt, bias, lam_params, subln_col):
    bsz, _, _, seq = qbt.shape
    hp = B_HEADS_PER_STEP
    width = hp * LANES
    const = lambda shape: pl.BlockSpec(shape, lambda b, h, i: (0,) * len(shape))
    return pl.pallas_call(
        _attn_b_kernel,
        grid=(bsz, B_HEADS // hp, seq // B_Q_COLS),
        in_specs=[pl.BlockSpec((None, hp, LANES, B_Q_COLS), lambda b, h, i: (b, h, 0, i)),
                  pl.BlockSpec((None, seq, width), lambda b, h, i: (b, 0, h)),
                  pl.BlockSpec((None, width, seq), lambda b, h, i: (b, h, 0)),
                  pl.BlockSpec((LANES, width), lambda b, h, i: (0, h)),
                  pl.BlockSpec((width, LANES), lambda b, h, i: (h, 0)),
                  const(bias.shape), const(lam_params.shape), const(subln_col.shape)],
        out_specs=pl.BlockSpec((None, B_Q_COLS, width), lambda b, h, i: (b, i, h)),
        out_shape=jax.ShapeDtypeStruct((bsz, seq, B_WIDTH), BF16),
        compiler_params=pltpu.CompilerParams(
            dimension_semantics=("arbitrary", "arbitrary", "arbitrary"),
            vmem_limit_bytes=VMEM_LIMIT_BYTES),
        name="attn_b",
    )(qbt, kb, vbt, kbm, vbmt, bias, lam_params, subln_col)


def _post_kernel(x_ref, oa_ref, ob_ref, woa_ref, wob_ref, g_ref, wg_ref, wu_ref, wd_ref, y_ref):
    x1 = x_ref[...] + _dot(oa_ref[...], woa_ref[...]) + _dot(ob_ref[...], wob_ref[...])
    ms = jnp.mean(x1 * x1, axis=-1, keepdims=True)
    h = (x1 * lax.rsqrt(ms + EPS) * g_ref[...]).astype(BF16)
    gate = _dot(h, wg_ref[...])
    up = _dot(h, wu_ref[...])
    act = (gate * jax.nn.sigmoid(gate) * up).astype(BF16)
    y_ref[...] = x1 + _dot(act, wd_ref[...])


def _post(x_rows, oa, ob, wo_a, wo_b, g, wg, wu, wd):
    n_rows, d_model = x_rows.shape
    row_blk = lambda width: pl.BlockSpec((POST_ROWS, width), lambda i: (i, 0))
    resident = lambda a: pl.BlockSpec(a.shape, lambda i: (0, 0), pipeline_mode=pl.Buffered(1))
    return pl.pallas_call(
        _post_kernel,
        grid=(n_rows // POST_ROWS,),
        in_specs=[row_blk(d_model), row_blk(A_WIDTH), row_blk(B_WIDTH),
                  resident(wo_a), resident(wo_b), resident(g),
                  resident(wg), resident(wu), resident(wd)],
        out_specs=row_blk(d_model),
        out_shape=jax.ShapeDtypeStruct((n_rows, d_model), F32),
        compiler_params=pltpu.CompilerParams(
            dimension_semantics=("arbitrary",), vmem_limit_bytes=VMEM_LIMIT_BYTES),
        name="post",
    )(x_rows, oa, ob, wo_a, wo_b, g, wg, wu, wd)


def _pair_tables(ang):
    c, s = jnp.cos(ang), jnp.sin(ang)
    cos64 = jnp.repeat(c, 2, axis=-1)
    sin64 = jnp.stack([-s, s], axis=-1).reshape(ang.shape[0], HEAD_DIM)
    return jnp.tile(cos64, (1, 2)), jnp.tile(sin64, (1, 2))


def _rope_tables(seq):
    half = HEAD_DIM // 2
    t = jnp.arange(seq)
    inv_a = ROPE_THETA ** (-jnp.arange(0, half, 2, dtype=F32) / half)
    r = (t // GRID_W).astype(F32)
    c = (t % GRID_W).astype(F32)
    ang_a = jnp.concatenate([r[:, None] * inv_a[None, :], c[:, None] * inv_a[None, :]], axis=-1)
    inv_b = ROPE_THETA ** (-jnp.arange(0, HEAD_DIM, 2, dtype=F32) / HEAD_DIM)
    pos = jnp.arange(N_META + seq, dtype=F32)
    ang_b = pos[:, None] * inv_b[None, :]
    seq_tabs = _pair_tables(ang_a) + _pair_tables(ang_b[N_META:])
    meta_tabs = _pair_tables(jnp.zeros((N_META, half), F32)) + _pair_tables(ang_b[:N_META])
    return seq_tabs, tuple(_pad_rows(t, META_ROWS) for t in meta_tabs)


def _pad_rows(a, rows):
    return jnp.pad(a, ((0, rows - a.shape[0]), (0, 0)))


def kernel(x, meta_tokens, attn_norm_g, w_in, a_q_norm_g, a_k_norm_g, b_q_norm_g, b_k_norm_g,
           b_lambda_q1, b_lambda_k1, b_lambda_q2, b_lambda_k2, b_subln_g, w_out,
           ffn_norm_g, w_gate, w_up, w_down):
    bsz, seq, d_model = x.shape
    assert w_in.shape[0] == 1, "single-layer block"
    assert seq % PROJ_ROWS == 0 and seq % KV_CHUNK == 0 and seq % A_Q_COLS == 0 and seq % B_Q_COLS == 0
    layer = 0

    seq_tabs, meta_tabs = _rope_tables(seq)
    gains = jnp.stack([jnp.tile(g[layer].astype(F32), 2)
                       for g in (a_q_norm_g, a_k_norm_g, b_q_norm_g, b_k_norm_g)])
    idx = jnp.arange(LANES) // HEAD_DIM
    seg = (idx[:, None] == idx[None, :]).astype(BF16)
    w_in_b = w_in[layer].astype(BF16)
    g_attn = attn_norm_g[layer].astype(F32)[None, :]

    qat, ka, vat, qbt, kb, vbt = _project(x, g_attn, w_in_b, seq_tabs, gains, seg, PROJ_ROWS)
    meta3 = _pad_rows(meta_tokens.astype(F32), META_ROWS)[None]
    _, kam, vamt, _, kbm, vbmt = _project(meta3, g_attn, w_in_b, meta_tabs, gains, seg, META_ROWS)
    kam, vamt, kbm, vbmt = kam[0], vamt[0], kbm[0], vbmt[0]
    bias = jnp.where(jnp.arange(META_ROWS) < N_META, 0.0, MASK_BIAS).astype(F32)
    bias = jnp.broadcast_to(bias[:, None], (META_ROWS, LANES))

    out_a = _attention_a(qat, ka, vat, kam, vamt, bias)

    lam_params = jnp.stack([jnp.pad(p[layer].astype(F32), (0, LANES - HEAD_DIM))
                            for p in (b_lambda_q1, b_lambda_k1, b_lambda_q2, b_lambda_k2)])
    subln_col = b_subln_g[layer].astype(F32)[:, None]
    out_b = _attention_b(qbt, kb, vbt, kbm, vbmt, bias, lam_params, subln_col)

    w_out_b = w_out[layer].astype(BF16)
    y = _post(x.reshape(bsz * seq, d_model), out_a.reshape(bsz * seq, A_WIDTH),
              out_b.reshape(bsz * seq, B_WIDTH), w_out_b[:A_WIDTH], w_out_b[A_WIDTH:],
              ffn_norm_g[layer].astype(F32)[None, :],
              w_gate[layer].astype(BF16), w_up[layer].astype(BF16), w_down[layer].astype(BF16))
    return y.reshape(bsz, seq, d_model)
```

```python
import functools
import math

import jax
import jax.numpy as jnp
from jax import lax
from jax.experimental import pallas as pl
from jax.experimental.pallas import tpu as pltpu

N_META = 16
GRID_W = 64
HEAD_DIM = 64
ROPE_THETA = 10000.0
EPS = 1e-6
A_HEADS = 8
A_KV_HEADS = 2
A_GROUP = A_HEADS // A_KV_HEADS
A_WIDTH = A_HEADS * HEAD_DIM
A_KV_WIDTH = A_KV_HEADS * HEAD_DIM
B_HEADS = 4
B_VDIM = 2 * HEAD_DIM
B_WIDTH = B_HEADS * B_VDIM
LAM_INIT = 0.8 - 0.6 * math.exp(-0.3 * 0)

LANES = 128
VMEM_LIMIT_BYTES = 56 * 1024 * 1024

Q_PRESCALE = (HEAD_DIM ** -0.5) * math.log2(math.e)
MASK_BIAS = -1e30
UNSHIFTED_SCORE_LIMIT = 60.0

PROJ_ROWS = 512
META_ROWS = LANES
A_Q_COLS = 128
B_Q_COLS = 256
B_HEADS_PER_STEP = 2
KV_CHUNK = 512
POST_ROWS = 512

BF16 = jnp.bfloat16
F32 = jnp.float32


def _dot(a, b):
    return jnp.dot(a, b, preferred_element_type=F32)


def _lane_index(shape):
    return lax.broadcasted_iota(jnp.int32, shape, len(shape) - 1)


def _row_index(shape):
    return lax.broadcasted_iota(jnp.int32, shape, 0)


def _proj_kernel(x_ref, g_ref, w_ref, cos_a_ref, sin_a_ref, cos_b_ref, sin_b_ref,
                 gains_ref, seg_ref,
                 qa_ref, ka_ref, va_ref, qb_ref, kb_ref, vb_ref):
    x = x_ref[...]
    ms = jnp.mean(x * x, axis=-1, keepdims=True)
    h = (x * lax.rsqrt(ms + EPS) * g_ref[...]).astype(BF16)
    proj = _dot(h, w_ref[...])

    rows = x.shape[0]
    lane = _lane_index((rows, LANES))
    even = (lane % 2) == 0
    low_half = lane < HEAD_DIM
    seg = seg_ref[...]

    def norm_rope(col, gain, cos, sin):
        ss = _dot((col * col).astype(BF16), seg)
        y = col * lax.rsqrt(ss * (1.0 / HEAD_DIM) + EPS) * gain
        swapped = jnp.where(even, pltpu.roll(y, LANES - 1, 1), pltpu.roll(y, 1, 1))
        return y * cos + swapped * sin

    cos_a, sin_a = cos_a_ref[...], sin_a_ref[...]
    cos_b, sin_b = cos_b_ref[...], sin_b_ref[...]
    g_aq = gains_ref[0:1, :] * Q_PRESCALE
    g_ak = gains_ref[1:2, :]
    g_bq = gains_ref[2:3, :] * Q_PRESCALE
    g_bk = gains_ref[3:4, :]

    off = 0
    for j in range(A_WIDTH // LANES):
        col = norm_rope(proj[:, off + j * LANES: off + (j + 1) * LANES], g_aq, cos_a, sin_a)
        flipped = pltpu.roll(col, HEAD_DIM, 1)
        for half in range(2):
            head = 2 * j + half
            kv = head // A_GROUP
            src = col if half == kv else flipped
            keep = low_half if kv == 0 else jnp.logical_not(low_half)
            qa_ref[head] = jnp.where(keep, src, 0.0).T.astype(BF16)
    off += A_WIDTH
    ka_ref[...] = norm_rope(proj[:, off:off + LANES], g_ak, cos_a, sin_a).astype(BF16)
    off += A_KV_WIDTH
    va_ref[...] = proj[:, off:off + LANES].T.astype(BF16)
    off += A_KV_WIDTH
    for j in range(B_HEADS):
        col = proj[:, off + j * LANES: off + (j + 1) * LANES]
        qb_ref[j] = norm_rope(col, g_bq, cos_b, sin_b).T.astype(BF16)
    off += B_WIDTH
    for j in range(B_HEADS):
        col = proj[:, off + j * LANES: off + (j + 1) * LANES]
        kb_ref[:, j * LANES:(j + 1) * LANES] = norm_rope(col, g_bk, cos_b, sin_b).astype(BF16)
    off += B_WIDTH
    for j in range(B_HEADS):
        col = proj[:, off + j * LANES: off + (j + 1) * LANES]
        vb_ref[j * LANES:(j + 1) * LANES, :] = col.T.astype(BF16)


def _project(x3, g, w_bf16, tables, gains, seg, rows_per_step):
    bsz, n_rows, d_model = x3.shape
    in_width = w_bf16.shape[1]
    t = rows_per_step
    tab_blk = pl.BlockSpec((t, LANES), lambda b, i: (i, 0))
    const = lambda shape: pl.BlockSpec(shape, lambda b, i: (0, 0))
    out_specs = [
        pl.BlockSpec((None, A_HEADS, LANES, t), lambda b, i: (b, 0, 0, i)),
        pl.BlockSpec((None, t, LANES), lambda b, i: (b, i, 0)),
        pl.BlockSpec((None, LANES, t), lambda b, i: (b, 0, i)),
        pl.BlockSpec((None, B_HEADS, LANES, t), lambda b, i: (b, 0, 0, i)),
        pl.BlockSpec((None, t, B_WIDTH), lambda b, i: (b, i, 0)),
        pl.BlockSpec((None, B_WIDTH, t), lambda b, i: (b, 0, i)),
    ]
    out_shapes = [(bsz, A_HEADS, LANES, n_rows), (bsz, n_rows, LANES), (bsz, LANES, n_rows),
                  (bsz, B_HEADS, LANES, n_rows), (bsz, n_rows, B_WIDTH), (bsz, B_WIDTH, n_rows)]
    return pl.pallas_call(
        _proj_kernel,
        grid=(bsz, n_rows // t),
        in_specs=[pl.BlockSpec((None, t, d_model), lambda b, i: (b, i, 0)),
                  const((1, d_model)), const((d_model, in_width)),
                  tab_blk, tab_blk, tab_blk, tab_blk, const(gains.shape), const(seg.shape)],
        out_specs=out_specs,
        out_shape=[jax.ShapeDtypeStruct(s, BF16) for s in out_shapes],
        compiler_params=pltpu.CompilerParams(
            dimension_semantics=("arbitrary", "arbitrary"), vmem_limit_bytes=VMEM_LIMIT_BYTES),
        name="proj",
    )(x3, g, w_bf16, *tables, gains, seg)


def _attend_t(streams, bias_ref, running_max):
    n_chunks = streams[0][1].shape[0] // KV_CHUNK
    items = [(i, None) for i in range(len(streams))]
    items += [(i, c) for c in range(n_chunks) for i in range(len(streams))]

    def scores(item):
        i, c = item
        qt, k_ref, _, km, _ = streams[i]
        if c is None:
            bias = jnp.concatenate([bias_ref[...]] * (qt.shape[1] // LANES), axis=1)
            return _dot(km, qt) + bias
        return _dot(k_ref[c * KV_CHUNK:(c + 1) * KV_CHUNK, :], qt)

    state = [None] * len(streams)
    s_next = scores(items[0])
    for pos, (i, c) in enumerate(items):
        s = s_next
        if pos + 1 < len(items):
            s_next = scores(items[pos + 1])
        _, _, vt_ref, _, vmt = streams[i]
        vt = vmt if c is None else vt_ref[:, c * KV_CHUNK:(c + 1) * KV_CHUNK]
        if not running_max:
            p = jnp.exp2(s)
            l, acc = jnp.sum(p, axis=0, keepdims=True), _dot(vt, p.astype(BF16))
            if state[i] is not None:
                l, acc = state[i][1] + l, state[i][2] + acc
            state[i] = (None, l, acc)
            continue
        s_max = jnp.max(s, axis=0, keepdims=True)
        if state[i] is None:
            p = jnp.exp2(s - s_max)
            state[i] = (s_max, jnp.sum(p, axis=0, keepdims=True), _dot(vt, p.astype(BF16)))
        else:
            m, l, acc = state[i]
            m_new = jnp.maximum(m, s_max)
            alpha = jnp.exp2(m - m_new)
            p = jnp.exp2(s - m_new)
            l = alpha * l + jnp.sum(p, axis=0, keepdims=True)
            acc = alpha * acc + _dot(vt, p.astype(BF16))
            state[i] = (m_new, l, acc)
    return [acc / l for (_, l, acc) in state]


def _attn_a_kernel(q_ref, k_ref, vt_ref, km_ref, vmt_ref, bias_ref, o_ref, *, running_max):
    tq = q_ref.shape[2]
    streams = []
    for kv in range(A_KV_HEADS):
        heads = range(kv * A_GROUP, (kv + 1) * A_GROUP)
        qt = jnp.concatenate([q_ref[h] for h in heads], axis=1)
        rows = slice(kv * HEAD_DIM, (kv + 1) * HEAD_DIM)
        streams.append((qt, k_ref, vt_ref.at[rows, :], km_ref[...], vmt_ref[rows, :]))
    outs = _attend_t(streams, bias_ref, running_max)
    pieces = [outs[kv][:, i * tq:(i + 1) * tq] for kv in range(A_KV_HEADS) for i in range(A_GROUP)]
    o_ref[...] = jnp.concatenate(pieces, axis=0).T.astype(o_ref.dtype)


def _attn_b_kernel(q_ref, k_ref, vt_ref, km_ref, vmt_ref, bias_ref, lam_ref, subln_ref, o_ref, *,
                   running_max):
    tq = q_ref.shape[2]
    first = _row_index((LANES, tq)) < HEAD_DIM
    zero = jnp.zeros((LANES, tq), BF16)
    streams = []
    for h in range(B_HEADS_PER_STEP):
        q = q_ref[h]
        cols = slice(h * LANES, (h + 1) * LANES)
        qt = jnp.concatenate([jnp.where(first, q, zero), jnp.where(first, zero, q)], axis=1)
        streams.append((qt, k_ref.at[:, cols], vt_ref.at[cols, :], km_ref[:, cols], vmt_ref[cols, :]))
    outs = _attend_t(streams, bias_ref, running_max)
    lam_p = lam_ref[...]
    lam = (jnp.exp(jnp.sum(lam_p[0:1] * lam_p[1:2], axis=-1, keepdims=True))
           - jnp.exp(jnp.sum(lam_p[2:3] * lam_p[3:4], axis=-1, keepdims=True)) + LAM_INIT)
    ys = []
    for o in outs:
        d = o[:, :tq] - lam * o[:, tq:]
        ms = jnp.mean(d * d, axis=0, keepdims=True)
        ys.append(d * lax.rsqrt(ms + EPS) * subln_ref[...] * (1.0 - LAM_INIT))
    o_ref[...] = jnp.concatenate(ys, axis=0).T.astype(o_ref.dtype)


def _attention_a(qat, ka, vat, kam, vamt, bias, running_max):
    bsz, _, _, seq = qat.shape
    const = lambda shape: pl.BlockSpec(shape, lambda b, i: (0,) * len(shape))
    return pl.pallas_call(
        functools.partial(_attn_a_kernel, running_max=running_max),
        grid=(bsz, seq // A_Q_COLS),
        in_specs=[pl.BlockSpec((None, A_HEADS, LANES, A_Q_COLS), lambda b, i: (b, 0, 0, i)),
                  pl.BlockSpec((None, seq, LANES), lambda b, i: (b, 0, 0)),
                  pl.BlockSpec((None, LANES, seq), lambda b, i: (b, 0, 0)),
                  const(kam.shape), const(vamt.shape), const(bias.shape)],
        out_specs=pl.BlockSpec((None, A_Q_COLS, A_WIDTH), lambda b, i: (b, i, 0)),
        out_shape=jax.ShapeDtypeStruct((bsz, seq, A_WIDTH), BF16),
        compiler_params=pltpu.CompilerParams(
            dimension_semantics=("arbitrary", "arbitrary"), vmem_limit_bytes=VMEM_LIMIT_BYTES),
        name="attn_a",
    )(qat, ka, vat, kam, vamt, bias)


def _attention_b(qbt, kb, vbt, kbm, vbmt, bias, lam_params, subln_col, running_max):
    bsz, _, _, seq = qbt.shape
    hp = B_HEADS_PER_STEP
    width = hp * LANES
    const = lambda shape: pl.BlockSpec(shape, lambda b, h, i: (0,) * len(shape))
    return pl.pallas_call(
        functools.partial(_attn_b_kernel, running_max=running_max),
        grid=(bsz, B_HEADS // hp, seq // B_Q_COLS),
        in_specs=[pl.BlockSpec((None, hp, LANES, B_Q_COLS), lambda b, h, i: (b, h, 0, i)),
                  pl.BlockSpec((None, seq, width), lambda b, h, i: (b, 0, h)),
                  pl.BlockSpec((None, width, seq), lambda b, h, i: (b, h, 0)),
                  pl.BlockSpec((LANES, width), lambda b, h, i: (0, h)),
                  pl.BlockSpec((width, LANES), lambda b, h, i: (h, 0)),
                  const(bias.shape), const(lam_params.shape), const(subln_col.shape)],
        out_specs=pl.BlockSpec((None, B_Q_COLS, width), lambda b, h, i: (b, i, h)),
        out_shape=jax.ShapeDtypeStruct((bsz, seq, B_WIDTH), BF16),
        compiler_params=pltpu.CompilerParams(
            dimension_semantics=("arbitrary", "arbitrary", "arbitrary"),
            vmem_limit_bytes=VMEM_LIMIT_BYTES),
        name="attn_b",
    )(qbt, kb, vbt, kbm, vbmt, bias, lam_params, subln_col)


def _post_kernel(x_ref, oa_ref, ob_ref, woa_ref, wob_ref, g_ref, wg_ref, wu_ref, wd_ref, y_ref):
    x1 = x_ref[...] + _dot(oa_ref[...], woa_ref[...]) + _dot(ob_ref[...], wob_ref[...])
    ms = jnp.mean(x1 * x1, axis=-1, keepdims=True)
    h = (x1 * lax.rsqrt(ms + EPS) * g_ref[...]).astype(BF16)
    gate = _dot(h, wg_ref[...])
    up = _dot(h, wu_ref[...])
    act = (gate * jax.nn.sigmoid(gate) * up).astype(BF16)
    y_ref[...] = x1 + _dot(act, wd_ref[...])


def _post(x_rows, oa, ob, wo_a, wo_b, g, wg, wu, wd):
    n_rows, d_model = x_rows.shape
    row_blk = lambda width: pl.BlockSpec((POST_ROWS, width), lambda i: (i, 0))
    resident = lambda a: pl.BlockSpec(a.shape, lambda i: (0, 0), pipeline_mode=pl.Buffered(1))
    return pl.pallas_call(
        _post_kernel,
        grid=(n_rows // POST_ROWS,),
        in_specs=[row_blk(d_model), row_blk(A_WIDTH), row_blk(B_WIDTH),
                  resident(wo_a), resident(wo_b), resident(g),
                  resident(wg), resident(wu), resident(wd)],
        out_specs=row_blk(d_model),
        out_shape=jax.ShapeDtypeStruct((n_rows, d_model), F32),
        compiler_params=pltpu.CompilerParams(
            dimension_semantics=("arbitrary",), vmem_limit_bytes=VMEM_LIMIT_BYTES),
        name="post",
    )(x_rows, oa, ob, wo_a, wo_b, g, wg, wu, wd)


def _pair_tables(ang):
    c, s = jnp.cos(ang), jnp.sin(ang)
    cos64 = jnp.repeat(c, 2, axis=-1)
    sin64 = jnp.stack([-s, s], axis=-1).reshape(ang.shape[0], HEAD_DIM)
    return jnp.tile(cos64, (1, 2)), jnp.tile(sin64, (1, 2))


def _rope_tables(seq):
    half = HEAD_DIM // 2
    t = jnp.arange(seq)
    inv_a = ROPE_THETA ** (-jnp.arange(0, half, 2, dtype=F32) / half)
    r = (t // GRID_W).astype(F32)
    c = (t % GRID_W).astype(F32)
    ang_a = jnp.concatenate([r[:, None] * inv_a[None, :], c[:, None] * inv_a[None, :]], axis=-1)
    inv_b = ROPE_THETA ** (-jnp.arange(0, HEAD_DIM, 2, dtype=F32) / HEAD_DIM)
    pos = jnp.arange(N_META + seq, dtype=F32)
    ang_b = pos[:, None] * inv_b[None, :]
    seq_tabs = _pair_tables(ang_a) + _pair_tables(ang_b[N_META:])
    meta_tabs = _pair_tables(jnp.zeros((N_META, half), F32)) + _pair_tables(ang_b[:N_META])
    return seq_tabs, tuple(_pad_rows(t, META_ROWS) for t in meta_tabs)


def _pad_rows(a, rows):
    return jnp.pad(a, ((0, rows - a.shape[0]), (0, 0)))


def kernel(x, meta_tokens, attn_norm_g, w_in, a_q_norm_g, a_k_norm_g, b_q_norm_g, b_k_norm_g,
           b_lambda_q1, b_lambda_k1, b_lambda_q2, b_lambda_k2, b_subln_g, w_out,
           ffn_norm_g, w_gate, w_up, w_down):
    bsz, seq, d_model = x.shape
    assert w_in.shape[0] == 1, "single-layer block"
    assert seq % PROJ_ROWS == 0 and seq % KV_CHUNK == 0 and seq % A_Q_COLS == 0 and seq % B_Q_COLS == 0
    layer = 0

    seq_tabs, meta_tabs = _rope_tables(seq)
    gains = jnp.stack([jnp.tile(g[layer].astype(F32), 2)
                       for g in (a_q_norm_g, a_k_norm_g, b_q_norm_g, b_k_norm_g)])
    idx = jnp.arange(LANES) // HEAD_DIM
    seg = (idx[:, None] == idx[None, :]).astype(BF16)
    w_in_b = w_in[layer].astype(BF16)
    g_attn = attn_norm_g[layer].astype(F32)[None, :]

    qat, ka, vat, qbt, kb, vbt = _project(x, g_attn, w_in_b, seq_tabs, gains, seg, PROJ_ROWS)
    meta3 = _pad_rows(meta_tokens.astype(F32), META_ROWS)[None]
    _, kam, vamt, _, kbm, vbmt = _project(meta3, g_attn, w_in_b, meta_tabs, gains, seg, META_ROWS)
    kam, vamt, kbm, vbmt = kam[0], vamt[0], kbm[0], vbmt[0]
    bias = jnp.where(jnp.arange(META_ROWS) < N_META, 0.0, MASK_BIAS).astype(F32)
    bias = jnp.broadcast_to(bias[:, None], (META_ROWS, LANES))

    lam_params = jnp.stack([jnp.pad(p[layer].astype(F32), (0, LANES - HEAD_DIM))
                            for p in (b_lambda_q1, b_lambda_k1, b_lambda_q2, b_lambda_k2)])
    subln_col = b_subln_g[layer].astype(F32)[:, None]

    def attend(running_max):
        return (_attention_a(qat, ka, vat, kam, vamt, bias, running_max),
                _attention_b(qbt, kb, vbt, kbm, vbmt, bias, lam_params, subln_col, running_max))

    amax = lambda g: jnp.max(jnp.abs(g[layer].astype(F32)))
    score_bound = HEAD_DIM * Q_PRESCALE * jnp.maximum(amax(a_q_norm_g) * amax(a_k_norm_g),
                                                      amax(b_q_norm_g) * amax(b_k_norm_g))
    out_a, out_b = lax.cond(score_bound <= UNSHIFTED_SCORE_LIMIT,
                            lambda: attend(False), lambda: attend(True))

    w_out_b = w_out[layer].astype(BF16)
    y = _post(x.reshape(bsz * seq, d_model), out_a.reshape(bsz * seq, A_WIDTH),
              out_b.reshape(bsz * seq, B_WIDTH), w_out_b[:A_WIDTH], w_out_b[A_WIDTH:],
              ffn_norm_g[layer].astype(F32)[None, :],
              w_gate[layer].astype(BF16), w_up[layer].astype(BF16), w_down[layer].astype(BF16))
    return y.reshape(bsz, seq, d_model)
```

```python
import functools
import math

import jax
import jax.numpy as jnp
from jax import lax
from jax.experimental import pallas as pl
from jax.experimental.pallas import tpu as pltpu

N_META = 16
GRID_W = 64
HEAD_DIM = 64
ROPE_THETA = 10000.0
EPS = 1e-6
A_HEADS = 8
A_KV_HEADS = 2
A_GROUP = A_HEADS // A_KV_HEADS
A_WIDTH = A_HEADS * HEAD_DIM
A_KV_WIDTH = A_KV_HEADS * HEAD_DIM
B_HEADS = 4
B_VDIM = 2 * HEAD_DIM
B_WIDTH = B_HEADS * B_VDIM
LAM_INIT = 0.8 - 0.6 * math.exp(-0.3 * 0)

LANES = 128
VMEM_LIMIT_BYTES = 56 * 1024 * 1024

Q_PRESCALE = (HEAD_DIM ** -0.5) * math.log2(math.e)
MASK_BIAS = -1e30
UNSHIFTED_SCORE_LIMIT = 60.0

PROJ_ROWS = 512
PROJ_ROW_BLOCK = 256
PROJ_GROUP_COLS = 4
META_ROWS = LANES
A_Q_COLS = 256
B_Q_COLS = 512
B_HEADS_PER_STEP = 2
KV_CHUNK = 512
POST_ROWS = 512

BF16 = jnp.bfloat16
F32 = jnp.float32


def _dot(a, b):
    return jnp.dot(a, b, preferred_element_type=F32)


def _lane_index(shape):
    return lax.broadcasted_iota(jnp.int32, shape, len(shape) - 1)


def _row_index(shape):
    return lax.broadcasted_iota(jnp.int32, shape, 0)


def _proj_kernel(x_ref, g_ref, w_ref, cos_a_ref, sin_a_ref, cos_b_ref, sin_b_ref,
                 gains_ref, seg_ref,
                 qa_ref, ka_ref, va_ref, qb_ref, kb_ref, vb_ref):
    rows = x_ref.shape[0]
    blk = min(rows, PROJ_ROW_BLOCK)
    lane = _lane_index((blk, LANES))
    even = (lane % 2) == 0
    low_half = lane < HEAD_DIM
    seg = seg_ref[...]
    g_aq = gains_ref[0:1, :] * Q_PRESCALE
    g_ak = gains_ref[1:2, :]
    g_bq = gains_ref[2:3, :] * Q_PRESCALE
    g_bk = gains_ref[3:4, :]

    def norm_rope(col, gain, cos_ref, sin_ref, rb):
        ss = _dot((col * col).astype(BF16), seg)
        y = col * lax.rsqrt(ss * (1.0 / HEAD_DIM) + EPS) * gain
        swapped = jnp.where(even, pltpu.roll(y, LANES - 1, 1), pltpu.roll(y, 1, 1))
        return y * cos_ref[rb, :] + swapped * sin_ref[rb, :]

    def emit_qa(j, col, rb):
        col = norm_rope(col, g_aq, cos_a_ref, sin_a_ref, rb)
        flipped = pltpu.roll(col, HEAD_DIM, 1)
        for half in range(2):
            head = 2 * j + half
            kv = head // A_GROUP
            src = col if half == kv else flipped
            keep = low_half if kv == 0 else jnp.logical_not(low_half)
            qa_ref[head, :, rb] = jnp.where(keep, src, 0.0).T.astype(BF16)

    def emit_ka(j, col, rb):
        ka_ref[rb, :] = norm_rope(col, g_ak, cos_a_ref, sin_a_ref, rb).astype(BF16)

    def emit_va(j, col, rb):
        va_ref[:, rb] = col.T.astype(BF16)

    def emit_qb(j, col, rb):
        qb_ref[j, :, rb] = norm_rope(col, g_bq, cos_b_ref, sin_b_ref, rb).T.astype(BF16)

    def emit_kb(j, col, rb):
        kb_ref[rb, j * LANES:(j + 1) * LANES] = norm_rope(col, g_bk, cos_b_ref, sin_b_ref, rb).astype(BF16)

    def emit_vb(j, col, rb):
        vb_ref[j * LANES:(j + 1) * LANES, rb] = col.T.astype(BF16)

    handlers = ([(emit_qa, j) for j in range(A_WIDTH // LANES)] + [(emit_ka, 0), (emit_va, 0)]
                + [(emit_qb, j) for j in range(B_HEADS)] + [(emit_kb, j) for j in range(B_HEADS)]
                + [(emit_vb, j) for j in range(B_HEADS)])
    assert len(handlers) * LANES == w_ref.shape[1]
    groups = [range(s, min(s + PROJ_GROUP_COLS, len(handlers)))
              for s in range(0, len(handlers), PROJ_GROUP_COLS)]

    row_blocks = [slice(r, r + blk) for r in range(0, rows, blk)]

    def normed(rb):
        x = x_ref[rb, :]
        ms = jnp.mean(x * x, axis=-1, keepdims=True)
        return (x * lax.rsqrt(ms + EPS) * g_ref[...]).astype(BF16)

    hs = [normed(rb) for rb in row_blocks]
    items = [(r, cols) for cols in groups for r in range(len(row_blocks))]

    def project(item):
        r, cols = item
        return _dot(hs[r], w_ref[:, cols.start * LANES:cols.stop * LANES])

    slab_next = project(items[0])
    for pos, (r, cols) in enumerate(items):
        slab = slab_next
        if pos + 1 < len(items):
            slab_next = project(items[pos + 1])
        for n, c in enumerate(cols):
            fn, j = handlers[c]
            fn(j, slab[:, n * LANES:(n + 1) * LANES], row_blocks[r])


def _project(x3, g, w_bf16, tables, gains, seg, rows_per_step):
    bsz, n_rows, d_model = x3.shape
    in_width = w_bf16.shape[1]
    t = rows_per_step
    tab_blk = pl.BlockSpec((t, LANES), lambda b, i: (i, 0))
    const = lambda shape: pl.BlockSpec(shape, lambda b, i: (0, 0))
    out_specs = [
        pl.BlockSpec((None, A_HEADS, LANES, t), lambda b, i: (b, 0, 0, i)),
        pl.BlockSpec((None, t, LANES), lambda b, i: (b, i, 0)),
        pl.BlockSpec((None, LANES, t), lambda b, i: (b, 0, i)),
        pl.BlockSpec((None, B_HEADS, LANES, t), lambda b, i: (b, 0, 0, i)),
        pl.BlockSpec((None, t, B_WIDTH), lambda b, i: (b, i, 0)),
        pl.BlockSpec((None, B_WIDTH, t), lambda b, i: (b, 0, i)),
    ]
    out_shapes = [(bsz, A_HEADS, LANES, n_rows), (bsz, n_rows, LANES), (bsz, LANES, n_rows),
                  (bsz, B_HEADS, LANES, n_rows), (bsz, n_rows, B_WIDTH), (bsz, B_WIDTH, n_rows)]
    return pl.pallas_call(
        _proj_kernel,
        grid=(bsz, n_rows // t),
        in_specs=[pl.BlockSpec((None, t, d_model), lambda b, i: (b, i, 0)),
                  const((1, d_model)), const((d_model, in_width)),
                  tab_blk, tab_blk, tab_blk, tab_blk, const(gains.shape), const(seg.shape)],
        out_specs=out_specs,
        out_shape=[jax.ShapeDtypeStruct(s, BF16) for s in out_shapes],
        compiler_params=pltpu.CompilerParams(
            dimension_semantics=("arbitrary", "arbitrary"), vmem_limit_bytes=VMEM_LIMIT_BYTES),
        name="proj",
    )(x3, g, w_bf16, *tables, gains, seg)


def _attend_t(streams, bias_ref, running_max):
    n_chunks = streams[0][1].shape[0] // KV_CHUNK
    items = [(i, None) for i in range(len(streams))]
    items += [(i, c) for c in range(n_chunks) for i in range(len(streams))]

    def scores(item):
        i, c = item
        qt, k_ref, _, km, _ = streams[i]
        if c is None:
            bias = jnp.concatenate([bias_ref[...]] * (qt.shape[1] // LANES), axis=1)
            return _dot(km, qt) + bias
        return _dot(k_ref[c * KV_CHUNK:(c + 1) * KV_CHUNK, :], qt)

    state = [None] * len(streams)
    s_next = scores(items[0])
    for pos, (i, c) in enumerate(items):
        s = s_next
        if pos + 1 < len(items):
            s_next = scores(items[pos + 1])
        _, _, vt_ref, _, vmt = streams[i]
        vt = vmt if c is None else vt_ref[:, c * KV_CHUNK:(c + 1) * KV_CHUNK]
        if not running_max:
            p = jnp.exp2(s)
            l, acc = jnp.sum(p, axis=0, keepdims=True), _dot(vt, p.astype(BF16))
            if state[i] is not None:
                l, acc = state[i][1] + l, state[i][2] + acc
            state[i] = (None, l, acc)
            continue
        s_max = jnp.max(s, axis=0, keepdims=True)
        if state[i] is None:
            p = jnp.exp2(s - s_max)
            state[i] = (s_max, jnp.sum(p, axis=0, keepdims=True), _dot(vt, p.astype(BF16)))
        else:
            m, l, acc = state[i]
            m_new = jnp.maximum(m, s_max)
            alpha = jnp.exp2(m - m_new)
            p = jnp.exp2(s - m_new)
            l = alpha * l + jnp.sum(p, axis=0, keepdims=True)
            acc = alpha * acc + _dot(vt, p.astype(BF16))
            state[i] = (m_new, l, acc)
    return [acc / l for (_, l, acc) in state]


def _attn_a_kernel(q_ref, k_ref, vt_ref, km_ref, vmt_ref, bias_ref, o_ref, *, running_max):
    tq = q_ref.shape[2]
    streams = []
    for kv in range(A_KV_HEADS):
        heads = range(kv * A_GROUP, (kv + 1) * A_GROUP)
        qt = jnp.concatenate([q_ref[h] for h in heads], axis=1)
        rows = slice(kv * HEAD_DIM, (kv + 1) * HEAD_DIM)
        streams.append((qt, k_ref, vt_ref.at[rows, :], km_ref[...], vmt_ref[rows, :]))
    outs = _attend_t(streams, bias_ref, running_max)
    pieces = [outs[kv][:, i * tq:(i + 1) * tq] for kv in range(A_KV_HEADS) for i in range(A_GROUP)]
    o_ref[...] = jnp.concatenate(pieces, axis=0).T.astype(o_ref.dtype)


def _attn_b_kernel(q_ref, k_ref, vt_ref, km_ref, vmt_ref, bias_ref, lam_ref, subln_ref, o_ref, *,
                   running_max):
    tq = q_ref.shape[2]
    first = _row_index((LANES, tq)) < HEAD_DIM
    zero = jnp.zeros((LANES, tq), BF16)
    streams = []
    for h in range(B_HEADS_PER_STEP):
        q = q_ref[h]
        cols = slice(h * LANES, (h + 1) * LANES)
        qt = jnp.concatenate([jnp.where(first, q, zero), jnp.where(first, zero, q)], axis=1)
        streams.append((qt, k_ref.at[:, cols], vt_ref.at[cols, :], km_ref[:, cols], vmt_ref[cols, :]))
    outs = _attend_t(streams, bias_ref, running_max)
    lam_p = lam_ref[...]
    lam = (jnp.exp(jnp.sum(lam_p[0:1] * lam_p[1:2], axis=-1, keepdims=True))
           - jnp.exp(jnp.sum(lam_p[2:3] * lam_p[3:4], axis=-1, keepdims=True)) + LAM_INIT)
    ys = []
    for o in outs:
        d = o[:, :tq] - lam * o[:, tq:]
        ms = jnp.mean(d * d, axis=0, keepdims=True)
        ys.append(d * lax.rsqrt(ms + EPS) * subln_ref[...] * (1.0 - LAM_INIT))
    o_ref[...] = jnp.concatenate(ys, axis=0).T.astype(o_ref.dtype)


def _attention_a(qat, ka, vat, kam, vamt, bias, running_max):
    bsz, _, _, seq = qat.shape
    const = lambda shape: pl.BlockSpec(shape, lambda b, i: (0,) * len(shape))
    return pl.pallas_call(
        functools.partial(_attn_a_kernel, running_max=running_max),
        grid=(bsz, seq // A_Q_COLS),
        in_specs=[pl.BlockSpec((None, A_HEADS, LANES, A_Q_COLS), lambda b, i: (b, 0, 0, i)),
                  pl.BlockSpec((None, seq, LANES), lambda b, i: (b, 0, 0)),
                  pl.BlockSpec((None, LANES, seq), lambda b, i: (b, 0, 0)),
                  const(kam.shape), const(vamt.shape), const(bias.shape)],
        out_specs=pl.BlockSpec((None, A_Q_COLS, A_WIDTH), lambda b, i: (b, i, 0)),
        out_shape=jax.ShapeDtypeStruct((bsz, seq, A_WIDTH), BF16),
        compiler_params=pltpu.CompilerParams(
            dimension_semantics=("arbitrary", "arbitrary"), vmem_limit_bytes=VMEM_LIMIT_BYTES),
        name="attn_a",
    )(qat, ka, vat, kam, vamt, bias)


def _attention_b(qbt, kb, vbt, kbm, vbmt, bias, lam_params, subln_col, running_max):
    bsz, _, _, seq = qbt.shape
    hp = B_HEADS_PER_STEP
    width = hp * LANES
    const = lambda shape: pl.BlockSpec(shape, lambda b, h, i: (0,) * len(shape))
    return pl.pallas_call(
        functools.partial(_attn_b_kernel, running_max=running_max),
        grid=(bsz, B_HEADS // hp, seq // B_Q_COLS),
        in_specs=[pl.BlockSpec((None, hp, LANES, B_Q_COLS), lambda b, h, i: (b, h, 0, i)),
                  pl.BlockSpec((None, seq, width), lambda b, h, i: (b, 0, h)),
                  pl.BlockSpec((None, width, seq), lambda b, h, i: (b, h, 0)),
                  pl.BlockSpec((LANES, width), lambda b, h, i: (0, h)),
                  pl.BlockSpec((width, LANES), lambda b, h, i: (h, 0)),
                  const(bias.shape), const(lam_params.shape), const(subln_col.shape)],
        out_specs=pl.BlockSpec((None, B_Q_COLS, width), lambda b, h, i: (b, i, h)),
        out_shape=jax.ShapeDtypeStruct((bsz, seq, B_WIDTH), BF16),
        compiler_params=pltpu.CompilerParams(
            dimension_semantics=("arbitrary", "arbitrary", "arbitrary"),
            vmem_limit_bytes=VMEM_LIMIT_BYTES),
        name="attn_b",
    )(qbt, kb, vbt, kbm, vbmt, bias, lam_params, subln_col)


def _post_kernel(x_ref, oa_ref, ob_ref, woa_ref, wob_ref, g_ref, wg_ref, wu_ref, wd_ref, y_ref):
    x1 = x_ref[...] + _dot(oa_ref[...], woa_ref[...]) + _dot(ob_ref[...], wob_ref[...])
    ms = jnp.mean(x1 * x1, axis=-1, keepdims=True)
    h = (x1 * lax.rsqrt(ms + EPS) * g_ref[...]).astype(BF16)
    gate = _dot(h, wg_ref[...])
    up = _dot(h, wu_ref[...])
    act = (gate * jax.nn.sigmoid(gate) * up).astype(BF16)
    y_ref[...] = x1 + _dot(act, wd_ref[...])


def _post(x_rows, oa, ob, wo_a, wo_b, g, wg, wu, wd):
    n_rows, d_model = x_rows.shape
    row_blk = lambda width: pl.BlockSpec((POST_ROWS, width), lambda i: (i, 0))
    resident = lambda a: pl.BlockSpec(a.shape, lambda i: (0, 0), pipeline_mode=pl.Buffered(1))
    return pl.pallas_call(
        _post_kernel,
        grid=(n_rows // POST_ROWS,),
        in_specs=[row_blk(d_model), row_blk(A_WIDTH), row_blk(B_WIDTH),
                  resident(wo_a), resident(wo_b), resident(g),
                  resident(wg), resident(wu), resident(wd)],
        out_specs=row_blk(d_model),
        out_shape=jax.ShapeDtypeStruct((n_rows, d_model), F32),
        compiler_params=pltpu.CompilerParams(
            dimension_semantics=("arbitrary",), vmem_limit_bytes=VMEM_LIMIT_BYTES),
        name="post",
    )(x_rows, oa, ob, wo_a, wo_b, g, wg, wu, wd)


def _pair_tables(ang):
    c, s = jnp.cos(ang), jnp.sin(ang)
    cos64 = jnp.repeat(c, 2, axis=-1)
    sin64 = jnp.stack([-s, s], axis=-1).reshape(ang.shape[0], HEAD_DIM)
    return jnp.tile(cos64, (1, 2)), jnp.tile(sin64, (1, 2))


def _rope_tables(seq):
    half = HEAD_DIM // 2
    t = jnp.arange(seq)
    inv_a = ROPE_THETA ** (-jnp.arange(0, half, 2, dtype=F32) / half)
    r = (t // GRID_W).astype(F32)
    c = (t % GRID_W).astype(F32)
    ang_a = jnp.concatenate([r[:, None] * inv_a[None, :], c[:, None] * inv_a[None, :]], axis=-1)
    inv_b = ROPE_THETA ** (-jnp.arange(0, HEAD_DIM, 2, dtype=F32) / HEAD_DIM)
    pos = jnp.arange(N_META + seq, dtype=F32)
    ang_b = pos[:, None] * inv_b[None, :]
    seq_tabs = _pair_tables(ang_a) + _pair_tables(ang_b[N_META:])
    meta_tabs = _pair_tables(jnp.zeros((N_META, half), F32)) + _pair_tables(ang_b[:N_META])
    return seq_tabs, tuple(_pad_rows(t, META_ROWS) for t in meta_tabs)


def _pad_rows(a, rows):
    return jnp.pad(a, ((0, rows - a.shape[0]), (0, 0)))


def kernel(x, meta_tokens, attn_norm_g, w_in, a_q_norm_g, a_k_norm_g, b_q_norm_g, b_k_norm_g,
           b_lambda_q1, b_lambda_k1, b_lambda_q2, b_lambda_k2, b_subln_g, w_out,
           ffn_norm_g, w_gate, w_up, w_down):
    bsz, seq, d_model = x.shape
    assert w_in.shape[0] == 1, "single-layer block"
    assert seq % PROJ_ROWS == 0 and seq % KV_CHUNK == 0 and seq % A_Q_COLS == 0 and seq % B_Q_COLS == 0
    layer = 0

    seq_tabs, meta_tabs = _rope_tables(seq)
    gains = jnp.stack([jnp.tile(g[layer].astype(F32), 2)
                       for g in (a_q_norm_g, a_k_norm_g, b_q_norm_g, b_k_norm_g)])
    idx = jnp.arange(LANES) // HEAD_DIM
    seg = (idx[:, None] == idx[None, :]).astype(BF16)
    w_in_b = w_in[layer].astype(BF16)
    g_attn = attn_norm_g[layer].astype(F32)[None, :]

    qat, ka, vat, qbt, kb, vbt = _project(x, g_attn, w_in_b, seq_tabs, gains, seg, PROJ_ROWS)
    meta3 = _pad_rows(meta_tokens.astype(F32), META_ROWS)[None]
    _, kam, vamt, _, kbm, vbmt = _project(meta3, g_attn, w_in_b, meta_tabs, gains, seg, META_ROWS)
    kam, vamt, kbm, vbmt = kam[0], vamt[0], kbm[0], vbmt[0]
    bias = jnp.where(jnp.arange(META_ROWS) < N_META, 0.0, MASK_BIAS).astype(F32)
    bias = jnp.broadcast_to(bias[:, None], (META_ROWS, LANES))

    lam_params = jnp.stack([jnp.pad(p[layer].astype(F32), (0, LANES - HEAD_DIM))
                            for p in (b_lambda_q1, b_lambda_k1, b_lambda_q2, b_lambda_k2)])
    subln_col = b_subln_g[layer].astype(F32)[:, None]

    def attend(running_max):
        return (_attention_a(qat, ka, vat, kam, vamt, bias, running_max),
                _attention_b(qbt, kb, vbt, kbm, vbmt, bias, lam_params, subln_col, running_max))

    amax = lambda g: jnp.max(jnp.abs(g[layer].astype(F32)))
    score_bound = HEAD_DIM * Q_PRESCALE * jnp.maximum(amax(a_q_norm_g) * amax(a_k_norm_g),
                                                      amax(b_q_norm_g) * amax(b_k_norm_g))
    out_a, out_b = lax.cond(score_bound <= UNSHIFTED_SCORE_LIMIT,
                            lambda: attend(False), lambda: attend(True))

    w_out_b = w_out[layer].astype(BF16)
    y = _post(x.reshape(bsz * seq, d_model), out_a.reshape(bsz * seq, A_WIDTH),
              out_b.reshape(bsz * seq, B_WIDTH), w_out_b[:A_WIDTH], w_out_b[A_WIDTH:],
              ffn_norm_g[layer].astype(F32)[None, :],
              w_gate[layer].astype(BF16), w_up[layer].astype(BF16), w_down[layer].astype(BF16))
    return y.reshape(bsz, seq, d_model)
```

```python
import functools
import math

import jax
import jax.numpy as jnp
from jax import lax
from jax.experimental import pallas as pl
from jax.experimental.pallas import tpu as pltpu

N_META = 16
GRID_W = 64
HEAD_DIM = 64
ROPE_THETA = 10000.0
EPS = 1e-6
A_HEADS = 8
A_KV_HEADS = 2
A_GROUP = A_HEADS // A_KV_HEADS
A_WIDTH = A_HEADS * HEAD_DIM
A_KV_WIDTH = A_KV_HEADS * HEAD_DIM
B_HEADS = 4
B_VDIM = 2 * HEAD_DIM
B_WIDTH = B_HEADS * B_VDIM
LAM_INIT = 0.8 - 0.6 * math.exp(-0.3 * 0)

LANES = 128
VMEM_LIMIT_BYTES = 56 * 1024 * 1024

Q_PRESCALE = (HEAD_DIM ** -0.5) * math.log2(math.e)
MASK_BIAS = -1e30
UNSHIFTED_SCORE_LIMIT = 60.0

PROJ_ROWS = 512
PROJ_ROW_BLOCK = 256
PROJ_GROUP_COLS = 4
META_ROWS = LANES
A_Q_COLS = 256
B_Q_COLS = 512
Q_TILES_PER_STEP = 2
A_STEP_COLS = A_Q_COLS * Q_TILES_PER_STEP
B_STEP_COLS = B_Q_COLS * Q_TILES_PER_STEP
B_HEADS_PER_STEP = 2
KV_CHUNK = 512
POST_ROWS = 512

BF16 = jnp.bfloat16
F32 = jnp.float32


def _dot(a, b):
    return jnp.dot(a, b, preferred_element_type=F32)


def _lane_index(shape):
    return lax.broadcasted_iota(jnp.int32, shape, len(shape) - 1)


def _row_index(shape):
    return lax.broadcasted_iota(jnp.int32, shape, 0)


def _proj_kernel(x_ref, g_ref, w_ref, cos_a_ref, sin_a_ref, cos_b_ref, sin_b_ref,
                 gains_ref, seg_ref,
                 qa_ref, ka_ref, va_ref, qb_ref, kb_ref, vb_ref):
    rows = x_ref.shape[0]
    blk = min(rows, PROJ_ROW_BLOCK)
    lane = _lane_index((blk, LANES))
    even = (lane % 2) == 0
    low_half = lane < HEAD_DIM
    seg = seg_ref[...]
    g_aq = gains_ref[0:1, :] * Q_PRESCALE
    g_ak = gains_ref[1:2, :]
    g_bq = gains_ref[2:3, :] * Q_PRESCALE
    g_bk = gains_ref[3:4, :]

    def norm_rope(col, gain, cos_ref, sin_ref, rb):
        ss = _dot((col * col).astype(BF16), seg)
        y = col * lax.rsqrt(ss * (1.0 / HEAD_DIM) + EPS) * gain
        swapped = jnp.where(even, pltpu.roll(y, LANES - 1, 1), pltpu.roll(y, 1, 1))
        return y * cos_ref[rb, :] + swapped * sin_ref[rb, :]

    def emit_qa(j, col, rb):
        col = norm_rope(col, g_aq, cos_a_ref, sin_a_ref, rb)
        flipped = pltpu.roll(col, HEAD_DIM, 1)
        for half in range(2):
            head = 2 * j + half
            kv = head // A_GROUP
            src = col if half == kv else flipped
            keep = low_half if kv == 0 else jnp.logical_not(low_half)
            qa_ref[head, :, rb] = jnp.where(keep, src, 0.0).T.astype(BF16)

    def emit_ka(j, col, rb):
        ka_ref[rb, :] = norm_rope(col, g_ak, cos_a_ref, sin_a_ref, rb).astype(BF16)

    def emit_va(j, col, rb):
        va_ref[:, rb] = col.T.astype(BF16)

    def emit_qb(j, col, rb):
        qb_ref[j, :, rb] = norm_rope(col, g_bq, cos_b_ref, sin_b_ref, rb).T.astype(BF16)

    def emit_kb(j, col, rb):
        kb_ref[rb, j * LANES:(j + 1) * LANES] = norm_rope(col, g_bk, cos_b_ref, sin_b_ref, rb).astype(BF16)

    def emit_vb(j, col, rb):
        vb_ref[j * LANES:(j + 1) * LANES, rb] = col.T.astype(BF16)

    handlers = ([(emit_qa, j) for j in range(A_WIDTH // LANES)] + [(emit_ka, 0), (emit_va, 0)]
                + [(emit_qb, j) for j in range(B_HEADS)] + [(emit_kb, j) for j in range(B_HEADS)]
                + [(emit_vb, j) for j in range(B_HEADS)])
    assert len(handlers) * LANES == w_ref.shape[1]
    groups = [range(s, min(s + PROJ_GROUP_COLS, len(handlers)))
              for s in range(0, len(handlers), PROJ_GROUP_COLS)]

    row_blocks = [slice(r, r + blk) for r in range(0, rows, blk)]

    def normed(rb):
        x = x_ref[rb, :]
        ms = jnp.mean(x * x, axis=-1, keepdims=True)
        return (x * lax.rsqrt(ms + EPS) * g_ref[...]).astype(BF16)

    hs = [normed(rb) for rb in row_blocks]
    items = [(r, cols) for cols in groups for r in range(len(row_blocks))]

    def project(item):
        r, cols = item
        return _dot(hs[r], w_ref[:, cols.start * LANES:cols.stop * LANES])

    slab_next = project(items[0])
    for pos, (r, cols) in enumerate(items):
        slab = slab_next
        if pos + 1 < len(items):
            slab_next = project(items[pos + 1])
        for n, c in enumerate(cols):
            fn, j = handlers[c]
            fn(j, slab[:, n * LANES:(n + 1) * LANES], row_blocks[r])


def _project(x3, g, w_bf16, tables, gains, seg, rows_per_step):
    bsz, n_rows, d_model = x3.shape
    in_width = w_bf16.shape[1]
    t = rows_per_step
    tab_blk = pl.BlockSpec((t, LANES), lambda b, i: (i, 0))
    const = lambda shape: pl.BlockSpec(shape, lambda b, i: (0, 0))
    out_specs = [
        pl.BlockSpec((None, A_HEADS, LANES, t), lambda b, i: (b, 0, 0, i)),
        pl.BlockSpec((None, t, LANES), lambda b, i: (b, i, 0)),
        pl.BlockSpec((None, LANES, t), lambda b, i: (b, 0, i)),
        pl.BlockSpec((None, B_HEADS, LANES, t), lambda b, i: (b, 0, 0, i)),
        pl.BlockSpec((None, t, B_WIDTH), lambda b, i: (b, i, 0)),
        pl.BlockSpec((None, B_WIDTH, t), lambda b, i: (b, 0, i)),
    ]
    out_shapes = [(bsz, A_HEADS, LANES, n_rows), (bsz, n_rows, LANES), (bsz, LANES, n_rows),
                  (bsz, B_HEADS, LANES, n_rows), (bsz, n_rows, B_WIDTH), (bsz, B_WIDTH, n_rows)]
    return pl.pallas_call(
        _proj_kernel,
        grid=(bsz, n_rows // t),
        in_specs=[pl.BlockSpec((None, t, d_model), lambda b, i: (b, i, 0)),
                  const((1, d_model)), const((d_model, in_width)),
                  tab_blk, tab_blk, tab_blk, tab_blk, const(gains.shape), const(seg.shape)],
        out_specs=out_specs,
        out_shape=[jax.ShapeDtypeStruct(s, BF16) for s in out_shapes],
        compiler_params=pltpu.CompilerParams(
            dimension_semantics=("arbitrary", "arbitrary"), vmem_limit_bytes=VMEM_LIMIT_BYTES),
        name="proj",
    )(x3, g, w_bf16, *tables, gains, seg)


def _attend_t(groups, bias_ref, running_max, finish):
    streams = [s for group in groups for s in group]
    n_chunks = streams[0][1].shape[0] // KV_CHUNK
    items, group_end, base = [], {}, 0
    for g, group in enumerate(groups):
        ids = range(base, base + len(group))
        items += [(i, None) for i in ids] + [(i, c) for c in range(n_chunks) for i in ids]
        group_end[len(items) - 1] = (g, ids)
        base += len(group)

    def scores(item):
        i, c = item
        qt, k_ref, _, km, _ = streams[i]
        if c is None:
            bias = jnp.concatenate([bias_ref[...]] * (qt.shape[1] // LANES), axis=1)
            return _dot(km, qt) + bias
        return _dot(k_ref[c * KV_CHUNK:(c + 1) * KV_CHUNK, :], qt)

    def update(i, c, s, state):
        _, _, vt_ref, _, vmt = streams[i]
        vt = vmt if c is None else vt_ref[:, c * KV_CHUNK:(c + 1) * KV_CHUNK]
        if not running_max:
            p = jnp.exp2(s)
            l, acc = jnp.sum(p, axis=0, keepdims=True), _dot(vt, p.astype(BF16))
            if state is not None:
                l, acc = state[1] + l, state[2] + acc
            return (None, l, acc)
        s_max = jnp.max(s, axis=0, keepdims=True)
        if state is None:
            p = jnp.exp2(s - s_max)
            return (s_max, jnp.sum(p, axis=0, keepdims=True), _dot(vt, p.astype(BF16)))
        m, l, acc = state
        m_new = jnp.maximum(m, s_max)
        alpha = jnp.exp2(m - m_new)
        p = jnp.exp2(s - m_new)
        return (m_new, alpha * l + jnp.sum(p, axis=0, keepdims=True),
                alpha * acc + _dot(vt, p.astype(BF16)))

    state = [None] * len(streams)
    s_next = scores(items[0])
    for pos, (i, c) in enumerate(items):
        s = s_next
        if pos + 1 < len(items):
            s_next = scores(items[pos + 1])
        state[i] = update(i, c, s, state[i])
        if pos in group_end:
            g, ids = group_end[pos]
            finish(g, [state[j][2] / state[j][1] for j in ids])


def _attn_a_kernel(q_ref, k_ref, vt_ref, km_ref, vmt_ref, bias_ref, o_ref, *, running_max):
    tq = A_Q_COLS
    km, vmt = km_ref[...], vmt_ref[...]
    groups = []
    for t in range(q_ref.shape[2] // tq):
        group = []
        for kv in range(A_KV_HEADS):
            heads = range(kv * A_GROUP, (kv + 1) * A_GROUP)
            qt = jnp.concatenate([q_ref[h, :, t * tq:(t + 1) * tq] for h in heads], axis=1)
            group.append((qt, k_ref, vt_ref, km, vmt))
        groups.append(group)

    def finish(t, outs):
        pieces = [outs[kv][kv * HEAD_DIM:(kv + 1) * HEAD_DIM, i * tq:(i + 1) * tq]
                  for kv in range(A_KV_HEADS) for i in range(A_GROUP)]
        o_ref[t * tq:(t + 1) * tq, :] = jnp.concatenate(pieces, axis=0).T.astype(o_ref.dtype)

    _attend_t(groups, bias_ref, running_max, finish)


def _attn_b_kernel(q_ref, k_ref, vt_ref, km_ref, vmt_ref, bias_ref, lam_ref, subln_ref, o_ref, *,
                   running_max):
    tq = B_Q_COLS
    first = _row_index((LANES, tq)) < HEAD_DIM
    zero = jnp.zeros((LANES, tq), BF16)
    groups = []
    for t in range(q_ref.shape[2] // tq):
        group = []
        for h in range(B_HEADS_PER_STEP):
            q = q_ref[h, :, t * tq:(t + 1) * tq]
            cols = slice(h * LANES, (h + 1) * LANES)
            qt = jnp.concatenate([jnp.where(first, q, zero), jnp.where(first, zero, q)], axis=1)
            group.append((qt, k_ref.at[:, cols], vt_ref.at[cols, :], km_ref[:, cols], vmt_ref[cols, :]))
        groups.append(group)
    lam_p = lam_ref[...]
    lam = (jnp.exp(jnp.sum(lam_p[0:1] * lam_p[1:2], axis=-1, keepdims=True))
           - jnp.exp(jnp.sum(lam_p[2:3] * lam_p[3:4], axis=-1, keepdims=True)) + LAM_INIT)

    def finish(t, outs):
        ys = []
        for o in outs:
            d = o[:, :tq] - lam * o[:, tq:]
            ms = jnp.mean(d * d, axis=0, keepdims=True)
            ys.append(d * lax.rsqrt(ms + EPS) * subln_ref[...] * (1.0 - LAM_INIT))
        o_ref[t * tq:(t + 1) * tq, :] = jnp.concatenate(ys, axis=0).T.astype(o_ref.dtype)

    _attend_t(groups, bias_ref, running_max, finish)


def _attention_a(qat, ka, vat, kam, vamt, bias, running_max):
    bsz, _, _, seq = qat.shape
    const = lambda shape: pl.BlockSpec(shape, lambda b, i: (0,) * len(shape))
    return pl.pallas_call(
        functools.partial(_attn_a_kernel, running_max=running_max),
        grid=(bsz, seq // A_STEP_COLS),
        in_specs=[pl.BlockSpec((None, A_HEADS, LANES, A_STEP_COLS), lambda b, i: (b, 0, 0, i)),
                  pl.BlockSpec((None, seq, LANES), lambda b, i: (b, 0, 0)),
                  pl.BlockSpec((None, LANES, seq), lambda b, i: (b, 0, 0)),
                  const(kam.shape), const(vamt.shape), const(bias.shape)],
        out_specs=pl.BlockSpec((None, A_STEP_COLS, A_WIDTH), lambda b, i: (b, i, 0)),
        out_shape=jax.ShapeDtypeStruct((bsz, seq, A_WIDTH), BF16),
        compiler_params=pltpu.CompilerParams(
            dimension_semantics=("arbitrary", "arbitrary"), vmem_limit_bytes=VMEM_LIMIT_BYTES),
        name="attn_a",
    )(qat, ka, vat, kam, vamt, bias)


def _attention_b(qbt, kb, vbt, kbm, vbmt, bias, lam_params, subln_col, running_max):
    bsz, _, _, seq = qbt.shape
    hp = B_HEADS_PER_STEP
    width = hp * LANES
    const = lambda shape: pl.BlockSpec(shape, lambda b, h, i: (0,) * len(shape))
    return pl.pallas_call(
        functools.partial(_attn_b_kernel, running_max=running_max),
        grid=(bsz, B_HEADS // hp, seq // B_STEP_COLS),
        in_specs=[pl.BlockSpec((None, hp, LANES, B_STEP_COLS), lambda b, h, i: (b, h, 0, i)),
                  pl.BlockSpec((None, seq, width), lambda b, h, i: (b, 0, h)),
                  pl.BlockSpec((None, width, seq), lambda b, h, i: (b, h, 0)),
                  pl.BlockSpec((LANES, width), lambda b, h, i: (0, h)),
                  pl.BlockSpec((width, LANES), lambda b, h, i: (h, 0)),
                  const(bias.shape), const(lam_params.shape), const(subln_col.shape)],
        out_specs=pl.BlockSpec((None, B_STEP_COLS, width), lambda b, h, i: (b, i, h)),
        out_shape=jax.ShapeDtypeStruct((bsz, seq, B_WIDTH), BF16),
        compiler_params=pltpu.CompilerParams(
            dimension_semantics=("arbitrary", "arbitrary", "arbitrary"),
            vmem_limit_bytes=VMEM_LIMIT_BYTES),
        name="attn_b",
    )(qbt, kb, vbt, kbm, vbmt, bias, lam_params, subln_col)


def _post_kernel(x_ref, oa_ref, ob_ref, woa_ref, wob_ref, g_ref, wg_ref, wu_ref, wd_ref, y_ref):
    x1 = x_ref[...] + _dot(oa_ref[...], woa_ref[...]) + _dot(ob_ref[...], wob_ref[...])
    ms = jnp.mean(x1 * x1, axis=-1, keepdims=True)
    h = (x1 * lax.rsqrt(ms + EPS) * g_ref[...]).astype(BF16)
    gate = _dot(h, wg_ref[...])
    up = _dot(h, wu_ref[...])
    act = (gate * jax.nn.sigmoid(gate) * up).astype(BF16)
    y_ref[...] = x1 + _dot(act, wd_ref[...])


def _post(x_rows, oa, ob, wo_a, wo_b, g, wg, wu, wd):
    n_rows, d_model = x_rows.shape
    row_blk = lambda width: pl.BlockSpec((POST_ROWS, width), lambda i: (i, 0))
    resident = lambda a: pl.BlockSpec(a.shape, lambda i: (0, 0), pipeline_mode=pl.Buffered(1))
    return pl.pallas_call(
        _post_kernel,
        grid=(n_rows // POST_ROWS,),
        in_specs=[row_blk(d_model), row_blk(A_WIDTH), row_blk(B_WIDTH),
                  resident(wo_a), resident(wo_b), resident(g),
                  resident(wg), resident(wu), resident(wd)],
        out_specs=row_blk(d_model),
        out_shape=jax.ShapeDtypeStruct((n_rows, d_model), F32),
        compiler_params=pltpu.CompilerParams(
            dimension_semantics=("arbitrary",), vmem_limit_bytes=VMEM_LIMIT_BYTES),
        name="post",
    )(x_rows, oa, ob, wo_a, wo_b, g, wg, wu, wd)


def _pair_tables(ang):
    c, s = jnp.cos(ang), jnp.sin(ang)
    cos64 = jnp.repeat(c, 2, axis=-1)
    sin64 = jnp.stack([-s, s], axis=-1).reshape(ang.shape[0], HEAD_DIM)
    return jnp.tile(cos64, (1, 2)), jnp.tile(sin64, (1, 2))


def _rope_tables(seq):
    half = HEAD_DIM // 2
    t = jnp.arange(seq)
    inv_a = ROPE_THETA ** (-jnp.arange(0, half, 2, dtype=F32) / half)
    r = (t // GRID_W).astype(F32)
    c = (t % GRID_W).astype(F32)
    ang_a = jnp.concatenate([r[:, None] * inv_a[None, :], c[:, None] * inv_a[None, :]], axis=-1)
    inv_b = ROPE_THETA ** (-jnp.arange(0, HEAD_DIM, 2, dtype=F32) / HEAD_DIM)
    pos = jnp.arange(N_META + seq, dtype=F32)
    ang_b = pos[:, None] * inv_b[None, :]
    seq_tabs = _pair_tables(ang_a) + _pair_tables(ang_b[N_META:])
    meta_tabs = _pair_tables(jnp.zeros((N_META, half), F32)) + _pair_tables(ang_b[:N_META])
    return seq_tabs, tuple(_pad_rows(t, META_ROWS) for t in meta_tabs)


def _pad_rows(a, rows):
    return jnp.pad(a, ((0, rows - a.shape[0]), (0, 0)))


def kernel(x, meta_tokens, attn_norm_g, w_in, a_q_norm_g, a_k_norm_g, b_q_norm_g, b_k_norm_g,
           b_lambda_q1, b_lambda_k1, b_lambda_q2, b_lambda_k2, b_subln_g, w_out,
           ffn_norm_g, w_gate, w_up, w_down):
    bsz, seq, d_model = x.shape
    assert w_in.shape[0] == 1, "single-layer block"
    assert seq % PROJ_ROWS == 0 and seq % KV_CHUNK == 0 and seq % A_STEP_COLS == 0 and seq % B_STEP_COLS == 0
    layer = 0

    seq_tabs, meta_tabs = _rope_tables(seq)
    gains = jnp.stack([jnp.tile(g[layer].astype(F32), 2)
                       for g in (a_q_norm_g, a_k_norm_g, b_q_norm_g, b_k_norm_g)])
    idx = jnp.arange(LANES) // HEAD_DIM
    seg = (idx[:, None] == idx[None, :]).astype(BF16)
    w_in_b = w_in[layer].astype(BF16)
    g_attn = attn_norm_g[layer].astype(F32)[None, :]

    qat, ka, vat, qbt, kb, vbt = _project(x, g_attn, w_in_b, seq_tabs, gains, seg, PROJ_ROWS)
    meta3 = _pad_rows(meta_tokens.astype(F32), META_ROWS)[None]
    _, kam, vamt, _, kbm, vbmt = _project(meta3, g_attn, w_in_b, meta_tabs, gains, seg, META_ROWS)
    kam, vamt, kbm, vbmt = kam[0], vamt[0], kbm[0], vbmt[0]
    bias = jnp.where(jnp.arange(META_ROWS) < N_META, 0.0, MASK_BIAS).astype(F32)
    bias = jnp.broadcast_to(bias[:, None], (META_ROWS, LANES))

    lam_params = jnp.stack([jnp.pad(p[layer].astype(F32), (0, LANES - HEAD_DIM))
                            for p in (b_lambda_q1, b_lambda_k1, b_lambda_q2, b_lambda_k2)])
    subln_col = b_subln_g[layer].astype(F32)[:, None]

    def attend(running_max):
        return (_attention_a(qat, ka, vat, kam, vamt, bias, running_max),
                _attention_b(qbt, kb, vbt, kbm, vbmt, bias, lam_params, subln_col, running_max))

    amax = lambda g: jnp.max(jnp.abs(g[layer].astype(F32)))
    score_bound = HEAD_DIM * Q_PRESCALE * jnp.maximum(amax(a_q_norm_g) * amax(a_k_norm_g),
                                                      amax(b_q_norm_g) * amax(b_k_norm_g))
    out_a, out_b = lax.cond(score_bound <= UNSHIFTED_SCORE_LIMIT,
                            lambda: attend(False), lambda: attend(True))

    w_out_b = w_out[layer].astype(BF16)
    y = _post(x.reshape(bsz * seq, d_model), out_a.reshape(bsz * seq, A_WIDTH),
              out_b.reshape(bsz * seq, B_WIDTH), w_out_b[:A_WIDTH], w_out_b[A_WIDTH:],
              ffn_norm_g[layer].astype(F32)[None, :],
              w_gate[layer].astype(BF16), w_up[layer].astype(BF16), w_down[layer].astype(BF16))
    return y.reshape(bsz, seq, d_model)
```

```python
import functools
import math

import jax
import jax.numpy as jnp
from jax import lax
from jax.experimental import pallas as pl
from jax.experimental.pallas import tpu as pltpu

N_META = 16
GRID_W = 64
HEAD_DIM = 64
ROPE_THETA = 10000.0
EPS = 1e-6
A_HEADS = 8
A_KV_HEADS = 2
A_GROUP = A_HEADS // A_KV_HEADS
A_WIDTH = A_HEADS * HEAD_DIM
A_KV_WIDTH = A_KV_HEADS * HEAD_DIM
B_HEADS = 4
B_VDIM = 2 * HEAD_DIM
B_WIDTH = B_HEADS * B_VDIM
LAM_INIT = 0.8 - 0.6 * math.exp(-0.3 * 0)

LANES = 128
VMEM_LIMIT_BYTES = 56 * 1024 * 1024

Q_PRESCALE = (HEAD_DIM ** -0.5) * math.log2(math.e)
MASK_BIAS = -1e30
UNSHIFTED_SCORE_LIMIT = 60.0

PROJ_ROWS = 512
PROJ_ROW_BLOCK = 256
PROJ_GROUP_COLS = 4
META_ROWS = LANES
MXU_TILE = 256
Q_TILES_PER_STEP = 2
A_Q_COLS = MXU_TILE
A_KV_CHUNK = MXU_TILE
A_LOOKAHEAD = 8
B_Q_COLS = 512
B_HEADS_PER_STEP = 2
KV_CHUNK = 512
A_STEP_COLS = A_Q_COLS * Q_TILES_PER_STEP
B_STEP_COLS = B_Q_COLS * Q_TILES_PER_STEP
POST_ROWS = 512

BF16 = jnp.bfloat16
F32 = jnp.float32


def _dot(a, b):
    return jnp.dot(a, b, preferred_element_type=F32)


def _lane_index(shape):
    return lax.broadcasted_iota(jnp.int32, shape, len(shape) - 1)


def _row_index(shape):
    return lax.broadcasted_iota(jnp.int32, shape, 0)


def _proj_kernel(x_ref, g_ref, w_ref, cos_a_ref, sin_a_ref, cos_b_ref, sin_b_ref,
                 gains_ref, seg_ref,
                 qa_ref, ka_ref, va_ref, qb_ref, kb_ref, vb_ref):
    rows = x_ref.shape[0]
    blk = min(rows, PROJ_ROW_BLOCK)
    lane = _lane_index((blk, LANES))
    even = (lane % 2) == 0
    low_half = lane < HEAD_DIM
    seg = seg_ref[...]
    g_aq = gains_ref[0:1, :] * Q_PRESCALE
    g_ak = gains_ref[1:2, :]
    g_bq = gains_ref[2:3, :] * Q_PRESCALE
    g_bk = gains_ref[3:4, :]

    def norm_rope(col, gain, cos_ref, sin_ref, rb):
        ss = _dot((col * col).astype(BF16), seg)
        y = col * lax.rsqrt(ss * (1.0 / HEAD_DIM) + EPS) * gain
        swapped = jnp.where(even, pltpu.roll(y, LANES - 1, 1), pltpu.roll(y, 1, 1))
        return y * cos_ref[rb, :] + swapped * sin_ref[rb, :]

    def emit_qa(j, col, rb):
        col = norm_rope(col, g_aq, cos_a_ref, sin_a_ref, rb)
        flipped = pltpu.roll(col, HEAD_DIM, 1)
        for half in range(2):
            head = 2 * j + half
            kv = head // A_GROUP
            src = col if half == kv else flipped
            keep = low_half if kv == 0 else jnp.logical_not(low_half)
            qa_ref[head, :, rb] = jnp.where(keep, src, 0.0).T.astype(BF16)

    def emit_ka(j, col, rb):
        ka_ref[rb, :] = norm_rope(col, g_ak, cos_a_ref, sin_a_ref, rb).astype(BF16)

    def emit_va(j, col, rb):
        va_ref[:, rb] = col.T.astype(BF16)

    def emit_qb(j, col, rb):
        qb_ref[j, :, rb] = norm_rope(col, g_bq, cos_b_ref, sin_b_ref, rb).T.astype(BF16)

    def emit_kb(j, col, rb):
        kb_ref[rb, j * LANES:(j + 1) * LANES] = norm_rope(col, g_bk, cos_b_ref, sin_b_ref, rb).astype(BF16)

    def emit_vb(j, col, rb):
        vb_ref[j * LANES:(j + 1) * LANES, rb] = col.T.astype(BF16)

    handlers = ([(emit_qa, j) for j in range(A_WIDTH // LANES)] + [(emit_ka, 0), (emit_va, 0)]
                + [(emit_qb, j) for j in range(B_HEADS)] + [(emit_kb, j) for j in range(B_HEADS)]
                + [(emit_vb, j) for j in range(B_HEADS)])
    assert len(handlers) * LANES == w_ref.shape[1]
    groups = [range(s, min(s + PROJ_GROUP_COLS, len(handlers)))
              for s in range(0, len(handlers), PROJ_GROUP_COLS)]

    row_blocks = [slice(r, r + blk) for r in range(0, rows, blk)]

    def normed(rb):
        x = x_ref[rb, :]
        ms = jnp.mean(x * x, axis=-1, keepdims=True)
        return (x * lax.rsqrt(ms + EPS) * g_ref[...]).astype(BF16)

    hs = [normed(rb) for rb in row_blocks]
    items = [(r, cols) for cols in groups for r in range(len(row_blocks))]

    def project(item):
        r, cols = item
        return _dot(hs[r], w_ref[:, cols.start * LANES:cols.stop * LANES])

    slab_next = project(items[0])
    for pos, (r, cols) in enumerate(items):
        slab = slab_next
        if pos + 1 < len(items):
            slab_next = project(items[pos + 1])
        for n, c in enumerate(cols):
            fn, j = handlers[c]
            fn(j, slab[:, n * LANES:(n + 1) * LANES], row_blocks[r])


def _project(x3, g, w_bf16, tables, gains, seg, rows_per_step):
    bsz, n_rows, d_model = x3.shape
    in_width = w_bf16.shape[1]
    t = rows_per_step
    tab_blk = pl.BlockSpec((t, LANES), lambda b, i: (i, 0))
    const = lambda shape: pl.BlockSpec(shape, lambda b, i: (0, 0))
    out_specs = [
        pl.BlockSpec((None, A_HEADS, LANES, t), lambda b, i: (b, 0, 0, i)),
        pl.BlockSpec((None, t, LANES), lambda b, i: (b, i, 0)),
        pl.BlockSpec((None, LANES, t), lambda b, i: (b, 0, i)),
        pl.BlockSpec((None, B_HEADS, LANES, t), lambda b, i: (b, 0, 0, i)),
        pl.BlockSpec((None, t, B_WIDTH), lambda b, i: (b, i, 0)),
        pl.BlockSpec((None, B_WIDTH, t), lambda b, i: (b, 0, i)),
    ]
    out_shapes = [(bsz, A_HEADS, LANES, n_rows), (bsz, n_rows, LANES), (bsz, LANES, n_rows),
                  (bsz, B_HEADS, LANES, n_rows), (bsz, n_rows, B_WIDTH), (bsz, B_WIDTH, n_rows)]
    return pl.pallas_call(
        _proj_kernel,
        grid=(bsz, n_rows // t),
        in_specs=[pl.BlockSpec((None, t, d_model), lambda b, i: (b, i, 0)),
                  const((1, d_model)), const((d_model, in_width)),
                  tab_blk, tab_blk, tab_blk, tab_blk, const(gains.shape), const(seg.shape)],
        out_specs=out_specs,
        out_shape=[jax.ShapeDtypeStruct(s, BF16) for s in out_shapes],
        compiler_params=pltpu.CompilerParams(
            dimension_semantics=("arbitrary", "arbitrary"), vmem_limit_bytes=VMEM_LIMIT_BYTES),
        name="proj",
    )(x3, g, w_bf16, *tables, gains, seg)


def _attend_t(groups, bias_ref, running_max, finish, lookahead=1, kv_chunk=KV_CHUNK):
    streams = [s for group in groups for s in group]
    n_chunks = streams[0][1].shape[0] // kv_chunk
    items, group_end, base = [], {}, 0
    for g, group in enumerate(groups):
        ids = range(base, base + len(group))
        items += [(i, None) for i in ids] + [(i, c) for c in range(n_chunks) for i in ids]
        group_end[len(items) - 1] = (g, ids)
        base += len(group)

    def scores(item):
        i, c = item
        qt, k_ref, _, km, _ = streams[i]
        if c is None:
            bias = jnp.concatenate([bias_ref[...]] * (qt.shape[1] // LANES), axis=1)
            return _dot(km, qt) + bias
        return _dot(k_ref[c * kv_chunk:(c + 1) * kv_chunk, :], qt)

    def update(i, c, s, state):
        _, _, vt_ref, _, vmt = streams[i]
        vt = vmt if c is None else vt_ref[:, c * kv_chunk:(c + 1) * kv_chunk]
        if not running_max:
            p = jnp.exp2(s)
            l, acc = jnp.sum(p, axis=0, keepdims=True), _dot(vt, p.astype(BF16))
            if state is not None:
                l, acc = state[1] + l, state[2] + acc
            return (None, l, acc)
        s_max = jnp.max(s, axis=0, keepdims=True)
        if state is None:
            p = jnp.exp2(s - s_max)
            return (s_max, jnp.sum(p, axis=0, keepdims=True), _dot(vt, p.astype(BF16)))
        m, l, acc = state
        m_new = jnp.maximum(m, s_max)
        alpha = jnp.exp2(m - m_new)
        p = jnp.exp2(s - m_new)
        return (m_new, alpha * l + jnp.sum(p, axis=0, keepdims=True),
                alpha * acc + _dot(vt, p.astype(BF16)))

    state = [None] * len(streams)
    pending = [scores(item) for item in items[:lookahead]]
    for pos, (i, c) in enumerate(items):
        s = pending.pop(0)
        if pos + lookahead < len(items):
            pending.append(scores(items[pos + lookahead]))
        state[i] = update(i, c, s, state[i])
        if pos in group_end:
            g, ids = group_end[pos]
            finish(g, [state[j][2] / state[j][1] for j in ids])


def _attn_a_kernel(q_ref, k_ref, vt_ref, km_ref, vmt_ref, bias_ref, o_ref, *, running_max):
    tq = A_Q_COLS
    km, vmt = km_ref[...], vmt_ref[...]
    groups = []
    for t in range(q_ref.shape[2] // tq):
        group = []
        for head in range(A_HEADS):
            rows = slice((head // A_GROUP) * HEAD_DIM, (head // A_GROUP + 1) * HEAD_DIM)
            group.append((q_ref[head, :, t * tq:(t + 1) * tq], k_ref, vt_ref.at[rows, :], km, vmt[rows, :]))
        groups.append(group)

    def finish(t, outs):
        o_ref[t * tq:(t + 1) * tq, :] = jnp.concatenate(outs, axis=0).T.astype(o_ref.dtype)

    _attend_t(groups, bias_ref, running_max, finish, lookahead=A_LOOKAHEAD, kv_chunk=A_KV_CHUNK)


def _attn_b_kernel(q_ref, k_ref, vt_ref, km_ref, vmt_ref, bias_ref, lam_ref, subln_ref, o_ref, *,
                   running_max):
    tq = B_Q_COLS
    first = _row_index((LANES, tq)) < HEAD_DIM
    zero = jnp.zeros((LANES, tq), BF16)
    groups = []
    for t in range(q_ref.shape[2] // tq):
        group = []
        for h in range(B_HEADS_PER_STEP):
            q = q_ref[h, :, t * tq:(t + 1) * tq]
            cols = slice(h * LANES, (h + 1) * LANES)
            qt = jnp.concatenate([jnp.where(first, q, zero), jnp.where(first, zero, q)], axis=1)
            group.append((qt, k_ref.at[:, cols], vt_ref.at[cols, :], km_ref[:, cols], vmt_ref[cols, :]))
        groups.append(group)
    lam_p = lam_ref[...]
    lam = (jnp.exp(jnp.sum(lam_p[0:1] * lam_p[1:2], axis=-1, keepdims=True))
           - jnp.exp(jnp.sum(lam_p[2:3] * lam_p[3:4], axis=-1, keepdims=True)) + LAM_INIT)

    def finish(t, outs):
        ys = []
        for o in outs:
            d = o[:, :tq] - lam * o[:, tq:]
            ms = jnp.mean(d * d, axis=0, keepdims=True)
            ys.append(d * lax.rsqrt(ms + EPS) * subln_ref[...] * (1.0 - LAM_INIT))
        o_ref[t * tq:(t + 1) * tq, :] = jnp.concatenate(ys, axis=0).T.astype(o_ref.dtype)

    _attend_t(groups, bias_ref, running_max, finish)


def _attention_a(qat, ka, vat, kam, vamt, bias, running_max):
    bsz, _, _, seq = qat.shape
    const = lambda shape: pl.BlockSpec(shape, lambda b, i: (0,) * len(shape))
    return pl.pallas_call(
        functools.partial(_attn_a_kernel, running_max=running_max),
        grid=(bsz, seq // A_STEP_COLS),
        in_specs=[pl.BlockSpec((None, A_HEADS, LANES, A_STEP_COLS), lambda b, i: (b, 0, 0, i)),
                  pl.BlockSpec((None, seq, LANES), lambda b, i: (b, 0, 0)),
                  pl.BlockSpec((None, LANES, seq), lambda b, i: (b, 0, 0)),
                  const(kam.shape), const(vamt.shape), const(bias.shape)],
        out_specs=pl.BlockSpec((None, A_STEP_COLS, A_WIDTH), lambda b, i: (b, i, 0)),
        out_shape=jax.ShapeDtypeStruct((bsz, seq, A_WIDTH), BF16),
        compiler_params=pltpu.CompilerParams(
            dimension_semantics=("arbitrary", "arbitrary"), vmem_limit_bytes=VMEM_LIMIT_BYTES),
        name="attn_a",
    )(qat, ka, vat, kam, vamt, bias)


def _attention_b(qbt, kb, vbt, kbm, vbmt, bias, lam_params, subln_col, running_max):
    bsz, _, _, seq = qbt.shape
    hp = B_HEADS_PER_STEP
    width = hp * LANES
    const = lambda shape: pl.BlockSpec(shape, lambda b, h, i: (0,) * len(shape))
    return pl.pallas_call(
        functools.partial(_attn_b_kernel, running_max=running_max),
        grid=(bsz, B_HEADS // hp, seq // B_STEP_COLS),
        in_specs=[pl.BlockSpec((None, hp, LANES, B_STEP_COLS), lambda b, h, i: (b, h, 0, i)),
                  pl.BlockSpec((None, seq, width), lambda b, h, i: (b, 0, h)),
                  pl.BlockSpec((None, width, seq), lambda b, h, i: (b, h, 0)),
                  pl.BlockSpec((LANES, width), lambda b, h, i: (0, h)),
                  pl.BlockSpec((width, LANES), lambda b, h, i: (h, 0)),
                  const(bias.shape), const(lam_params.shape), const(subln_col.shape)],
        out_specs=pl.BlockSpec((None, B_STEP_COLS, width), lambda b, h, i: (b, i, h)),
        out_shape=jax.ShapeDtypeStruct((bsz, seq, B_WIDTH), BF16),
        compiler_params=pltpu.CompilerParams(
            dimension_semantics=("arbitrary", "arbitrary", "arbitrary"),
            vmem_limit_bytes=VMEM_LIMIT_BYTES),
        name="attn_b",
    )(qbt, kb, vbt, kbm, vbmt, bias, lam_params, subln_col)


def _post_kernel(x_ref, oa_ref, ob_ref, woa_ref, wob_ref, g_ref, wg_ref, wu_ref, wd_ref, y_ref):
    x1 = x_ref[...] + _dot(oa_ref[...], woa_ref[...]) + _dot(ob_ref[...], wob_ref[...])
    ms = jnp.mean(x1 * x1, axis=-1, keepdims=True)
    h = (x1 * lax.rsqrt(ms + EPS) * g_ref[...]).astype(BF16)
    gate = _dot(h, wg_ref[...])
    up = _dot(h, wu_ref[...])
    act = (gate * jax.nn.sigmoid(gate) * up).astype(BF16)
    y_ref[...] = x1 + _dot(act, wd_ref[...])


def _post(x_rows, oa, ob, wo_a, wo_b, g, wg, wu, wd):
    n_rows, d_model = x_rows.shape
    row_blk = lambda width: pl.BlockSpec((POST_ROWS, width), lambda i: (i, 0))
    resident = lambda a: pl.BlockSpec(a.shape, lambda i: (0, 0), pipeline_mode=pl.Buffered(1))
    return pl.pallas_call(
        _post_kernel,
        grid=(n_rows // POST_ROWS,),
        in_specs=[row_blk(d_model), row_blk(A_WIDTH), row_blk(B_WIDTH),
                  resident(wo_a), resident(wo_b), resident(g),
                  resident(wg), resident(wu), resident(wd)],
        out_specs=row_blk(d_model),
        out_shape=jax.ShapeDtypeStruct((n_rows, d_model), F32),
        compiler_params=pltpu.CompilerParams(
            dimension_semantics=("arbitrary",), vmem_limit_bytes=VMEM_LIMIT_BYTES),
        name="post",
    )(x_rows, oa, ob, wo_a, wo_b, g, wg, wu, wd)


def _pair_tables(ang):
    c, s = jnp.cos(ang), jnp.sin(ang)
    cos64 = jnp.repeat(c, 2, axis=-1)
    sin64 = jnp.stack([-s, s], axis=-1).reshape(ang.shape[0], HEAD_DIM)
    return jnp.tile(cos64, (1, 2)), jnp.tile(sin64, (1, 2))


def _rope_tables(seq):
    half = HEAD_DIM // 2
    t = jnp.arange(seq)
    inv_a = ROPE_THETA ** (-jnp.arange(0, half, 2, dtype=F32) / half)
    r = (t // GRID_W).astype(F32)
    c = (t % GRID_W).astype(F32)
    ang_a = jnp.concatenate([r[:, None] * inv_a[None, :], c[:, None] * inv_a[None, :]], axis=-1)
    inv_b = ROPE_THETA ** (-jnp.arange(0, HEAD_DIM, 2, dtype=F32) / HEAD_DIM)
    pos = jnp.arange(N_META + seq, dtype=F32)
    ang_b = pos[:, None] * inv_b[None, :]
    seq_tabs = _pair_tables(ang_a) + _pair_tables(ang_b[N_META:])
    meta_tabs = _pair_tables(jnp.zeros((N_META, half), F32)) + _pair_tables(ang_b[:N_META])
    return seq_tabs, tuple(_pad_rows(t, META_ROWS) for t in meta_tabs)


def _pad_rows(a, rows):
    return jnp.pad(a, ((0, rows - a.shape[0]), (0, 0)))


def kernel(x, meta_tokens, attn_norm_g, w_in, a_q_norm_g, a_k_norm_g, b_q_norm_g, b_k_norm_g,
           b_lambda_q1, b_lambda_k1, b_lambda_q2, b_lambda_k2, b_subln_g, w_out,
           ffn_norm_g, w_gate, w_up, w_down):
    bsz, seq, d_model = x.shape
    assert w_in.shape[0] == 1, "single-layer block"
    assert seq % PROJ_ROWS == 0 and seq % KV_CHUNK == 0 and seq % A_STEP_COLS == 0 and seq % B_STEP_COLS == 0
    layer = 0

    seq_tabs, meta_tabs = _rope_tables(seq)
    gains = jnp.stack([jnp.tile(g[layer].astype(F32), 2)
                       for g in (a_q_norm_g, a_k_norm_g, b_q_norm_g, b_k_norm_g)])
    idx = jnp.arange(LANES) // HEAD_DIM
    seg = (idx[:, None] == idx[None, :]).astype(BF16)
    w_in_b = w_in[layer].astype(BF16)
    g_attn = attn_norm_g[layer].astype(F32)[None, :]

    qat, ka, vat, qbt, kb, vbt = _project(x, g_attn, w_in_b, seq_tabs, gains, seg, PROJ_ROWS)
    meta3 = _pad_rows(meta_tokens.astype(F32), META_ROWS)[None]
    _, kam, vamt, _, kbm, vbmt = _project(meta3, g_attn, w_in_b, meta_tabs, gains, seg, META_ROWS)
    kam, vamt, kbm, vbmt = kam[0], vamt[0], kbm[0], vbmt[0]
    bias = jnp.where(jnp.arange(META_ROWS) < N_META, 0.0, MASK_BIAS).astype(F32)
    bias = jnp.broadcast_to(bias[:, None], (META_ROWS, LANES))

    lam_params = jnp.stack([jnp.pad(p[layer].astype(F32), (0, LANES - HEAD_DIM))
                            for p in (b_lambda_q1, b_lambda_k1, b_lambda_q2, b_lambda_k2)])
    subln_col = b_subln_g[layer].astype(F32)[:, None]

    def attend(running_max):
        return (_attention_a(qat, ka, vat, kam, vamt, bias, running_max),
                _attention_b(qbt, kb, vbt, kbm, vbmt, bias, lam_params, subln_col, running_max))

    amax = lambda g: jnp.max(jnp.abs(g[layer].astype(F32)))
    score_bound = HEAD_DIM * Q_PRESCALE * jnp.maximum(amax(a_q_norm_g) * amax(a_k_norm_g),
                                                      amax(b_q_norm_g) * amax(b_k_norm_g))
    out_a, out_b = lax.cond(score_bound <= UNSHIFTED_SCORE_LIMIT,
                            lambda: attend(False), lambda: attend(True))

    w_out_b = w_out[layer].astype(BF16)
    y = _post(x.reshape(bsz * seq, d_model), out_a.reshape(bsz * seq, A_WIDTH),
              out_b.reshape(bsz * seq, B_WIDTH), w_out_b[:A_WIDTH], w_out_b[A_WIDTH:],
              ffn_norm_g[layer].astype(F32)[None, :],
              w_gate[layer].astype(BF16), w_up[layer].astype(BF16), w_down[layer].astype(BF16))
    return y.reshape(bsz, seq, d_model)
```

```python
import functools
import math

import jax
import jax.numpy as jnp
from jax import lax
from jax.experimental import pallas as pl
from jax.experimental.pallas import tpu as pltpu

N_META = 16
GRID_W = 64
HEAD_DIM = 64
ROPE_THETA = 10000.0
EPS = 1e-6
A_HEADS = 8
A_KV_HEADS = 2
A_GROUP = A_HEADS // A_KV_HEADS
A_WIDTH = A_HEADS * HEAD_DIM
A_KV_WIDTH = A_KV_HEADS * HEAD_DIM
B_HEADS = 4
B_VDIM = 2 * HEAD_DIM
B_WIDTH = B_HEADS * B_VDIM
LAM_INIT = 0.8 - 0.6 * math.exp(-0.3 * 0)

LANES = 128
VMEM_LIMIT_BYTES = 56 * 1024 * 1024

Q_PRESCALE = (HEAD_DIM ** -0.5) * math.log2(math.e)
MASK_BIAS = -1e30
UNSHIFTED_SCORE_LIMIT = 60.0

PROJ_ROWS = 1024
PROJ_ROW_BLOCK = 256
PROJ_GROUP_COLS = 4
META_ROWS = LANES
MXU_TILE = 256
Q_TILES_PER_STEP = 2
A_Q_COLS = MXU_TILE
A_KV_CHUNK = MXU_TILE
A_LOOKAHEAD = 8
B_Q_COLS = 512
B_HEADS_PER_STEP = 2
B_KV_CHUNK = 512
B_LOOKAHEAD = 1
KV_CHUNK = 512
A_STEP_COLS = A_Q_COLS * Q_TILES_PER_STEP
B_STEP_COLS = B_Q_COLS * Q_TILES_PER_STEP
POST_ROWS = 512
POST_ROW_BLOCK = 256

BF16 = jnp.bfloat16
F32 = jnp.float32


def _dot(a, b):
    return jnp.dot(a, b, preferred_element_type=F32)


def _lane_index(shape):
    return lax.broadcasted_iota(jnp.int32, shape, len(shape) - 1)


def _row_index(shape):
    return lax.broadcasted_iota(jnp.int32, shape, 0)


def _proj_kernel(x_ref, g_ref, w_ref, cos_a_ref, sin_a_ref, cos_b_ref, sin_b_ref,
                 gains_ref, seg_ref,
                 qa_ref, ka_ref, va_ref, qb_ref, kb_ref, vb_ref):
    rows = x_ref.shape[0]
    blk = min(rows, PROJ_ROW_BLOCK)
    lane = _lane_index((blk, LANES))
    even = (lane % 2) == 0
    low_half = lane < HEAD_DIM
    seg = seg_ref[...]
    g_aq = gains_ref[0:1, :] * Q_PRESCALE
    g_ak = gains_ref[1:2, :]
    g_bq = gains_ref[2:3, :] * Q_PRESCALE
    g_bk = gains_ref[3:4, :]

    def norm_rope(col, gain, cos_ref, sin_ref, rb):
        ss = _dot((col * col).astype(BF16), seg)
        y = col * lax.rsqrt(ss * (1.0 / HEAD_DIM) + EPS) * gain
        swapped = jnp.where(even, pltpu.roll(y, LANES - 1, 1), pltpu.roll(y, 1, 1))
        return y * cos_ref[rb, :] + swapped * sin_ref[rb, :]

    def emit_qa(j, col, rb):
        col = norm_rope(col, g_aq, cos_a_ref, sin_a_ref, rb)
        flipped = pltpu.roll(col, HEAD_DIM, 1)
        for half in range(2):
            head = 2 * j + half
            kv = head // A_GROUP
            src = col if half == kv else flipped
            keep = low_half if kv == 0 else jnp.logical_not(low_half)
            qa_ref[head, :, rb] = jnp.where(keep, src, 0.0).T.astype(BF16)

    def emit_ka(j, col, rb):
        ka_ref[rb, :] = norm_rope(col, g_ak, cos_a_ref, sin_a_ref, rb).astype(BF16)

    def emit_va(j, col, rb):
        va_ref[:, rb] = col.T.astype(BF16)

    def emit_qb(j, col, rb):
        qb_ref[j, :, rb] = norm_rope(col, g_bq, cos_b_ref, sin_b_ref, rb).T.astype(BF16)

    def emit_kb(j, col, rb):
        kb_ref[rb, j * LANES:(j + 1) * LANES] = norm_rope(col, g_bk, cos_b_ref, sin_b_ref, rb).astype(BF16)

    def emit_vb(j, col, rb):
        vb_ref[j * LANES:(j + 1) * LANES, rb] = col.T.astype(BF16)

    handlers = ([(emit_qa, j) for j in range(A_WIDTH // LANES)] + [(emit_ka, 0), (emit_va, 0)]
                + [(emit_qb, j) for j in range(B_HEADS)] + [(emit_kb, j) for j in range(B_HEADS)]
                + [(emit_vb, j) for j in range(B_HEADS)])
    assert len(handlers) * LANES == w_ref.shape[1]
    groups = [range(s, min(s + PROJ_GROUP_COLS, len(handlers)))
              for s in range(0, len(handlers), PROJ_GROUP_COLS)]

    row_blocks = [slice(r, r + blk) for r in range(0, rows, blk)]

    def normed(rb):
        x = x_ref[rb, :]
        ms = jnp.mean(x * x, axis=-1, keepdims=True)
        return (x * lax.rsqrt(ms + EPS) * g_ref[...]).astype(BF16)

    hs = [normed(rb) for rb in row_blocks]
    items = [(r, cols) for cols in groups for r in range(len(row_blocks))]

    def project(item):
        r, cols = item
        return _dot(hs[r], w_ref[:, cols.start * LANES:cols.stop * LANES])

    slab_next = project(items[0])
    for pos, (r, cols) in enumerate(items):
        slab = slab_next
        if pos + 1 < len(items):
            slab_next = project(items[pos + 1])
        for n, c in enumerate(cols):
            fn, j = handlers[c]
            fn(j, slab[:, n * LANES:(n + 1) * LANES], row_blocks[r])


def _project(x3, g, w_bf16, tables, gains, seg, rows_per_step):
    bsz, n_rows, d_model = x3.shape
    in_width = w_bf16.shape[1]
    t = rows_per_step
    tab_blk = pl.BlockSpec((t, LANES), lambda b, i: (i, 0))
    const = lambda shape: pl.BlockSpec(shape, lambda b, i: (0, 0))
    out_specs = [
        pl.BlockSpec((None, A_HEADS, LANES, t), lambda b, i: (b, 0, 0, i)),
        pl.BlockSpec((None, t, LANES), lambda b, i: (b, i, 0)),
        pl.BlockSpec((None, LANES, t), lambda b, i: (b, 0, i)),
        pl.BlockSpec((None, B_HEADS, LANES, t), lambda b, i: (b, 0, 0, i)),
        pl.BlockSpec((None, t, B_WIDTH), lambda b, i: (b, i, 0)),
        pl.BlockSpec((None, B_WIDTH, t), lambda b, i: (b, 0, i)),
    ]
    out_shapes = [(bsz, A_HEADS, LANES, n_rows), (bsz, n_rows, LANES), (bsz, LANES, n_rows),
                  (bsz, B_HEADS, LANES, n_rows), (bsz, n_rows, B_WIDTH), (bsz, B_WIDTH, n_rows)]
    return pl.pallas_call(
        _proj_kernel,
        grid=(bsz, n_rows // t),
        in_specs=[pl.BlockSpec((None, t, d_model), lambda b, i: (b, i, 0)),
                  const((1, d_model)), const((d_model, in_width)),
                  tab_blk, tab_blk, tab_blk, tab_blk, const(gains.shape), const(seg.shape)],
        out_specs=out_specs,
        out_shape=[jax.ShapeDtypeStruct(s, BF16) for s in out_shapes],
        compiler_params=pltpu.CompilerParams(
            dimension_semantics=("arbitrary", "arbitrary"), vmem_limit_bytes=VMEM_LIMIT_BYTES),
        name="proj",
    )(x3, g, w_bf16, *tables, gains, seg)


def _attend_t(groups, bias_ref, running_max, finish, lookahead=1, kv_chunk=KV_CHUNK):
    streams = [s for group in groups for s in group]
    n_chunks = streams[0][1].shape[0] // kv_chunk
    items, group_end, base = [], {}, 0
    for g, group in enumerate(groups):
        ids = range(base, base + len(group))
        items += [(i, None) for i in ids] + [(i, c) for c in range(n_chunks) for i in ids]
        group_end[len(items) - 1] = (g, ids)
        base += len(group)

    def scores(item):
        i, c = item
        qt, k_ref, _, km, _ = streams[i]
        if c is None:
            bias = jnp.concatenate([bias_ref[...]] * (qt.shape[1] // LANES), axis=1)
            return _dot(km, qt) + bias
        return _dot(k_ref[c * kv_chunk:(c + 1) * kv_chunk, :], qt)

    def update(i, c, s, state):
        _, _, vt_ref, _, vmt = streams[i]
        vt = vmt if c is None else vt_ref[:, c * kv_chunk:(c + 1) * kv_chunk]
        if not running_max:
            p = jnp.exp2(s)
            l, acc = jnp.sum(p, axis=0, keepdims=True), _dot(vt, p.astype(BF16))
            if state is not None:
                l, acc = state[1] + l, state[2] + acc
            return (None, l, acc)
        s_max = jnp.max(s, axis=0, keepdims=True)
        if state is None:
            p = jnp.exp2(s - s_max)
            return (s_max, jnp.sum(p, axis=0, keepdims=True), _dot(vt, p.astype(BF16)))
        m, l, acc = state
        m_new = jnp.maximum(m, s_max)
        alpha = jnp.exp2(m - m_new)
        p = jnp.exp2(s - m_new)
        return (m_new, alpha * l + jnp.sum(p, axis=0, keepdims=True),
                alpha * acc + _dot(vt, p.astype(BF16)))

    state = [None] * len(streams)
    pending = [scores(item) for item in items[:lookahead]]
    for pos, (i, c) in enumerate(items):
        s = pending.pop(0)
        if pos + lookahead < len(items):
            pending.append(scores(items[pos + lookahead]))
        state[i] = update(i, c, s, state[i])
        if pos in group_end:
            g, ids = group_end[pos]
            finish(g, [state[j][2] / state[j][1] for j in ids])


def _attn_a_kernel(q_ref, k_ref, vt_ref, km_ref, vmt_ref, bias_ref, o_ref, *, running_max):
    tq = A_Q_COLS
    km, vmt = km_ref[...], vmt_ref[...]
    groups = []
    for t in range(q_ref.shape[2] // tq):
        group = []
        for head in range(A_HEADS):
            rows = slice((head // A_GROUP) * HEAD_DIM, (head // A_GROUP + 1) * HEAD_DIM)
            group.append((q_ref[head, :, t * tq:(t + 1) * tq], k_ref, vt_ref.at[rows, :], km, vmt[rows, :]))
        groups.append(group)

    def finish(t, outs):
        o_ref[t * tq:(t + 1) * tq, :] = jnp.concatenate(outs, axis=0).T.astype(o_ref.dtype)

    _attend_t(groups, bias_ref, running_max, finish, lookahead=A_LOOKAHEAD, kv_chunk=A_KV_CHUNK)


def _attn_b_kernel(q_ref, k_ref, vt_ref, km_ref, vmt_ref, bias_ref, lam_ref, subln_ref, o_ref, *,
                   running_max):
    tq = B_Q_COLS
    first = _row_index((LANES, tq)) < HEAD_DIM
    zero = jnp.zeros((LANES, tq), BF16)
    groups = []
    for t in range(q_ref.shape[2] // tq):
        group = []
        for h in range(B_HEADS_PER_STEP):
            q = q_ref[h, :, t * tq:(t + 1) * tq]
            cols = slice(h * LANES, (h + 1) * LANES)
            qt = jnp.concatenate([jnp.where(first, q, zero), jnp.where(first, zero, q)], axis=1)
            group.append((qt, k_ref.at[:, cols], vt_ref.at[cols, :], km_ref[:, cols], vmt_ref[cols, :]))
        groups.append(group)
    lam_p = lam_ref[...]
    lam = (jnp.exp(jnp.sum(lam_p[0:1] * lam_p[1:2], axis=-1, keepdims=True))
           - jnp.exp(jnp.sum(lam_p[2:3] * lam_p[3:4], axis=-1, keepdims=True)) + LAM_INIT)

    def finish(t, outs):
        ys = []
        for o in outs:
            d = o[:, :tq] - lam * o[:, tq:]
            ms = jnp.mean(d * d, axis=0, keepdims=True)
            ys.append(d * lax.rsqrt(ms + EPS) * subln_ref[...] * (1.0 - LAM_INIT))
        o_ref[t * tq:(t + 1) * tq, :] = jnp.concatenate(ys, axis=0).T.astype(o_ref.dtype)

    _attend_t(groups, bias_ref, running_max, finish, lookahead=B_LOOKAHEAD, kv_chunk=B_KV_CHUNK)


def _attention_a(qat, ka, vat, kam, vamt, bias, running_max):
    bsz, _, _, seq = qat.shape
    const = lambda shape: pl.BlockSpec(shape, lambda b, i: (0,) * len(shape))
    return pl.pallas_call(
        functools.partial(_attn_a_kernel, running_max=running_max),
        grid=(bsz, seq // A_STEP_COLS),
        in_specs=[pl.BlockSpec((None, A_HEADS, LANES, A_STEP_COLS), lambda b, i: (b, 0, 0, i)),
                  pl.BlockSpec((None, seq, LANES), lambda b, i: (b, 0, 0)),
                  pl.BlockSpec((None, LANES, seq), lambda b, i: (b, 0, 0)),
                  const(kam.shape), const(vamt.shape), const(bias.shape)],
        out_specs=pl.BlockSpec((None, A_STEP_COLS, A_WIDTH), lambda b, i: (b, i, 0)),
        out_shape=jax.ShapeDtypeStruct((bsz, seq, A_WIDTH), BF16),
        compiler_params=pltpu.CompilerParams(
            dimension_semantics=("arbitrary", "arbitrary"), vmem_limit_bytes=VMEM_LIMIT_BYTES),
        name="attn_a",
    )(qat, ka, vat, kam, vamt, bias)


def _attention_b(qbt, kb, vbt, kbm, vbmt, bias, lam_params, subln_col, running_max):
    bsz, _, _, seq = qbt.shape
    hp = B_HEADS_PER_STEP
    width = hp * LANES
    const = lambda shape: pl.BlockSpec(shape, lambda b, h, i: (0,) * len(shape))
    return pl.pallas_call(
        functools.partial(_attn_b_kernel, running_max=running_max),
        grid=(bsz, B_HEADS // hp, seq // B_STEP_COLS),
        in_specs=[pl.BlockSpec((None, hp, LANES, B_STEP_COLS), lambda b, h, i: (b, h, 0, i)),
                  pl.BlockSpec((None, seq, width), lambda b, h, i: (b, 0, h)),
                  pl.BlockSpec((None, width, seq), lambda b, h, i: (b, h, 0)),
                  pl.BlockSpec((LANES, width), lambda b, h, i: (0, h)),
                  pl.BlockSpec((width, LANES), lambda b, h, i: (h, 0)),
                  const(bias.shape), const(lam_params.shape), const(subln_col.shape)],
        out_specs=pl.BlockSpec((None, B_STEP_COLS, width), lambda b, h, i: (b, i, h)),
        out_shape=jax.ShapeDtypeStruct((bsz, seq, B_WIDTH), BF16),
        compiler_params=pltpu.CompilerParams(
            dimension_semantics=("arbitrary", "arbitrary", "arbitrary"),
            vmem_limit_bytes=VMEM_LIMIT_BYTES),
        name="attn_b",
    )(qbt, kb, vbt, kbm, vbmt, bias, lam_params, subln_col)


def _post_kernel(x_ref, oa_ref, ob_ref, woa_ref, wob_ref, g_ref, wg_ref, wu_ref, wd_ref, y_ref):
    blocks = [slice(r, r + POST_ROW_BLOCK) for r in range(0, x_ref.shape[0], POST_ROW_BLOCK)]
    x1 = [x_ref[rb, :] + _dot(oa_ref[rb, :], woa_ref[...]) + _dot(ob_ref[rb, :], wob_ref[...])
          for rb in blocks]

    def normed(v):
        ms = jnp.mean(v * v, axis=-1, keepdims=True)
        return (v * lax.rsqrt(ms + EPS) * g_ref[...]).astype(BF16)

    gate_up = []
    for v in x1:
        h = normed(v)
        gate_up.append((_dot(h, wg_ref[...]), _dot(h, wu_ref[...])))
    for rb, v, (gate, up) in zip(blocks, x1, gate_up):
        act = (gate * jax.nn.sigmoid(gate) * up).astype(BF16)
        y_ref[rb, :] = v + _dot(act, wd_ref[...])


def _post(x_rows, oa, ob, wo_a, wo_b, g, wg, wu, wd):
    n_rows, d_model = x_rows.shape
    row_blk = lambda width: pl.BlockSpec((POST_ROWS, width), lambda i: (i, 0))
    resident = lambda a: pl.BlockSpec(a.shape, lambda i: (0, 0), pipeline_mode=pl.Buffered(1))
    return pl.pallas_call(
        _post_kernel,
        grid=(n_rows // POST_ROWS,),
        in_specs=[row_blk(d_model), row_blk(A_WIDTH), row_blk(B_WIDTH),
                  resident(wo_a), resident(wo_b), resident(g),
                  resident(wg), resident(wu), resident(wd)],
        out_specs=row_blk(d_model),
        out_shape=jax.ShapeDtypeStruct((n_rows, d_model), F32),
        compiler_params=pltpu.CompilerParams(
            dimension_semantics=("arbitrary",), vmem_limit_bytes=VMEM_LIMIT_BYTES),
        name="post",
    )(x_rows, oa, ob, wo_a, wo_b, g, wg, wu, wd)


def _pair_tables(ang):
    c, s = jnp.cos(ang), jnp.sin(ang)
    cos64 = jnp.repeat(c, 2, axis=-1)
    sin64 = jnp.stack([-s, s], axis=-1).reshape(ang.shape[0], HEAD_DIM)
    return jnp.tile(cos64, (1, 2)), jnp.tile(sin64, (1, 2))


def _rope_tables(seq):
    half = HEAD_DIM // 2
    t = jnp.arange(seq)
    inv_a = ROPE_THETA ** (-jnp.arange(0, half, 2, dtype=F32) / half)
    r = (t // GRID_W).astype(F32)
    c = (t % GRID_W).astype(F32)
    ang_a = jnp.concatenate([r[:, None] * inv_a[None, :], c[:, None] * inv_a[None, :]], axis=-1)
    inv_b = ROPE_THETA ** (-jnp.arange(0, HEAD_DIM, 2, dtype=F32) / HEAD_DIM)
    pos = jnp.arange(N_META + seq, dtype=F32)
    ang_b = pos[:, None] * inv_b[None, :]
    seq_tabs = _pair_tables(ang_a) + _pair_tables(ang_b[N_META:])
    meta_tabs = _pair_tables(jnp.zeros((N_META, half), F32)) + _pair_tables(ang_b[:N_META])
    return seq_tabs, tuple(_pad_rows(t, META_ROWS) for t in meta_tabs)


def _pad_rows(a, rows):
    return jnp.pad(a, ((0, rows - a.shape[0]), (0, 0)))


def kernel(x, meta_tokens, attn_norm_g, w_in, a_q_norm_g, a_k_norm_g, b_q_norm_g, b_k_norm_g,
           b_lambda_q1, b_lambda_k1, b_lambda_q2, b_lambda_k2, b_subln_g, w_out,
           ffn_norm_g, w_gate, w_up, w_down):
    bsz, seq, d_model = x.shape
    assert w_in.shape[0] == 1, "single-layer block"
    assert seq % PROJ_ROWS == 0 and seq % KV_CHUNK == 0 and seq % A_STEP_COLS == 0 and seq % B_STEP_COLS == 0
    layer = 0

    seq_tabs, meta_tabs = _rope_tables(seq)
    gains = jnp.stack([jnp.tile(g[layer].astype(F32), 2)
                       for g in (a_q_norm_g, a_k_norm_g, b_q_norm_g, b_k_norm_g)])
    idx = jnp.arange(LANES) // HEAD_DIM
    seg = (idx[:, None] == idx[None, :]).astype(BF16)
    w_in_b = w_in[layer].astype(BF16)
    g_attn = attn_norm_g[layer].astype(F32)[None, :]

    qat, ka, vat, qbt, kb, vbt = _project(x, g_attn, w_in_b, seq_tabs, gains, seg, PROJ_ROWS)
    meta3 = _pad_rows(meta_tokens.astype(F32), META_ROWS)[None]
    _, kam, vamt, _, kbm, vbmt = _project(meta3, g_attn, w_in_b, meta_tabs, gains, seg, META_ROWS)
    kam, vamt, kbm, vbmt = kam[0], vamt[0], kbm[0], vbmt[0]
    bias = jnp.where(jnp.arange(META_ROWS) < N_META, 0.0, MASK_BIAS).astype(F32)
    bias = jnp.broadcast_to(bias[:, None], (META_ROWS, LANES))

    lam_params = jnp.stack([jnp.pad(p[layer].astype(F32), (0, LANES - HEAD_DIM))
                            for p in (b_lambda_q1, b_lambda_k1, b_lambda_q2, b_lambda_k2)])
    subln_col = b_subln_g[layer].astype(F32)[:, None]

    def attend(running_max):
        return (_attention_a(qat, ka, vat, kam, vamt, bias, running_max),
                _attention_b(qbt, kb, vbt, kbm, vbmt, bias, lam_params, subln_col, running_max))

    amax = lambda g: jnp.max(jnp.abs(g[layer].astype(F32)))
    score_bound = HEAD_DIM * Q_PRESCALE * jnp.maximum(amax(a_q_norm_g) * amax(a_k_norm_g),
                                                      amax(b_q_norm_g) * amax(b_k_norm_g))
    out_a, out_b = lax.cond(score_bound <= UNSHIFTED_SCORE_LIMIT,
                            lambda: attend(False), lambda: attend(True))

    w_out_b = w_out[layer].astype(BF16)
    y = _post(x.reshape(bsz * seq, d_model), out_a.reshape(bsz * seq, A_WIDTH),
              out_b.reshape(bsz * seq, B_WIDTH), w_out_b[:A_WIDTH], w_out_b[A_WIDTH:],
              ffn_norm_g[layer].astype(F32)[None, :],
              w_gate[layer].astype(BF16), w_up[layer].astype(BF16), w_down[layer].astype(BF16))
    return y.reshape(bsz, seq, d_model)
```

```python
import functools
import math

import jax
import jax.numpy as jnp
from jax import lax
from jax.experimental import pallas as pl
from jax.experimental.pallas import tpu as pltpu

N_META = 16
GRID_W = 64
HEAD_DIM = 64
ROPE_THETA = 10000.0
EPS = 1e-6
A_HEADS = 8
A_KV_HEADS = 2
A_GROUP = A_HEADS // A_KV_HEADS
A_WIDTH = A_HEADS * HEAD_DIM
A_KV_WIDTH = A_KV_HEADS * HEAD_DIM
B_HEADS = 4
B_VDIM = 2 * HEAD_DIM
B_WIDTH = B_HEADS * B_VDIM
LAM_INIT = 0.8 - 0.6 * math.exp(-0.3 * 0)

LANES = 128
VMEM_LIMIT_BYTES = 56 * 1024 * 1024

Q_PRESCALE = (HEAD_DIM ** -0.5) * math.log2(math.e)
MASK_BIAS = -1e30
UNSHIFTED_SCORE_LIMIT = 60.0

PROJ_ROWS = 1024
PROJ_ROW_BLOCK = 256
PROJ_GROUP_COLS = 4
META_ROWS = LANES
MXU_TILE = 256
Q_TILES_PER_STEP = 2
A_Q_COLS = MXU_TILE
A_KV_CHUNK = MXU_TILE
A_LOOKAHEAD = 8
B_Q_COLS = 512
B_HEADS_PER_STEP = 2
B_KV_CHUNK = MXU_TILE
B_LOOKAHEAD = 8
KV_CHUNK = 512
A_STEP_COLS = A_Q_COLS * Q_TILES_PER_STEP
B_STEP_COLS = B_Q_COLS * Q_TILES_PER_STEP
POST_ROWS = 512
POST_ROW_BLOCK = 256

BF16 = jnp.bfloat16
F32 = jnp.float32


def _dot(a, b):
    return jnp.dot(a, b, preferred_element_type=F32)


def _lane_index(shape):
    return lax.broadcasted_iota(jnp.int32, shape, len(shape) - 1)


def _row_index(shape):
    return lax.broadcasted_iota(jnp.int32, shape, 0)


def _proj_kernel(x_ref, g_ref, w_ref, cos_a_ref, sin_a_ref, cos_b_ref, sin_b_ref,
                 gains_ref, seg_ref,
                 qa_ref, ka_ref, va_ref, qb_ref, kb_ref, vb_ref):
    rows = x_ref.shape[0]
    blk = min(rows, PROJ_ROW_BLOCK)
    lane = _lane_index((blk, LANES))
    even = (lane % 2) == 0
    low_half = lane < HEAD_DIM
    seg = seg_ref[...]
    g_aq = gains_ref[0:1, :] * Q_PRESCALE
    g_ak = gains_ref[1:2, :]
    g_bq = gains_ref[2:3, :] * Q_PRESCALE
    g_bk = gains_ref[3:4, :]

    def norm_rope(col, gain, cos_ref, sin_ref, rb):
        ss = _dot((col * col).astype(BF16), seg)
        y = col * lax.rsqrt(ss * (1.0 / HEAD_DIM) + EPS) * gain
        swapped = jnp.where(even, pltpu.roll(y, LANES - 1, 1), pltpu.roll(y, 1, 1))
        return y * cos_ref[rb, :] + swapped * sin_ref[rb, :]

    def emit_qa(j, col, rb):
        col = norm_rope(col, g_aq, cos_a_ref, sin_a_ref, rb)
        flipped = pltpu.roll(col, HEAD_DIM, 1)
        for half in range(2):
            head = 2 * j + half
            kv = head // A_GROUP
            src = col if half == kv else flipped
            keep = low_half if kv == 0 else jnp.logical_not(low_half)
            qa_ref[head, :, rb] = jnp.where(keep, src, 0.0).T.astype(BF16)

    def emit_ka(j, col, rb):
        ka_ref[rb, :] = norm_rope(col, g_ak, cos_a_ref, sin_a_ref, rb).astype(BF16)

    def emit_va(j, col, rb):
        va_ref[:, rb] = col.T.astype(BF16)

    def emit_qb(j, col, rb):
        qb_ref[j, :, rb] = norm_rope(col, g_bq, cos_b_ref, sin_b_ref, rb).T.astype(BF16)

    def emit_kb(j, col, rb):
        kb_ref[rb, j * LANES:(j + 1) * LANES] = norm_rope(col, g_bk, cos_b_ref, sin_b_ref, rb).astype(BF16)

    def emit_vb(j, col, rb):
        vb_ref[j * LANES:(j + 1) * LANES, rb] = col.T.astype(BF16)

    handlers = ([(emit_qa, j) for j in range(A_WIDTH // LANES)] + [(emit_ka, 0), (emit_va, 0)]
                + [(emit_qb, j) for j in range(B_HEADS)] + [(emit_kb, j) for j in range(B_HEADS)]
                + [(emit_vb, j) for j in range(B_HEADS)])
    assert len(handlers) * LANES == w_ref.shape[1]
    groups = [range(s, min(s + PROJ_GROUP_COLS, len(handlers)))
              for s in range(0, len(handlers), PROJ_GROUP_COLS)]

    row_blocks = [slice(r, r + blk) for r in range(0, rows, blk)]

    def normed(rb):
        x = x_ref[rb, :]
        ms = jnp.mean(x * x, axis=-1, keepdims=True)
        return (x * lax.rsqrt(ms + EPS) * g_ref[...]).astype(BF16)

    hs = [normed(rb) for rb in row_blocks]
    items = [(r, cols) for cols in groups for r in range(len(row_blocks))]

    def project(item):
        r, cols = item
        return _dot(hs[r], w_ref[:, cols.start * LANES:cols.stop * LANES])

    slab_next = project(items[0])
    for pos, (r, cols) in enumerate(items):
        slab = slab_next
        if pos + 1 < len(items):
            slab_next = project(items[pos + 1])
        for n, c in enumerate(cols):
            fn, j = handlers[c]
            fn(j, slab[:, n * LANES:(n + 1) * LANES], row_blocks[r])


def _project(x3, g, w_bf16, tables, gains, seg, rows_per_step):
    bsz, n_rows, d_model = x3.shape
    in_width = w_bf16.shape[1]
    t = rows_per_step
    tab_blk = pl.BlockSpec((t, LANES), lambda b, i: (i, 0))
    const = lambda shape: pl.BlockSpec(shape, lambda b, i: (0, 0))
    out_specs = [
        pl.BlockSpec((None, A_HEADS, LANES, t), lambda b, i: (b, 0, 0, i)),
        pl.BlockSpec((None, t, LANES), lambda b, i: (b, i, 0)),
        pl.BlockSpec((None, LANES, t), lambda b, i: (b, 0, i)),
        pl.BlockSpec((None, B_HEADS, LANES, t), lambda b, i: (b, 0, 0, i)),
        pl.BlockSpec((None, t, B_WIDTH), lambda b, i: (b, i, 0)),
        pl.BlockSpec((None, B_WIDTH, t), lambda b, i: (b, 0, i)),
    ]
    out_shapes = [(bsz, A_HEADS, LANES, n_rows), (bsz, n_rows, LANES), (bsz, LANES, n_rows),
                  (bsz, B_HEADS, LANES, n_rows), (bsz, n_rows, B_WIDTH), (bsz, B_WIDTH, n_rows)]
    return pl.pallas_call(
        _proj_kernel,
        grid=(bsz, n_rows // t),
        in_specs=[pl.BlockSpec((None, t, d_model), lambda b, i: (b, i, 0)),
                  const((1, d_model)), const((d_model, in_width)),
                  tab_blk, tab_blk, tab_blk, tab_blk, const(gains.shape), const(seg.shape)],
        out_specs=out_specs,
        out_shape=[jax.ShapeDtypeStruct(s, BF16) for s in out_shapes],
        compiler_params=pltpu.CompilerParams(
            dimension_semantics=("arbitrary", "arbitrary"), vmem_limit_bytes=VMEM_LIMIT_BYTES),
        name="proj",
    )(x3, g, w_bf16, *tables, gains, seg)


def _attend_t(groups, bias_ref, running_max, finish, lookahead=1, kv_chunk=KV_CHUNK):
    streams = [s for group in groups for s in group]
    n_chunks = streams[0][1].shape[0] // kv_chunk
    items, group_end, base = [], {}, 0
    for g, group in enumerate(groups):
        ids = range(base, base + len(group))
        items += [(i, None) for i in ids] + [(i, c) for c in range(n_chunks) for i in ids]
        group_end[len(items) - 1] = (g, ids)
        base += len(group)

    def scores(item):
        i, c = item
        qt, k_ref, _, km, _ = streams[i]
        if c is None:
            bias = jnp.concatenate([bias_ref[...]] * (qt.shape[1] // LANES), axis=1)
            return _dot(km, qt) + bias
        return _dot(k_ref[c * kv_chunk:(c + 1) * kv_chunk, :], qt)

    def update(i, c, s, state):
        _, _, vt_ref, _, vmt = streams[i]
        vt = vmt if c is None else vt_ref[:, c * kv_chunk:(c + 1) * kv_chunk]
        if not running_max:
            p = jnp.exp2(s)
            l, acc = jnp.sum(p, axis=0, keepdims=True), _dot(vt, p.astype(BF16))
            if state is not None:
                l, acc = state[1] + l, state[2] + acc
            return (None, l, acc)
        s_max = jnp.max(s, axis=0, keepdims=True)
        if state is None:
            p = jnp.exp2(s - s_max)
            return (s_max, jnp.sum(p, axis=0, keepdims=True), _dot(vt, p.astype(BF16)))
        m, l, acc = state
        m_new = jnp.maximum(m, s_max)
        alpha = jnp.exp2(m - m_new)
        p = jnp.exp2(s - m_new)
        return (m_new, alpha * l + jnp.sum(p, axis=0, keepdims=True),
                alpha * acc + _dot(vt, p.astype(BF16)))

    state = [None] * len(streams)
    pending = [scores(item) for item in items[:lookahead]]
    for pos, (i, c) in enumerate(items):
        s = pending.pop(0)
        if pos + lookahead < len(items):
            pending.append(scores(items[pos + lookahead]))
        state[i] = update(i, c, s, state[i])
        if pos in group_end:
            g, ids = group_end[pos]
            finish(g, [state[j][2] / state[j][1] for j in ids])


def _attn_a_kernel(q_ref, k_ref, vt_ref, km_ref, vmt_ref, bias_ref, o_ref, *, running_max):
    tq = A_Q_COLS
    km, vmt = km_ref[...], vmt_ref[...]
    groups = []
    for t in range(q_ref.shape[2] // tq):
        group = []
        for head in range(A_HEADS):
            rows = slice((head // A_GROUP) * HEAD_DIM, (head // A_GROUP + 1) * HEAD_DIM)
            group.append((q_ref[head, :, t * tq:(t + 1) * tq], k_ref, vt_ref.at[rows, :], km, vmt[rows, :]))
        groups.append(group)

    def finish(t, outs):
        o_ref[t * tq:(t + 1) * tq, :] = jnp.concatenate(outs, axis=0).T.astype(o_ref.dtype)

    _attend_t(groups, bias_ref, running_max, finish, lookahead=A_LOOKAHEAD, kv_chunk=A_KV_CHUNK)


def _attn_b_kernel(q_ref, k_ref, vt_ref, km_ref, vmt_ref, bias_ref, lam_ref, subln_ref, o_ref, *,
                   running_max):
    tq = B_Q_COLS
    parts = tq // MXU_TILE
    first = _row_index((LANES, MXU_TILE)) < HEAD_DIM
    zero = jnp.zeros((LANES, MXU_TILE), BF16)
    groups = []
    for t in range(q_ref.shape[2] // tq):
        group = []
        for h in range(B_HEADS_PER_STEP):
            cols = slice(h * LANES, (h + 1) * LANES)
            kv = (k_ref.at[:, cols], vt_ref.at[cols, :], km_ref[:, cols], vmt_ref[cols, :])
            for sub in range(2):
                for part in range(parts):
                    start = t * tq + part * MXU_TILE
                    q = q_ref[h, :, start:start + MXU_TILE]
                    qt = jnp.where(first, q, zero) if sub == 0 else jnp.where(first, zero, q)
                    group.append((qt,) + kv)
        groups.append(group)
    lam_p = lam_ref[...]
    lam = (jnp.exp(jnp.sum(lam_p[0:1] * lam_p[1:2], axis=-1, keepdims=True))
           - jnp.exp(jnp.sum(lam_p[2:3] * lam_p[3:4], axis=-1, keepdims=True)) + LAM_INIT)

    def finish(t, outs):
        ys = []
        for h in range(B_HEADS_PER_STEP):
            o1 = jnp.concatenate(outs[(2 * h) * parts:(2 * h + 1) * parts], axis=1)
            o2 = jnp.concatenate(outs[(2 * h + 1) * parts:(2 * h + 2) * parts], axis=1)
            d = o1 - lam * o2
            ms = jnp.mean(d * d, axis=0, keepdims=True)
            ys.append(d * lax.rsqrt(ms + EPS) * subln_ref[...] * (1.0 - LAM_INIT))
        o_ref[t * tq:(t + 1) * tq, :] = jnp.concatenate(ys, axis=0).T.astype(o_ref.dtype)

    _attend_t(groups, bias_ref, running_max, finish, lookahead=B_LOOKAHEAD, kv_chunk=B_KV_CHUNK)


def _attention_a(qat, ka, vat, kam, vamt, bias, running_max):
    bsz, _, _, seq = qat.shape
    const = lambda shape: pl.BlockSpec(shape, lambda b, i: (0,) * len(shape))
    return pl.pallas_call(
        functools.partial(_attn_a_kernel, running_max=running_max),
        grid=(bsz, seq // A_STEP_COLS),
        in_specs=[pl.BlockSpec((None, A_HEADS, LANES, A_STEP_COLS), lambda b, i: (b, 0, 0, i)),
                  pl.BlockSpec((None, seq, LANES), lambda b, i: (b, 0, 0)),
                  pl.BlockSpec((None, LANES, seq), lambda b, i: (b, 0, 0)),
                  const(kam.shape), const(vamt.shape), const(bias.shape)],
        out_specs=pl.BlockSpec((None, A_STEP_COLS, A_WIDTH), lambda b, i: (b, i, 0)),
        out_shape=jax.ShapeDtypeStruct((bsz, seq, A_WIDTH), BF16),
        compiler_params=pltpu.CompilerParams(
            dimension_semantics=("arbitrary", "arbitrary"), vmem_limit_bytes=VMEM_LIMIT_BYTES),
        name="attn_a",
    )(qat, ka, vat, kam, vamt, bias)


def _attention_b(qbt, kb, vbt, kbm, vbmt, bias, lam_params, subln_col, running_max):
    bsz, _, _, seq = qbt.shape
    hp = B_HEADS_PER_STEP
    width = hp * LANES
    const = lambda shape: pl.BlockSpec(shape, lambda b, h, i: (0,) * len(shape))
    return pl.pallas_call(
        functools.partial(_attn_b_kernel, running_max=running_max),
        grid=(bsz, B_HEADS // hp, seq // B_STEP_COLS),
        in_specs=[pl.BlockSpec((None, hp, LANES, B_STEP_COLS), lambda b, h, i: (b, h, 0, i)),
                  pl.BlockSpec((None, seq, width), lambda b, h, i: (b, 0, h)),
                  pl.BlockSpec((None, width, seq), lambda b, h, i: (b, h, 0)),
                  pl.BlockSpec((LANES, width), lambda b, h, i: (0, h)),
                  pl.BlockSpec((width, LANES), lambda b, h, i: (h, 0)),
                  const(bias.shape), const(lam_params.shape), const(subln_col.shape)],
        out_specs=pl.BlockSpec((None, B_STEP_COLS, width), lambda b, h, i: (b, i, h)),
        out_shape=jax.ShapeDtypeStruct((bsz, seq, B_WIDTH), BF16),
        compiler_params=pltpu.CompilerParams(
            dimension_semantics=("arbitrary", "arbitrary", "arbitrary"),
            vmem_limit_bytes=VMEM_LIMIT_BYTES),
        name="attn_b",
    )(qbt, kb, vbt, kbm, vbmt, bias, lam_params, subln_col)


def _post_kernel(x_ref, oa_ref, ob_ref, woa_ref, wob_ref, g_ref, wg_ref, wu_ref, wd_ref, y_ref):
    blocks = [slice(r, r + POST_ROW_BLOCK) for r in range(0, x_ref.shape[0], POST_ROW_BLOCK)]
    x1 = [x_ref[rb, :] + _dot(oa_ref[rb, :], woa_ref[...]) + _dot(ob_ref[rb, :], wob_ref[...])
          for rb in blocks]

    def normed(v):
        ms = jnp.mean(v * v, axis=-1, keepdims=True)
        return (v * lax.rsqrt(ms + EPS) * g_ref[...]).astype(BF16)

    gate_up = []
    for v in x1:
        h = normed(v)
        gate_up.append((_dot(h, wg_ref[...]), _dot(h, wu_ref[...])))
    for rb, v, (gate, up) in zip(blocks, x1, gate_up):
        act = (gate * jax.nn.sigmoid(gate) * up).astype(BF16)
        y_ref[rb, :] = v + _dot(act, wd_ref[...])


def _post(x_rows, oa, ob, wo_a, wo_b, g, wg, wu, wd):
    n_rows, d_model = x_rows.shape
    row_blk = lambda width: pl.BlockSpec((POST_ROWS, width), lambda i: (i, 0))
    resident = lambda a: pl.BlockSpec(a.shape, lambda i: (0, 0), pipeline_mode=pl.Buffered(1))
    return pl.pallas_call(
        _post_kernel,
        grid=(n_rows // POST_ROWS,),
        in_specs=[row_blk(d_model), row_blk(A_WIDTH), row_blk(B_WIDTH),
                  resident(wo_a), resident(wo_b), resident(g),
                  resident(wg), resident(wu), resident(wd)],
        out_specs=row_blk(d_model),
        out_shape=jax.ShapeDtypeStruct((n_rows, d_model), F32),
        compiler_params=pltpu.CompilerParams(
            dimension_semantics=("arbitrary",), vmem_limit_bytes=VMEM_LIMIT_BYTES),
        name="post",
    )(x_rows, oa, ob, wo_a, wo_b, g, wg, wu, wd)


def _pair_tables(ang):
    c, s = jnp.cos(ang), jnp.sin(ang)
    cos64 = jnp.repeat(c, 2, axis=-1)
    sin64 = jnp.stack([-s, s], axis=-1).reshape(ang.shape[0], HEAD_DIM)
    return jnp.tile(cos64, (1, 2)), jnp.tile(sin64, (1, 2))


def _rope_tables(seq):
    half = HEAD_DIM // 2
    t = jnp.arange(seq)
    inv_a = ROPE_THETA ** (-jnp.arange(0, half, 2, dtype=F32) / half)
    r = (t // GRID_W).astype(F32)
    c = (t % GRID_W).astype(F32)
    ang_a = jnp.concatenate([r[:, None] * inv_a[None, :], c[:, None] * inv_a[None, :]], axis=-1)
    inv_b = ROPE_THETA ** (-jnp.arange(0, HEAD_DIM, 2, dtype=F32) / HEAD_DIM)
    pos = jnp.arange(N_META + seq, dtype=F32)
    ang_b = pos[:, None] * inv_b[None, :]
    seq_tabs = _pair_tables(ang_a) + _pair_tables(ang_b[N_META:])
    meta_tabs = _pair_tables(jnp.zeros((N_META, half), F32)) + _pair_tables(ang_b[:N_META])
    return seq_tabs, tuple(_pad_rows(t, META_ROWS) for t in meta_tabs)


def _pad_rows(a, rows):
    return jnp.pad(a, ((0, rows - a.shape[0]), (0, 0)))


def kernel(x, meta_tokens, attn_norm_g, w_in, a_q_norm_g, a_k_norm_g, b_q_norm_g, b_k_norm_g,
           b_lambda_q1, b_lambda_k1, b_lambda_q2, b_lambda_k2, b_subln_g, w_out,
           ffn_norm_g, w_gate, w_up, w_down):
    bsz, seq, d_model = x.shape
    assert w_in.shape[0] == 1, "single-layer block"
    assert seq % PROJ_ROWS == 0 and seq % KV_CHUNK == 0 and seq % A_STEP_COLS == 0 and seq % B_STEP_COLS == 0
    layer = 0

    seq_tabs, meta_tabs = _rope_tables(seq)
    gains = jnp.stack([jnp.tile(g[layer].astype(F32), 2)
                       for g in (a_q_norm_g, a_k_norm_g, b_q_norm_g, b_k_norm_g)])
    idx = jnp.arange(LANES) // HEAD_DIM
    seg = (idx[:, None] == idx[None, :]).astype(BF16)
    w_in_b = w_in[layer].astype(BF16)
    g_attn = attn_norm_g[layer].astype(F32)[None, :]

    qat, ka, vat, qbt, kb, vbt = _project(x, g_attn, w_in_b, seq_tabs, gains, seg, PROJ_ROWS)
    meta3 = _pad_rows(meta_tokens.astype(F32), META_ROWS)[None]
    _, kam, vamt, _, kbm, vbmt = _project(meta3, g_attn, w_in_b, meta_tabs, gains, seg, META_ROWS)
    kam, vamt, kbm, vbmt = kam[0], vamt[0], kbm[0], vbmt[0]
    bias = jnp.where(jnp.arange(META_ROWS) < N_META, 0.0, MASK_BIAS).astype(F32)
    bias = jnp.broadcast_to(bias[:, None], (META_ROWS, LANES))

    lam_params = jnp.stack([jnp.pad(p[layer].astype(F32), (0, LANES - HEAD_DIM))
                            for p in (b_lambda_q1, b_lambda_k1, b_lambda_q2, b_lambda_k2)])
    subln_col = b_subln_g[layer].astype(F32)[:, None]

    def attend(running_max):
        return (_attention_a(qat, ka, vat, kam, vamt, bias, running_max),
                _attention_b(qbt, kb, vbt, kbm, vbmt, bias, lam_params, subln_col, running_max))

    amax = lambda g: jnp.max(jnp.abs(g[layer].astype(F32)))
    score_bound = HEAD_DIM * Q_PRESCALE * jnp.maximum(amax(a_q_norm_g) * amax(a_k_norm_g),
                                                      amax(b_q_norm_g) * amax(b_k_norm_g))
    out_a, out_b = lax.cond(score_bound <= UNSHIFTED_SCORE_LIMIT,
                            lambda: attend(False), lambda: attend(True))

    w_out_b = w_out[layer].astype(BF16)
    y = _post(x.reshape(bsz * seq, d_model), out_a.reshape(bsz * seq, A_WIDTH),
              out_b.reshape(bsz * seq, B_WIDTH), w_out_b[:A_WIDTH], w_out_b[A_WIDTH:],
              ffn_norm_g[layer].astype(F32)[None, :],
              w_gate[layer].astype(BF16), w_up[layer].astype(BF16), w_down[layer].astype(BF16))
    return y.reshape(bsz, seq, d_model)
```

```python
import functools
import math

import jax
import jax.numpy as jnp
from jax import lax
from jax.experimental import pallas as pl
from jax.experimental.pallas import tpu as pltpu

N_META = 16
GRID_W = 64
HEAD_DIM = 64
ROPE_THETA = 10000.0
EPS = 1e-6
A_HEADS = 8
A_KV_HEADS = 2
A_GROUP = A_HEADS // A_KV_HEADS
A_WIDTH = A_HEADS * HEAD_DIM
A_KV_WIDTH = A_KV_HEADS * HEAD_DIM
B_HEADS = 4
B_VDIM = 2 * HEAD_DIM
B_WIDTH = B_HEADS * B_VDIM
LAM_INIT = 0.8 - 0.6 * math.exp(-0.3 * 0)

LANES = 128
VMEM_LIMIT_BYTES = 56 * 1024 * 1024

Q_PRESCALE = (HEAD_DIM ** -0.5) * math.log2(math.e)
MASK_BIAS = -1e30
UNSHIFTED_SCORE_LIMIT = 60.0

PROJ_ROWS = 1024
PROJ_ROW_BLOCK = 256
PROJ_GROUP_COLS = 4
META_ROWS = LANES
MXU_TILE = 256
Q_TILES_PER_STEP = 2
A_Q_COLS = MXU_TILE
A_KV_CHUNK = MXU_TILE
A_LOOKAHEAD = 8
B_Q_COLS = 512
B_HEADS_PER_STEP = 2
B_KV_CHUNK = 512
B_LOOKAHEAD = 1
KV_CHUNK = 512
A_STEP_COLS = A_Q_COLS * Q_TILES_PER_STEP
B_STEP_COLS = B_Q_COLS * Q_TILES_PER_STEP
POST_ROWS = 512
POST_ROW_BLOCK = 256

BF16 = jnp.bfloat16
F32 = jnp.float32


def _dot(a, b):
    return jnp.dot(a, b, preferred_element_type=F32)


def _lane_index(shape):
    return lax.broadcasted_iota(jnp.int32, shape, len(shape) - 1)


def _row_index(shape):
    return lax.broadcasted_iota(jnp.int32, shape, 0)


def _proj_kernel(x_ref, g_ref, w_ref, cos_a_ref, sin_a_ref, cos_b_ref, sin_b_ref,
                 gains_ref, seg_ref,
                 qa_ref, ka_ref, va_ref, qb_ref, kb_ref, vb_ref):
    rows = x_ref.shape[0]
    blk = min(rows, PROJ_ROW_BLOCK)
    lane = _lane_index((blk, LANES))
    even = (lane % 2) == 0
    low_half = lane < HEAD_DIM
    seg = seg_ref[...]
    g_aq = gains_ref[0:1, :] * Q_PRESCALE
    g_ak = gains_ref[1:2, :]
    g_bq = gains_ref[2:3, :] * Q_PRESCALE
    g_bk = gains_ref[3:4, :]

    def norm_rope(col, gain, cos_ref, sin_ref, rb):
        ss = _dot((col * col).astype(BF16), seg)
        y = col * lax.rsqrt(ss * (1.0 / HEAD_DIM) + EPS) * gain
        swapped = jnp.where(even, pltpu.roll(y, LANES - 1, 1), pltpu.roll(y, 1, 1))
        return y * cos_ref[rb, :] + swapped * sin_ref[rb, :]

    def emit_qa(j, col, rb):
        col = norm_rope(col, g_aq, cos_a_ref, sin_a_ref, rb)
        flipped = pltpu.roll(col, HEAD_DIM, 1)
        for half in range(2):
            head = 2 * j + half
            kv = head // A_GROUP
            src = col if half == kv else flipped
            keep = low_half if kv == 0 else jnp.logical_not(low_half)
            qa_ref[head, :, rb] = jnp.where(keep, src, 0.0).T.astype(BF16)

    def emit_ka(j, col, rb):
        ka_ref[rb, :] = norm_rope(col, g_ak, cos_a_ref, sin_a_ref, rb).astype(BF16)

    def emit_va(j, col, rb):
        va_ref[:, rb] = col.T.astype(BF16)

    def emit_qb(j, col, rb):
        qb_ref[j, :, rb] = norm_rope(col, g_bq, cos_b_ref, sin_b_ref, rb).T.astype(BF16)

    def emit_kb(j, col, rb):
        kb_ref[rb, j * LANES:(j + 1) * LANES] = norm_rope(col, g_bk, cos_b_ref, sin_b_ref, rb).astype(BF16)

    def emit_vb(j, col, rb):
        vb_ref[j * LANES:(j + 1) * LANES, rb] = col.T.astype(BF16)

    handlers = ([(emit_qa, j) for j in range(A_WIDTH // LANES)] + [(emit_ka, 0), (emit_va, 0)]
                + [(emit_qb, j) for j in range(B_HEADS)] + [(emit_kb, j) for j in range(B_HEADS)]
                + [(emit_vb, j) for j in range(B_HEADS)])
    assert len(handlers) * LANES == w_ref.shape[1]
    groups = [range(s, min(s + PROJ_GROUP_COLS, len(handlers)))
              for s in range(0, len(handlers), PROJ_GROUP_COLS)]

    row_blocks = [slice(r, r + blk) for r in range(0, rows, blk)]

    def normed(rb):
        x = x_ref[rb, :]
        ms = jnp.mean(x * x, axis=-1, keepdims=True)
        return (x * lax.rsqrt(ms + EPS) * g_ref[...]).astype(BF16)

    hs = [normed(rb) for rb in row_blocks]
    items = [(r, cols) for cols in groups for r in range(len(row_blocks))]

    def project(item):
        r, cols = item
        return _dot(hs[r], w_ref[:, cols.start * LANES:cols.stop * LANES])

    slab_next = project(items[0])
    for pos, (r, cols) in enumerate(items):
        slab = slab_next
        if pos + 1 < len(items):
            slab_next = project(items[pos + 1])
        for n, c in enumerate(cols):
            fn, j = handlers[c]
            fn(j, slab[:, n * LANES:(n + 1) * LANES], row_blocks[r])


def _project(x3, g, w_bf16, tables, gains, seg, rows_per_step):
    bsz, n_rows, d_model = x3.shape
    in_width = w_bf16.shape[1]
    t = rows_per_step
    tab_blk = pl.BlockSpec((t, LANES), lambda b, i: (i, 0))
    const = lambda shape: pl.BlockSpec(shape, lambda b, i: (0, 0))
    out_specs = [
        pl.BlockSpec((None, A_HEADS, LANES, t), lambda b, i: (b, 0, 0, i)),
        pl.BlockSpec((None, t, LANES), lambda b, i: (b, i, 0)),
        pl.BlockSpec((None, LANES, t), lambda b, i: (b, 0, i)),
        pl.BlockSpec((None, B_HEADS, LANES, t), lambda b, i: (b, 0, 0, i)),
        pl.BlockSpec((None, t, B_WIDTH), lambda b, i: (b, i, 0)),
        pl.BlockSpec((None, B_WIDTH, t), lambda b, i: (b, 0, i)),
    ]
    out_shapes = [(bsz, A_HEADS, LANES, n_rows), (bsz, n_rows, LANES), (bsz, LANES, n_rows),
                  (bsz, B_HEADS, LANES, n_rows), (bsz, n_rows, B_WIDTH), (bsz, B_WIDTH, n_rows)]
    return pl.pallas_call(
        _proj_kernel,
        grid=(bsz, n_rows // t),
        in_specs=[pl.BlockSpec((None, t, d_model), lambda b, i: (b, i, 0)),
                  const((1, d_model)), const((d_model, in_width)),
                  tab_blk, tab_blk, tab_blk, tab_blk, const(gains.shape), const(seg.shape)],
        out_specs=out_specs,
        out_shape=[jax.ShapeDtypeStruct(s, BF16) for s in out_shapes],
        compiler_params=pltpu.CompilerParams(
            dimension_semantics=("arbitrary", "arbitrary"), vmem_limit_bytes=VMEM_LIMIT_BYTES),
        name="proj",
    )(x3, g, w_bf16, *tables, gains, seg)


def _attend_t(groups, bias_ref, running_max, finish, lookahead=1, kv_chunk=KV_CHUNK):
    streams = [s for group in groups for s in group]
    n_chunks = streams[0][1].shape[0] // kv_chunk
    items, group_end, base = [], {}, 0
    for g, group in enumerate(groups):
        ids = range(base, base + len(group))
        items += [(i, None) for i in ids] + [(i, c) for c in range(n_chunks) for i in ids]
        group_end[len(items) - 1] = (g, ids)
        base += len(group)

    def scores(item):
        i, c = item
        qt, k_ref, _, km, _ = streams[i]
        if c is None:
            bias = jnp.concatenate([bias_ref[...]] * (qt.shape[1] // LANES), axis=1)
            return _dot(km, qt) + bias
        return _dot(k_ref[c * kv_chunk:(c + 1) * kv_chunk, :], qt)

    def update(i, c, s, state):
        _, _, vt_ref, _, vmt = streams[i]
        vt = vmt if c is None else vt_ref[:, c * kv_chunk:(c + 1) * kv_chunk]
        if not running_max:
            p = jnp.exp2(s)
            l, acc = jnp.sum(p, axis=0, keepdims=True), _dot(vt, p.astype(BF16))
            if state is not None:
                l, acc = state[1] + l, state[2] + acc
            return (None, l, acc)
        s_max = jnp.max(s, axis=0, keepdims=True)
        if state is None:
            p = jnp.exp2(s - s_max)
            return (s_max, jnp.sum(p, axis=0, keepdims=True), _dot(vt, p.astype(BF16)))
        m, l, acc = state
        m_new = jnp.maximum(m, s_max)
        alpha = jnp.exp2(m - m_new)
        p = jnp.exp2(s - m_new)
        return (m_new, alpha * l + jnp.sum(p, axis=0, keepdims=True),
                alpha * acc + _dot(vt, p.astype(BF16)))

    state = [None] * len(streams)
    pending = [scores(item) for item in items[:lookahead]]
    for pos, (i, c) in enumerate(items):
        s = pending.pop(0)
        if pos + lookahead < len(items):
            pending.append(scores(items[pos + lookahead]))
        state[i] = update(i, c, s, state[i])
        if pos in group_end:
            g, ids = group_end[pos]
            finish(g, [state[j][2] / state[j][1] for j in ids])


def _attn_a_kernel(q_ref, k_ref, vt_ref, km_ref, vmt_ref, bias_ref, o_ref, *, running_max):
    tq = A_Q_COLS
    km, vmt = km_ref[...], vmt_ref[...]
    groups = []
    for t in range(q_ref.shape[2] // tq):
        group = []
        for head in range(A_HEADS):
            rows = slice((head // A_GROUP) * HEAD_DIM, (head // A_GROUP + 1) * HEAD_DIM)
            group.append((q_ref[head, :, t * tq:(t + 1) * tq], k_ref, vt_ref.at[rows, :], km, vmt[rows, :]))
        groups.append(group)

    def finish(t, outs):
        o_ref[t * tq:(t + 1) * tq, :] = jnp.concatenate(outs, axis=0).T.astype(o_ref.dtype)

    _attend_t(groups, bias_ref, running_max, finish, lookahead=A_LOOKAHEAD, kv_chunk=A_KV_CHUNK)


def _attn_b_kernel(q_ref, *refs, running_max):
    k_refs = refs[:B_HEADS_PER_STEP]
    vt_ref, km_ref, vmt_ref, bias_ref, lam_ref, subln_ref, o_ref = refs[B_HEADS_PER_STEP:]
    tq = B_Q_COLS
    first = _row_index((LANES, tq)) < HEAD_DIM
    zero = jnp.zeros((LANES, tq), BF16)
    groups = []
    for t in range(q_ref.shape[2] // tq):
        group = []
        for h in range(B_HEADS_PER_STEP):
            q = q_ref[h, :, t * tq:(t + 1) * tq]
            cols = slice(h * LANES, (h + 1) * LANES)
            qt = jnp.concatenate([jnp.where(first, q, zero), jnp.where(first, zero, q)], axis=1)
            group.append((qt, k_refs[h], vt_ref.at[cols, :], km_ref[:, cols], vmt_ref[cols, :]))
        groups.append(group)
    lam_p = lam_ref[...]
    lam = (jnp.exp(jnp.sum(lam_p[0:1] * lam_p[1:2], axis=-1, keepdims=True))
           - jnp.exp(jnp.sum(lam_p[2:3] * lam_p[3:4], axis=-1, keepdims=True)) + LAM_INIT)

    def finish(t, outs):
        ys = []
        for o in outs:
            d = o[:, :tq] - lam * o[:, tq:]
            ms = jnp.mean(d * d, axis=0, keepdims=True)
            ys.append(d * lax.rsqrt(ms + EPS) * subln_ref[...] * (1.0 - LAM_INIT))
        o_ref[t * tq:(t + 1) * tq, :] = jnp.concatenate(ys, axis=0).T.astype(o_ref.dtype)

    _attend_t(groups, bias_ref, running_max, finish, lookahead=B_LOOKAHEAD, kv_chunk=B_KV_CHUNK)


def _attention_a(qat, ka, vat, kam, vamt, bias, running_max):
    bsz, _, _, seq = qat.shape
    const = lambda shape: pl.BlockSpec(shape, lambda b, i: (0,) * len(shape))
    return pl.pallas_call(
        functools.partial(_attn_a_kernel, running_max=running_max),
        grid=(bsz, seq // A_STEP_COLS),
        in_specs=[pl.BlockSpec((None, A_HEADS, LANES, A_STEP_COLS), lambda b, i: (b, 0, 0, i)),
                  pl.BlockSpec((None, seq, LANES), lambda b, i: (b, 0, 0)),
                  pl.BlockSpec((None, LANES, seq), lambda b, i: (b, 0, 0)),
                  const(kam.shape), const(vamt.shape), const(bias.shape)],
        out_specs=pl.BlockSpec((None, A_STEP_COLS, A_WIDTH), lambda b, i: (b, i, 0)),
        out_shape=jax.ShapeDtypeStruct((bsz, seq, A_WIDTH), BF16),
        compiler_params=pltpu.CompilerParams(
            dimension_semantics=("arbitrary", "arbitrary"), vmem_limit_bytes=VMEM_LIMIT_BYTES),
        name="attn_a",
    )(qat, ka, vat, kam, vamt, bias)


def _attention_b(qbt, kb, vbt, kbm, vbmt, bias, lam_params, subln_col, running_max):
    bsz, _, _, seq = qbt.shape
    hp = B_HEADS_PER_STEP
    width = hp * LANES
    const = lambda shape: pl.BlockSpec(shape, lambda b, h, i: (0,) * len(shape))
    return pl.pallas_call(
        functools.partial(_attn_b_kernel, running_max=running_max),
        grid=(bsz, B_HEADS // hp, seq // B_STEP_COLS),
        in_specs=[pl.BlockSpec((None, hp, LANES, B_STEP_COLS), lambda b, h, i: (b, h, 0, i))]
                 + [pl.BlockSpec((None, seq, LANES), lambda b, h, i, j=j: (b, 0, hp * h + j))
                    for j in range(hp)]
                 + [pl.BlockSpec((None, width, seq), lambda b, h, i: (b, h, 0)),
                  pl.BlockSpec((LANES, width), lambda b, h, i: (0, h)),
                  pl.BlockSpec((width, LANES), lambda b, h, i: (h, 0)),
                  const(bias.shape), const(lam_params.shape), const(subln_col.shape)],
        out_specs=pl.BlockSpec((None, B_STEP_COLS, width), lambda b, h, i: (b, i, h)),
        out_shape=jax.ShapeDtypeStruct((bsz, seq, B_WIDTH), BF16),
        compiler_params=pltpu.CompilerParams(
            dimension_semantics=("arbitrary", "arbitrary", "arbitrary"),
            vmem_limit_bytes=VMEM_LIMIT_BYTES),
        name="attn_b",
    )(qbt, *([kb] * hp), vbt, kbm, vbmt, bias, lam_params, subln_col)


def _post_kernel(x_ref, oa_ref, ob_ref, woa_ref, wob_ref, g_ref, wg_ref, wu_ref, wd_ref, y_ref):
    blocks = [slice(r, r + POST_ROW_BLOCK) for r in range(0, x_ref.shape[0], POST_ROW_BLOCK)]
    x1 = [x_ref[rb, :] + _dot(oa_ref[rb, :], woa_ref[...]) + _dot(ob_ref[rb, :], wob_ref[...])
          for rb in blocks]

    def normed(v):
        ms = jnp.mean(v * v, axis=-1, keepdims=True)
        return (v * lax.rsqrt(ms + EPS) * g_ref[...]).astype(BF16)

    gate_up = []
    for v in x1:
        h = normed(v)
        gate_up.append((_dot(h, wg_ref[...]), _dot(h, wu_ref[...])))
    for rb, v, (gate, up) in zip(blocks, x1, gate_up):
        act = (gate * jax.nn.sigmoid(gate) * up).astype(BF16)
        y_ref[rb, :] = v + _dot(act, wd_ref[...])


def _post(x_rows, oa, ob, wo_a, wo_b, g, wg, wu, wd):
    n_rows, d_model = x_rows.shape
    row_blk = lambda width: pl.BlockSpec((POST_ROWS, width), lambda i: (i, 0))
    resident = lambda a: pl.BlockSpec(a.shape, lambda i: (0, 0), pipeline_mode=pl.Buffered(1))
    return pl.pallas_call(
        _post_kernel,
        grid=(n_rows // POST_ROWS,),
        in_specs=[row_blk(d_model), row_blk(A_WIDTH), row_blk(B_WIDTH),
                  resident(wo_a), resident(wo_b), resident(g),
                  resident(wg), resident(wu), resident(wd)],
        out_specs=row_blk(d_model),
        out_shape=jax.ShapeDtypeStruct((n_rows, d_model), F32),
        compiler_params=pltpu.CompilerParams(
            dimension_semantics=("arbitrary",), vmem_limit_bytes=VMEM_LIMIT_BYTES),
        name="post",
    )(x_rows, oa, ob, wo_a, wo_b, g, wg, wu, wd)


def _pair_tables(ang):
    c, s = jnp.cos(ang), jnp.sin(ang)
    cos64 = jnp.repeat(c, 2, axis=-1)
    sin64 = jnp.stack([-s, s], axis=-1).reshape(ang.shape[0], HEAD_DIM)
    return jnp.tile(cos64, (1, 2)), jnp.tile(sin64, (1, 2))


def _rope_tables(seq):
    half = HEAD_DIM // 2
    t = jnp.arange(seq)
    inv_a = ROPE_THETA ** (-jnp.arange(0, half, 2, dtype=F32) / half)
    r = (t // GRID_W).astype(F32)
    c = (t % GRID_W).astype(F32)
    ang_a = jnp.concatenate([r[:, None] * inv_a[None, :], c[:, None] * inv_a[None, :]], axis=-1)
    inv_b = ROPE_THETA ** (-jnp.arange(0, HEAD_DIM, 2, dtype=F32) / HEAD_DIM)
    pos = jnp.arange(N_META + seq, dtype=F32)
    ang_b = pos[:, None] * inv_b[None, :]
    seq_tabs = _pair_tables(ang_a) + _pair_tables(ang_b[N_META:])
    meta_tabs = _pair_tables(jnp.zeros((N_META, half), F32)) + _pair_tables(ang_b[:N_META])
    return seq_tabs, tuple(_pad_rows(t, META_ROWS) for t in meta_tabs)


def _pad_rows(a, rows):
    return jnp.pad(a, ((0, rows - a.shape[0]), (0, 0)))


def kernel(x, meta_tokens, attn_norm_g, w_in, a_q_norm_g, a_k_norm_g, b_q_norm_g, b_k_norm_g,
           b_lambda_q1, b_lambda_k1, b_lambda_q2, b_lambda_k2, b_subln_g, w_out,
           ffn_norm_g, w_gate, w_up, w_down):
    bsz, seq, d_model = x.shape
    assert w_in.shape[0] == 1, "single-layer block"
    assert seq % PROJ_ROWS == 0 and seq % KV_CHUNK == 0 and seq % A_STEP_COLS == 0 and seq % B_STEP_COLS == 0
    layer = 0

    seq_tabs, meta_tabs = _rope_tables(seq)
    gains = jnp.stack([jnp.tile(g[layer].astype(F32), 2)
                       for g in (a_q_norm_g, a_k_norm_g, b_q_norm_g, b_k_norm_g)])
    idx = jnp.arange(LANES) // HEAD_DIM
    seg = (idx[:, None] == idx[None, :]).astype(BF16)
    w_in_b = w_in[layer].astype(BF16)
    g_attn = attn_norm_g[layer].astype(F32)[None, :]

    qat, ka, vat, qbt, kb, vbt = _project(x, g_attn, w_in_b, seq_tabs, gains, seg, PROJ_ROWS)
    meta3 = _pad_rows(meta_tokens.astype(F32), META_ROWS)[None]
    _, kam, vamt, _, kbm, vbmt = _project(meta3, g_attn, w_in_b, meta_tabs, gains, seg, META_ROWS)
    kam, vamt, kbm, vbmt = kam[0], vamt[0], kbm[0], vbmt[0]
    bias = jnp.where(jnp.arange(META_ROWS) < N_META, 0.0, MASK_BIAS).astype(F32)
    bias = jnp.broadcast_to(bias[:, None], (META_ROWS, LANES))

    lam_params = jnp.stack([jnp.pad(p[layer].astype(F32), (0, LANES - HEAD_DIM))
                            for p in (b_lambda_q1, b_lambda_k1, b_lambda_q2, b_lambda_k2)])
    subln_col = b_subln_g[layer].astype(F32)[:, None]

    def attend(running_max):
        return (_attention_a(qat, ka, vat, kam, vamt, bias, running_max),
                _attention_b(qbt, kb, vbt, kbm, vbmt, bias, lam_params, subln_col, running_max))

    amax = lambda g: jnp.max(jnp.abs(g[layer].astype(F32)))
    score_bound = HEAD_DIM * Q_PRESCALE * jnp.maximum(amax(a_q_norm_g) * amax(a_k_norm_g),
                                                      amax(b_q_norm_g) * amax(b_k_norm_g))
    out_a, out_b = lax.cond(score_bound <= UNSHIFTED_SCORE_LIMIT,
                            lambda: attend(False), lambda: attend(True))

    w_out_b = w_out[layer].astype(BF16)
    y = _post(x.reshape(bsz * seq, d_model), out_a.reshape(bsz * seq, A_WIDTH),
              out_b.reshape(bsz * seq, B_WIDTH), w_out_b[:A_WIDTH], w_out_b[A_WIDTH:],
              ffn_norm_g[layer].astype(F32)[None, :],
              w_gate[layer].astype(BF16), w_up[layer].astype(BF16), w_down[layer].astype(BF16))
    return y.reshape(bsz, seq, d_model)
```

```python
import functools
import math

import jax
import jax.numpy as jnp
from jax import lax
from jax.experimental import pallas as pl
from jax.experimental.pallas import tpu as pltpu

N_META = 16
GRID_W = 64
HEAD_DIM = 64
ROPE_THETA = 10000.0
EPS = 1e-6
A_HEADS = 8
A_KV_HEADS = 2
A_GROUP = A_HEADS // A_KV_HEADS
A_WIDTH = A_HEADS * HEAD_DIM
A_KV_WIDTH = A_KV_HEADS * HEAD_DIM
B_HEADS = 4
B_VDIM = 2 * HEAD_DIM
B_WIDTH = B_HEADS * B_VDIM
LAM_INIT = 0.8 - 0.6 * math.exp(-0.3 * 0)

LANES = 128
VMEM_LIMIT_BYTES = 56 * 1024 * 1024

Q_PRESCALE = (HEAD_DIM ** -0.5) * math.log2(math.e)
MASK_BIAS = -1e30
UNSHIFTED_SCORE_LIMIT = 60.0

PROJ_ROWS = 1024
PROJ_ROW_BLOCK = 256
PROJ_GROUP_COLS = 4
META_ROWS = LANES
MXU_TILE = 256
Q_TILES_PER_STEP = 2
A_Q_COLS = MXU_TILE
A_KV_CHUNK = MXU_TILE
A_LOOKAHEAD = 8
B_Q_COLS = 512
B_HEADS_PER_STEP = 2
B_KV_CHUNK = 512
B_LOOKAHEAD = 1
KV_CHUNK = 512
B_TILES_PER_STEP = 4
A_STEP_COLS = A_Q_COLS * Q_TILES_PER_STEP
B_STEP_COLS = B_Q_COLS * B_TILES_PER_STEP
POST_ROWS = 512
POST_ROW_BLOCK = 256

BF16 = jnp.bfloat16
F32 = jnp.float32


def _dot(a, b):
    return jnp.dot(a, b, preferred_element_type=F32)


def _lane_index(shape):
    return lax.broadcasted_iota(jnp.int32, shape, len(shape) - 1)


def _row_index(shape):
    return lax.broadcasted_iota(jnp.int32, shape, 0)


def _proj_kernel(x_ref, g_ref, w_ref, cos_a_ref, sin_a_ref, cos_b_ref, sin_b_ref,
                 gains_ref, seg_ref,
                 qa_ref, ka_ref, va_ref, qb_ref, kb_ref, vb_ref):
    rows = x_ref.shape[0]
    blk = min(rows, PROJ_ROW_BLOCK)
    lane = _lane_index((blk, LANES))
    even = (lane % 2) == 0
    low_half = lane < HEAD_DIM
    seg = seg_ref[...]
    g_aq = gains_ref[0:1, :] * Q_PRESCALE
    g_ak = gains_ref[1:2, :]
    g_bq = gains_ref[2:3, :] * Q_PRESCALE
    g_bk = gains_ref[3:4, :]

    def norm_rope(col, gain, cos_ref, sin_ref, rb):
        ss = _dot((col * col).astype(BF16), seg)
        y = col * lax.rsqrt(ss * (1.0 / HEAD_DIM) + EPS) * gain
        swapped = jnp.where(even, pltpu.roll(y, LANES - 1, 1), pltpu.roll(y, 1, 1))
        return y * cos_ref[rb, :] + swapped * sin_ref[rb, :]

    def emit_qa(j, col, rb):
        col = norm_rope(col, g_aq, cos_a_ref, sin_a_ref, rb)
        flipped = pltpu.roll(col, HEAD_DIM, 1)
        for half in range(2):
            head = 2 * j + half
            kv = head // A_GROUP
            src = col if half == kv else flipped
            keep = low_half if kv == 0 else jnp.logical_not(low_half)
            qa_ref[head, :, rb] = jnp.where(keep, src, 0.0).T.astype(BF16)

    def emit_ka(j, col, rb):
        ka_ref[rb, :] = norm_rope(col, g_ak, cos_a_ref, sin_a_ref, rb).astype(BF16)

    def emit_va(j, col, rb):
        va_ref[:, rb] = col.T.astype(BF16)

    def emit_qb(j, col, rb):
        qb_ref[j, :, rb] = norm_rope(col, g_bq, cos_b_ref, sin_b_ref, rb).T.astype(BF16)

    def emit_kb(j, col, rb):
        kb_ref[rb, j * LANES:(j + 1) * LANES] = norm_rope(col, g_bk, cos_b_ref, sin_b_ref, rb).astype(BF16)

    def emit_vb(j, col, rb):
        vb_ref[j * LANES:(j + 1) * LANES, rb] = col.T.astype(BF16)

    handlers = ([(emit_qa, j) for j in range(A_WIDTH // LANES)] + [(emit_ka, 0), (emit_va, 0)]
                + [(emit_qb, j) for j in range(B_HEADS)] + [(emit_kb, j) for j in range(B_HEADS)]
                + [(emit_vb, j) for j in range(B_HEADS)])
    assert len(handlers) * LANES == w_ref.shape[1]
    groups = [range(s, min(s + PROJ_GROUP_COLS, len(handlers)))
              for s in range(0, len(handlers), PROJ_GROUP_COLS)]

    row_blocks = [slice(r, r + blk) for r in range(0, rows, blk)]

    def normed(rb):
        x = x_ref[rb, :]
        ms = jnp.mean(x * x, axis=-1, keepdims=True)
        return (x * lax.rsqrt(ms + EPS) * g_ref[...]).astype(BF16)

    hs = [normed(rb) for rb in row_blocks]
    items = [(r, cols) for cols in groups for r in range(len(row_blocks))]

    def project(item):
        r, cols = item
        return _dot(hs[r], w_ref[:, cols.start * LANES:cols.stop * LANES])

    slab_next = project(items[0])
    for pos, (r, cols) in enumerate(items):
        slab = slab_next
        if pos + 1 < len(items):
            slab_next = project(items[pos + 1])
        for n, c in enumerate(cols):
            fn, j = handlers[c]
            fn(j, slab[:, n * LANES:(n + 1) * LANES], row_blocks[r])


def _project(x3, g, w_bf16, tables, gains, seg, rows_per_step):
    bsz, n_rows, d_model = x3.shape
    in_width = w_bf16.shape[1]
    t = rows_per_step
    tab_blk = pl.BlockSpec((t, LANES), lambda b, i: (i, 0))
    const = lambda shape: pl.BlockSpec(shape, lambda b, i: (0, 0))
    out_specs = [
        pl.BlockSpec((None, A_HEADS, LANES, t), lambda b, i: (b, 0, 0, i)),
        pl.BlockSpec((None, t, LANES), lambda b, i: (b, i, 0)),
        pl.BlockSpec((None, LANES, t), lambda b, i: (b, 0, i)),
        pl.BlockSpec((None, B_HEADS, LANES, t), lambda b, i: (b, 0, 0, i)),
        pl.BlockSpec((None, t, B_WIDTH), lambda b, i: (b, i, 0)),
        pl.BlockSpec((None, B_WIDTH, t), lambda b, i: (b, 0, i)),
    ]
    out_shapes = [(bsz, A_HEADS, LANES, n_rows), (bsz, n_rows, LANES), (bsz, LANES, n_rows),
                  (bsz, B_HEADS, LANES, n_rows), (bsz, n_rows, B_WIDTH), (bsz, B_WIDTH, n_rows)]
    return pl.pallas_call(
        _proj_kernel,
        grid=(bsz, n_rows // t),
        in_specs=[pl.BlockSpec((None, t, d_model), lambda b, i: (b, i, 0)),
                  const((1, d_model)), const((d_model, in_width)),
                  tab_blk, tab_blk, tab_blk, tab_blk, const(gains.shape), const(seg.shape)],
        out_specs=out_specs,
        out_shape=[jax.ShapeDtypeStruct(s, BF16) for s in out_shapes],
        compiler_params=pltpu.CompilerParams(
            dimension_semantics=("arbitrary", "arbitrary"), vmem_limit_bytes=VMEM_LIMIT_BYTES),
        name="proj",
    )(x3, g, w_bf16, *tables, gains, seg)


def _attend_t(groups, bias_ref, running_max, finish, lookahead=1, kv_chunk=KV_CHUNK):
    streams = [s for group in groups for s in group]
    n_chunks = streams[0][1].shape[0] // kv_chunk
    items, group_end, base = [], {}, 0
    for g, group in enumerate(groups):
        ids = range(base, base + len(group))
        items += [(i, None) for i in ids] + [(i, c) for c in range(n_chunks) for i in ids]
        group_end[len(items) - 1] = (g, ids)
        base += len(group)

    def scores(item):
        i, c = item
        qt, k_ref, _, km, _ = streams[i]
        if c is None:
            bias = jnp.concatenate([bias_ref[...]] * (qt.shape[1] // LANES), axis=1)
            return _dot(km, qt) + bias
        return _dot(k_ref[c * kv_chunk:(c + 1) * kv_chunk, :], qt)

    def update(i, c, s, state):
        _, _, vt_ref, _, vmt = streams[i]
        vt = vmt if c is None else vt_ref[:, c * kv_chunk:(c + 1) * kv_chunk]
        if not running_max:
            p = jnp.exp2(s)
            l, acc = jnp.sum(p, axis=0, keepdims=True), _dot(vt, p.astype(BF16))
            if state is not None:
                l, acc = state[1] + l, state[2] + acc
            return (None, l, acc)
        s_max = jnp.max(s, axis=0, keepdims=True)
        if state is None:
            p = jnp.exp2(s - s_max)
            return (s_max, jnp.sum(p, axis=0, keepdims=True), _dot(vt, p.astype(BF16)))
        m, l, acc = state
        m_new = jnp.maximum(m, s_max)
        alpha = jnp.exp2(m - m_new)
        p = jnp.exp2(s - m_new)
        return (m_new, alpha * l + jnp.sum(p, axis=0, keepdims=True),
                alpha * acc + _dot(vt, p.astype(BF16)))

    state = [None] * len(streams)
    pending = [scores(item) for item in items[:lookahead]]
    for pos, (i, c) in enumerate(items):
        s = pending.pop(0)
        if pos + lookahead < len(items):
            pending.append(scores(items[pos + lookahead]))
        state[i] = update(i, c, s, state[i])
        if pos in group_end:
            g, ids = group_end[pos]
            finish(g, [state[j][2] / state[j][1] for j in ids])


def _attn_a_kernel(q_ref, k_ref, vt_ref, km_ref, vmt_ref, bias_ref, o_ref, *, running_max):
    tq = A_Q_COLS
    km, vmt = km_ref[...], vmt_ref[...]
    groups = []
    for t in range(q_ref.shape[2] // tq):
        group = []
        for head in range(A_HEADS):
            rows = slice((head // A_GROUP) * HEAD_DIM, (head // A_GROUP + 1) * HEAD_DIM)
            group.append((q_ref[head, :, t * tq:(t + 1) * tq], k_ref, vt_ref.at[rows, :], km, vmt[rows, :]))
        groups.append(group)

    def finish(t, outs):
        o_ref[t * tq:(t + 1) * tq, :] = jnp.concatenate(outs, axis=0).T.astype(o_ref.dtype)

    _attend_t(groups, bias_ref, running_max, finish, lookahead=A_LOOKAHEAD, kv_chunk=A_KV_CHUNK)


def _attn_b_kernel(q_ref, k_ref, vt_ref, km_ref, vmt_ref, bias_ref, lam_ref, subln_ref, o_ref, *,
                   running_max):
    tq = B_Q_COLS
    first = _row_index((LANES, tq)) < HEAD_DIM
    zero = jnp.zeros((LANES, tq), BF16)
    groups = []
    for t in range(q_ref.shape[2] // tq):
        group = []
        for h in range(B_HEADS_PER_STEP):
            q = q_ref[h, :, t * tq:(t + 1) * tq]
            cols = slice(h * LANES, (h + 1) * LANES)
            qt = jnp.concatenate([jnp.where(first, q, zero), jnp.where(first, zero, q)], axis=1)
            group.append((qt, k_ref.at[:, cols], vt_ref.at[cols, :], km_ref[:, cols], vmt_ref[cols, :]))
        groups.append(group)
    lam_p = lam_ref[...]
    lam = (jnp.exp(jnp.sum(lam_p[0:1] * lam_p[1:2], axis=-1, keepdims=True))
           - jnp.exp(jnp.sum(lam_p[2:3] * lam_p[3:4], axis=-1, keepdims=True)) + LAM_INIT)

    def finish(t, outs):
        ys = []
        for o in outs:
            d = o[:, :tq] - lam * o[:, tq:]
            ms = jnp.mean(d * d, axis=0, keepdims=True)
            ys.append(d * lax.rsqrt(ms + EPS) * subln_ref[...] * (1.0 - LAM_INIT))
        o_ref[t * tq:(t + 1) * tq, :] = jnp.concatenate(ys, axis=0).T.astype(o_ref.dtype)

    _attend_t(groups, bias_ref, running_max, finish, lookahead=B_LOOKAHEAD, kv_chunk=B_KV_CHUNK)


def _attention_a(qat, ka, vat, kam, vamt, bias, running_max):
    bsz, _, _, seq = qat.shape
    const = lambda shape: pl.BlockSpec(shape, lambda b, i: (0,) * len(shape))
    return pl.pallas_call(
        functools.partial(_attn_a_kernel, running_max=running_max),
        grid=(bsz, seq // A_STEP_COLS),
        in_specs=[pl.BlockSpec((None, A_HEADS, LANES, A_STEP_COLS), lambda b, i: (b, 0, 0, i)),
                  pl.BlockSpec((None, seq, LANES), lambda b, i: (b, 0, 0)),
                  pl.BlockSpec((None, LANES, seq), lambda b, i: (b, 0, 0)),
                  const(kam.shape), const(vamt.shape), const(bias.shape)],
        out_specs=pl.BlockSpec((None, A_STEP_COLS, A_WIDTH), lambda b, i: (b, i, 0)),
        out_shape=jax.ShapeDtypeStruct((bsz, seq, A_WIDTH), BF16),
        compiler_params=pltpu.CompilerParams(
            dimension_semantics=("arbitrary", "arbitrary"), vmem_limit_bytes=VMEM_LIMIT_BYTES),
        name="attn_a",
    )(qat, ka, vat, kam, vamt, bias)


def _attention_b(qbt, kb, vbt, kbm, vbmt, bias, lam_params, subln_col, running_max):
    bsz, _, _, seq = qbt.shape
    hp = B_HEADS_PER_STEP
    width = hp * LANES
    const = lambda shape: pl.BlockSpec(shape, lambda b, h, i: (0,) * len(shape))
    return pl.pallas_call(
        functools.partial(_attn_b_kernel, running_max=running_max),
        grid=(bsz, B_HEADS // hp, seq // B_STEP_COLS),
        in_specs=[pl.BlockSpec((None, hp, LANES, B_STEP_COLS), lambda b, h, i: (b, h, 0, i)),
                  pl.BlockSpec((None, seq, width), lambda b, h, i: (b, 0, h)),
                  pl.BlockSpec((None, width, seq), lambda b, h, i: (b, h, 0)),
                  pl.BlockSpec((LANES, width), lambda b, h, i: (0, h)),
                  pl.BlockSpec((width, LANES), lambda b, h, i: (h, 0)),
                  const(bias.shape), const(lam_params.shape), const(subln_col.shape)],
        out_specs=pl.BlockSpec((None, B_STEP_COLS, width), lambda b, h, i: (b, i, h)),
        out_shape=jax.ShapeDtypeStruct((bsz, seq, B_WIDTH), BF16),
        compiler_params=pltpu.CompilerParams(
            dimension_semantics=("arbitrary", "arbitrary", "arbitrary"),
            vmem_limit_bytes=VMEM_LIMIT_BYTES),
        name="attn_b",
    )(qbt, kb, vbt, kbm, vbmt, bias, lam_params, subln_col)


def _post_kernel(x_ref, oa_ref, ob_ref, woa_ref, wob_ref, g_ref, wg_ref, wu_ref, wd_ref, y_ref):
    blocks = [slice(r, r + POST_ROW_BLOCK) for r in range(0, x_ref.shape[0], POST_ROW_BLOCK)]
    x1 = [x_ref[rb, :] + _dot(oa_ref[rb, :], woa_ref[...]) + _dot(ob_ref[rb, :], wob_ref[...])
          for rb in blocks]

    def normed(v):
        ms = jnp.mean(v * v, axis=-1, keepdims=True)
        return (v * lax.rsqrt(ms + EPS) * g_ref[...]).astype(BF16)

    gate_up = []
    for v in x1:
        h = normed(v)
        gate_up.append((_dot(h, wg_ref[...]), _dot(h, wu_ref[...])))
    for rb, v, (gate, up) in zip(blocks, x1, gate_up):
        act = (gate * jax.nn.sigmoid(gate) * up).astype(BF16)
        y_ref[rb, :] = v + _dot(act, wd_ref[...])


def _post(x_rows, oa, ob, wo_a, wo_b, g, wg, wu, wd):
    n_rows, d_model = x_rows.shape
    row_blk = lambda width: pl.BlockSpec((POST_ROWS, width), lambda i: (i, 0))
    resident = lambda a: pl.BlockSpec(a.shape, lambda i: (0, 0), pipeline_mode=pl.Buffered(1))
    return pl.pallas_call(
        _post_kernel,
        grid=(n_rows // POST_ROWS,),
        in_specs=[row_blk(d_model), row_blk(A_WIDTH), row_blk(B_WIDTH),
                  resident(wo_a), resident(wo_b), resident(g),
                  resident(wg), resident(wu), resident(wd)],
        out_specs=row_blk(d_model),
        out_shape=jax.ShapeDtypeStruct((n_rows, d_model), F32),
        compiler_params=pltpu.CompilerParams(
            dimension_semantics=("arbitrary",), vmem_limit_bytes=VMEM_LIMIT_BYTES),
        name="post",
    )(x_rows, oa, ob, wo_a, wo_b, g, wg, wu, wd)


def _pair_tables(ang):
    c, s = jnp.cos(ang), jnp.sin(ang)
    cos64 = jnp.repeat(c, 2, axis=-1)
    sin64 = jnp.stack([-s, s], axis=-1).reshape(ang.shape[0], HEAD_DIM)
    return jnp.tile(cos64, (1, 2)), jnp.tile(sin64, (1, 2))


def _rope_tables(seq):
    half = HEAD_DIM // 2
    t = jnp.arange(seq)
    inv_a = ROPE_THETA ** (-jnp.arange(0, half, 2, dtype=F32) / half)
    r = (t // GRID_W).astype(F32)
    c = (t % GRID_W).astype(F32)
    ang_a = jnp.concatenate([r[:, None] * inv_a[None, :], c[:, None] * inv_a[None, :]], axis=-1)
    inv_b = ROPE_THETA ** (-jnp.arange(0, HEAD_DIM, 2, dtype=F32) / HEAD_DIM)
    pos = jnp.arange(N_META + seq, dtype=F32)
    ang_b = pos[:, None] * inv_b[None, :]
    seq_tabs = _pair_tables(ang_a) + _pair_tables(ang_b[N_META:])
    meta_tabs = _pair_tables(jnp.zeros((N_META, half), F32)) + _pair_tables(ang_b[:N_META])
    return seq_tabs, tuple(_pad_rows(t, META_ROWS) for t in meta_tabs)


def _pad_rows(a, rows):
    return jnp.pad(a, ((0, rows - a.shape[0]), (0, 0)))


def kernel(x, meta_tokens, attn_norm_g, w_in, a_q_norm_g, a_k_norm_g, b_q_norm_g, b_k_norm_g,
           b_lambda_q1, b_lambda_k1, b_lambda_q2, b_lambda_k2, b_subln_g, w_out,
           ffn_norm_g, w_gate, w_up, w_down):
    bsz, seq, d_model = x.shape
    assert w_in.shape[0] == 1, "single-layer block"
    assert seq % PROJ_ROWS == 0 and seq % KV_CHUNK == 0 and seq % A_STEP_COLS == 0 and seq % B_STEP_COLS == 0
    layer = 0

    seq_tabs, meta_tabs = _rope_tables(seq)
    gains = jnp.stack([jnp.tile(g[layer].astype(F32), 2)
                       for g in (a_q_norm_g, a_k_norm_g, b_q_norm_g, b_k_norm_g)])
    idx = jnp.arange(LANES) // HEAD_DIM
    seg = (idx[:, None] == idx[None, :]).astype(BF16)
    w_in_b = w_in[layer].astype(BF16)
    g_attn = attn_norm_g[layer].astype(F32)[None, :]

    qat, ka, vat, qbt, kb, vbt = _project(x, g_attn, w_in_b, seq_tabs, gains, seg, PROJ_ROWS)
    meta3 = _pad_rows(meta_tokens.astype(F32), META_ROWS)[None]
    _, kam, vamt, _, kbm, vbmt = _project(meta3, g_attn, w_in_b, meta_tabs, gains, seg, META_ROWS)
    kam, vamt, kbm, vbmt = kam[0], vamt[0], kbm[0], vbmt[0]
    bias = jnp.where(jnp.arange(META_ROWS) < N_META, 0.0, MASK_BIAS).astype(F32)
    bias = jnp.broadcast_to(bias[:, None], (META_ROWS, LANES))

    lam_params = jnp.stack([jnp.pad(p[layer].astype(F32), (0, LANES - HEAD_DIM))
                            for p in (b_lambda_q1, b_lambda_k1, b_lambda_q2, b_lambda_k2)])
    subln_col = b_subln_g[layer].astype(F32)[:, None]

    def attend(running_max):
        return (_attention_a(qat, ka, vat, kam, vamt, bias, running_max),
                _attention_b(qbt, kb, vbt, kbm, vbmt, bias, lam_params, subln_col, running_max))

    amax = lambda g: jnp.max(jnp.abs(g[layer].astype(F32)))
    score_bound = HEAD_DIM * Q_PRESCALE * jnp.maximum(amax(a_q_norm_g) * amax(a_k_norm_g),
                                                      amax(b_q_norm_g) * amax(b_k_norm_g))
    out_a, out_b = lax.cond(score_bound <= UNSHIFTED_SCORE_LIMIT,
                            lambda: attend(False), lambda: attend(True))

    w_out_b = w_out[layer].astype(BF16)
    y = _post(x.reshape(bsz * seq, d_model), out_a.reshape(bsz * seq, A_WIDTH),
              out_b.reshape(bsz * seq, B_WIDTH), w_out_b[:A_WIDTH], w_out_b[A_WIDTH:],
              ffn_norm_g[layer].astype(F32)[None, :],
              w_gate[layer].astype(BF16), w_up[layer].astype(BF16), w_down[layer].astype(BF16))
    return y.reshape(bsz, seq, d_model)
```

```python
import functools
import math

import jax
import jax.numpy as jnp
from jax import lax
from jax.experimental import pallas as pl
from jax.experimental.pallas import tpu as pltpu

N_META = 16
GRID_W = 64
HEAD_DIM = 64
ROPE_THETA = 10000.0
EPS = 1e-6
A_HEADS = 8
A_KV_HEADS = 2
A_GROUP = A_HEADS // A_KV_HEADS
A_WIDTH = A_HEADS * HEAD_DIM
A_KV_WIDTH = A_KV_HEADS * HEAD_DIM
B_HEADS = 4
B_VDIM = 2 * HEAD_DIM
B_WIDTH = B_HEADS * B_VDIM
LAM_INIT = 0.8 - 0.6 * math.exp(-0.3 * 0)

LANES = 128
VMEM_LIMIT_BYTES = 56 * 1024 * 1024

Q_PRESCALE = (HEAD_DIM ** -0.5) * math.log2(math.e)
MASK_BIAS = -1e30
UNSHIFTED_SCORE_LIMIT = 60.0

PROJ_ROWS = 1024
PROJ_ROW_BLOCK = 256
PROJ_GROUP_COLS = 4
META_ROWS = LANES
MXU_TILE = 256
Q_TILES_PER_STEP = 2
A_Q_COLS = MXU_TILE
A_KV_CHUNK = MXU_TILE
A_LOOKAHEAD = 8
B_Q_COLS = 512
B_HEADS_PER_STEP = 2
B_KV_CHUNK = 512
B_LOOKAHEAD = 1
KV_CHUNK = 512
A_STEP_COLS = A_Q_COLS * Q_TILES_PER_STEP
B_STEP_COLS = B_Q_COLS * Q_TILES_PER_STEP
POST_ROWS = 512
POST_ROW_BLOCK = 256

BF16 = jnp.bfloat16
F32 = jnp.float32


def _dot(a, b):
    return jnp.dot(a, b, preferred_element_type=F32)


def _lane_index(shape):
    return lax.broadcasted_iota(jnp.int32, shape, len(shape) - 1)


def _row_index(shape):
    return lax.broadcasted_iota(jnp.int32, shape, 0)


def _proj_kernel(x_ref, g_ref, w_ref, cos_a_ref, sin_a_ref, cos_b_ref, sin_b_ref,
                 gains_ref, seg_ref,
                 qa_ref, ka_ref, va_ref, qb_ref, kb_ref, vb_ref):
    rows = x_ref.shape[0]
    blk = min(rows, PROJ_ROW_BLOCK)
    lane = _lane_index((blk, LANES))
    even = (lane % 2) == 0
    low_half = lane < HEAD_DIM
    seg_pair = seg_ref[...]
    seg_one = seg_ref[0:LANES, 0:LANES]
    g_aq = gains_ref[0:1, :] * Q_PRESCALE
    g_ak = gains_ref[1:2, :]
    g_bq = gains_ref[2:3, :] * Q_PRESCALE
    g_bk = gains_ref[3:4, :]

    def norm_rope(col, ss, gain, cos_ref, sin_ref, rb):
        y = col * lax.rsqrt(ss * (1.0 / HEAD_DIM) + EPS) * gain
        swapped = jnp.where(even, pltpu.roll(y, LANES - 1, 1), pltpu.roll(y, 1, 1))
        return y * cos_ref[rb, :] + swapped * sin_ref[rb, :]

    def emit_qa(j, col, ss, rb):
        col = norm_rope(col, ss, g_aq, cos_a_ref, sin_a_ref, rb)
        flipped = pltpu.roll(col, HEAD_DIM, 1)
        for half in range(2):
            head = 2 * j + half
            kv = head // A_GROUP
            src = col if half == kv else flipped
            keep = low_half if kv == 0 else jnp.logical_not(low_half)
            qa_ref[head, :, rb] = jnp.where(keep, src, 0.0).T.astype(BF16)

    def emit_ka(j, col, ss, rb):
        ka_ref[rb, :] = norm_rope(col, ss, g_ak, cos_a_ref, sin_a_ref, rb).astype(BF16)

    def emit_va(j, col, ss, rb):
        va_ref[:, rb] = col.T.astype(BF16)

    def emit_qb(j, col, ss, rb):
        qb_ref[j, :, rb] = norm_rope(col, ss, g_bq, cos_b_ref, sin_b_ref, rb).T.astype(BF16)

    def emit_kb(j, col, ss, rb):
        kb_ref[rb, j * LANES:(j + 1) * LANES] = norm_rope(col, ss, g_bk, cos_b_ref, sin_b_ref, rb).astype(BF16)

    def emit_vb(j, col, ss, rb):
        vb_ref[j * LANES:(j + 1) * LANES, rb] = col.T.astype(BF16)

    handlers = ([(emit_qa, j) for j in range(A_WIDTH // LANES)] + [(emit_ka, 0), (emit_va, 0)]
                + [(emit_qb, j) for j in range(B_HEADS)] + [(emit_kb, j) for j in range(B_HEADS)]
                + [(emit_vb, j) for j in range(B_HEADS)])
    assert len(handlers) * LANES == w_ref.shape[1]
    groups = [range(s, min(s + PROJ_GROUP_COLS, len(handlers)))
              for s in range(0, len(handlers), PROJ_GROUP_COLS)]

    row_blocks = [slice(r, r + blk) for r in range(0, rows, blk)]

    def normed(rb):
        x = x_ref[rb, :]
        ms = jnp.mean(x * x, axis=-1, keepdims=True)
        return (x * lax.rsqrt(ms + EPS) * g_ref[...]).astype(BF16)

    hs = [normed(rb) for rb in row_blocks]
    items = [(r, cols) for cols in groups for r in range(len(row_blocks))]

    def sums_of_squares(slab, cols):
        needs = [handlers[c][0] not in (emit_va, emit_vb) for c in cols]
        ss, n = [None] * len(cols), 0
        while n < len(cols):
            width = 2 if (needs[n] and n + 1 < len(cols) and needs[n + 1]) else 1
            if needs[n]:
                part = slab[:, n * LANES:(n + width) * LANES]
                tot = _dot((part * part).astype(BF16), seg_pair if width == 2 else seg_one)
                for w in range(width):
                    ss[n + w] = tot[:, w * LANES:(w + 1) * LANES]
            n += width
        return ss

    def project(item):
        r, cols = item
        return _dot(hs[r], w_ref[:, cols.start * LANES:cols.stop * LANES])

    slab_next = project(items[0])
    for pos, (r, cols) in enumerate(items):
        slab = slab_next
        if pos + 1 < len(items):
            slab_next = project(items[pos + 1])
        ss = sums_of_squares(slab, cols)
        for n, c in enumerate(cols):
            fn, j = handlers[c]
            fn(j, slab[:, n * LANES:(n + 1) * LANES], ss[n], row_blocks[r])


def _project(x3, g, w_bf16, tables, gains, seg, rows_per_step):
    bsz, n_rows, d_model = x3.shape
    in_width = w_bf16.shape[1]
    t = rows_per_step
    tab_blk = pl.BlockSpec((t, LANES), lambda b, i: (i, 0))
    const = lambda shape: pl.BlockSpec(shape, lambda b, i: (0, 0))
    out_specs = [
        pl.BlockSpec((None, A_HEADS, LANES, t), lambda b, i: (b, 0, 0, i)),
        pl.BlockSpec((None, t, LANES), lambda b, i: (b, i, 0)),
        pl.BlockSpec((None, LANES, t), lambda b, i: (b, 0, i)),
        pl.BlockSpec((None, B_HEADS, LANES, t), lambda b, i: (b, 0, 0, i)),
        pl.BlockSpec((None, t, B_WIDTH), lambda b, i: (b, i, 0)),
        pl.BlockSpec((None, B_WIDTH, t), lambda b, i: (b, 0, i)),
    ]
    out_shapes = [(bsz, A_HEADS, LANES, n_rows), (bsz, n_rows, LANES), (bsz, LANES, n_rows),
                  (bsz, B_HEADS, LANES, n_rows), (bsz, n_rows, B_WIDTH), (bsz, B_WIDTH, n_rows)]
    return pl.pallas_call(
        _proj_kernel,
        grid=(bsz, n_rows // t),
        in_specs=[pl.BlockSpec((None, t, d_model), lambda b, i: (b, i, 0)),
                  const((1, d_model)), const((d_model, in_width)),
                  tab_blk, tab_blk, tab_blk, tab_blk, const(gains.shape), const(seg.shape)],
        out_specs=out_specs,
        out_shape=[jax.ShapeDtypeStruct(s, BF16) for s in out_shapes],
        compiler_params=pltpu.CompilerParams(
            dimension_semantics=("arbitrary", "arbitrary"), vmem_limit_bytes=VMEM_LIMIT_BYTES),
        name="proj",
    )(x3, g, w_bf16, *tables, gains, seg)


def _attend_t(groups, bias_ref, running_max, finish, lookahead=1, kv_chunk=KV_CHUNK):
    streams = [s for group in groups for s in group]
    n_chunks = streams[0][1].shape[0] // kv_chunk
    items, group_end, base = [], {}, 0
    for g, group in enumerate(groups):
        ids = range(base, base + len(group))
        items += [(i, None) for i in ids] + [(i, c) for c in range(n_chunks) for i in ids]
        group_end[len(items) - 1] = (g, ids)
        base += len(group)

    def scores(item):
        i, c = item
        qt, k_ref, _, km, _ = streams[i]
        if c is None:
            bias = jnp.concatenate([bias_ref[...]] * (qt.shape[1] // LANES), axis=1)
            return _dot(km, qt) + bias
        return _dot(k_ref[c * kv_chunk:(c + 1) * kv_chunk, :], qt)

    def update(i, c, s, state):
        _, _, vt_ref, _, vmt = streams[i]
        vt = vmt if c is None else vt_ref[:, c * kv_chunk:(c + 1) * kv_chunk]
        if not running_max:
            p = jnp.exp2(s)
            l, acc = jnp.sum(p, axis=0, keepdims=True), _dot(vt, p.astype(BF16))
            if state is not None:
                l, acc = state[1] + l, state[2] + acc
            return (None, l, acc)
        s_max = jnp.max(s, axis=0, keepdims=True)
        if state is None:
            p = jnp.exp2(s - s_max)
            return (s_max, jnp.sum(p, axis=0, keepdims=True), _dot(vt, p.astype(BF16)))
        m, l, acc = state
        m_new = jnp.maximum(m, s_max)
        alpha = jnp.exp2(m - m_new)
        p = jnp.exp2(s - m_new)
        return (m_new, alpha * l + jnp.sum(p, axis=0, keepdims=True),
                alpha * acc + _dot(vt, p.astype(BF16)))

    state = [None] * len(streams)
    pending = [scores(item) for item in items[:lookahead]]
    for pos, (i, c) in enumerate(items):
        s = pending.pop(0)
        if pos + lookahead < len(items):
            pending.append(scores(items[pos + lookahead]))
        state[i] = update(i, c, s, state[i])
        if pos in group_end:
            g, ids = group_end[pos]
            finish(g, [state[j][2] / state[j][1] for j in ids])


def _attn_a_kernel(q_ref, k_ref, vt_ref, km_ref, vmt_ref, bias_ref, o_ref, *, running_max):
    tq = A_Q_COLS
    km, vmt = km_ref[...], vmt_ref[...]
    groups = []
    for t in range(q_ref.shape[2] // tq):
        group = []
        for head in range(A_HEADS):
            rows = slice((head // A_GROUP) * HEAD_DIM, (head // A_GROUP + 1) * HEAD_DIM)
            group.append((q_ref[head, :, t * tq:(t + 1) * tq], k_ref, vt_ref.at[rows, :], km, vmt[rows, :]))
        groups.append(group)

    def finish(t, outs):
        o_ref[t * tq:(t + 1) * tq, :] = jnp.concatenate(outs, axis=0).T.astype(o_ref.dtype)

    _attend_t(groups, bias_ref, running_max, finish, lookahead=A_LOOKAHEAD, kv_chunk=A_KV_CHUNK)


def _attn_b_kernel(q_ref, k_ref, vt_ref, km_ref, vmt_ref, bias_ref, lam_ref, subln_ref, o_ref, *,
                   running_max):
    tq = B_Q_COLS
    first = _row_index((LANES, tq)) < HEAD_DIM
    zero = jnp.zeros((LANES, tq), BF16)
    groups = []
    for t in range(q_ref.shape[2] // tq):
        group = []
        for h in range(B_HEADS_PER_STEP):
            q = q_ref[h, :, t * tq:(t + 1) * tq]
            cols = slice(h * LANES, (h + 1) * LANES)
            qt = jnp.concatenate([jnp.where(first, q, zero), jnp.where(first, zero, q)], axis=1)
            group.append((qt, k_ref.at[:, cols], vt_ref.at[cols, :], km_ref[:, cols], vmt_ref[cols, :]))
        groups.append(group)
    lam_p = lam_ref[...]
    lam = (jnp.exp(jnp.sum(lam_p[0:1] * lam_p[1:2], axis=-1, keepdims=True))
           - jnp.exp(jnp.sum(lam_p[2:3] * lam_p[3:4], axis=-1, keepdims=True)) + LAM_INIT)

    def finish(t, outs):
        ys = []
        for o in outs:
            d = o[:, :tq] - lam * o[:, tq:]
            ms = jnp.mean(d * d, axis=0, keepdims=True)
            ys.append(d * lax.rsqrt(ms + EPS) * subln_ref[...] * (1.0 - LAM_INIT))
        o_ref[t * tq:(t + 1) * tq, :] = jnp.concatenate(ys, axis=0).T.astype(o_ref.dtype)

    _attend_t(groups, bias_ref, running_max, finish, lookahead=B_LOOKAHEAD, kv_chunk=B_KV_CHUNK)


def _attention_a(qat, ka, vat, kam, vamt, bias, running_max):
    bsz, _, _, seq = qat.shape
    const = lambda shape: pl.BlockSpec(shape, lambda b, i: (0,) * len(shape))
    return pl.pallas_call(
        functools.partial(_attn_a_kernel, running_max=running_max),
        grid=(bsz, seq // A_STEP_COLS),
        in_specs=[pl.BlockSpec((None, A_HEADS, LANES, A_STEP_COLS), lambda b, i: (b, 0, 0, i)),
                  pl.BlockSpec((None, seq, LANES), lambda b, i: (b, 0, 0)),
                  pl.BlockSpec((None, LANES, seq), lambda b, i: (b, 0, 0)),
                  const(kam.shape), const(vamt.shape), const(bias.shape)],
        out_specs=pl.BlockSpec((None, A_STEP_COLS, A_WIDTH), lambda b, i: (b, i, 0)),
        out_shape=jax.ShapeDtypeStruct((bsz, seq, A_WIDTH), BF16),
        compiler_params=pltpu.CompilerParams(
            dimension_semantics=("arbitrary", "arbitrary"), vmem_limit_bytes=VMEM_LIMIT_BYTES),
        name="attn_a",
    )(qat, ka, vat, kam, vamt, bias)


def _attention_b(qbt, kb, vbt, kbm, vbmt, bias, lam_params, subln_col, running_max):
    bsz, _, _, seq = qbt.shape
    hp = B_HEADS_PER_STEP
    width = hp * LANES
    const = lambda shape: pl.BlockSpec(shape, lambda b, h, i: (0,) * len(shape))
    return pl.pallas_call(
        functools.partial(_attn_b_kernel, running_max=running_max),
        grid=(bsz, B_HEADS // hp, seq // B_STEP_COLS),
        in_specs=[pl.BlockSpec((None, hp, LANES, B_STEP_COLS), lambda b, h, i: (b, h, 0, i)),
                  pl.BlockSpec((None, seq, width), lambda b, h, i: (b, 0, h)),
                  pl.BlockSpec((None, width, seq), lambda b, h, i: (b, h, 0)),
                  pl.BlockSpec((LANES, width), lambda b, h, i: (0, h)),
                  pl.BlockSpec((width, LANES), lambda b, h, i: (h, 0)),
                  const(bias.shape), const(lam_params.shape), const(subln_col.shape)],
        out_specs=pl.BlockSpec((None, B_STEP_COLS, width), lambda b, h, i: (b, i, h)),
        out_shape=jax.ShapeDtypeStruct((bsz, seq, B_WIDTH), BF16),
        compiler_params=pltpu.CompilerParams(
            dimension_semantics=("arbitrary", "arbitrary", "arbitrary"),
            vmem_limit_bytes=VMEM_LIMIT_BYTES),
        name="attn_b",
    )(qbt, kb, vbt, kbm, vbmt, bias, lam_params, subln_col)


def _post_kernel(x_ref, oa_ref, ob_ref, wo_ref, g_ref, wg_ref, wu_ref, wd_ref, y_ref):
    woa_ref, wob_ref = wo_ref.at[0:A_WIDTH, :], wo_ref.at[A_WIDTH:A_WIDTH + B_WIDTH, :]
    blocks = [slice(r, r + POST_ROW_BLOCK) for r in range(0, x_ref.shape[0], POST_ROW_BLOCK)]
    x1 = [x_ref[rb, :] + _dot(oa_ref[rb, :], woa_ref[...]) + _dot(ob_ref[rb, :], wob_ref[...])
          for rb in blocks]

    def normed(v):
        ms = jnp.mean(v * v, axis=-1, keepdims=True)
        return (v * lax.rsqrt(ms + EPS) * g_ref[...]).astype(BF16)

    gate_up = []
    for v in x1:
        h = normed(v)
        gate_up.append((_dot(h, wg_ref[...]), _dot(h, wu_ref[...])))
    for rb, v, (gate, up) in zip(blocks, x1, gate_up):
        act = (gate * jax.nn.sigmoid(gate) * up).astype(BF16)
        y_ref[rb, :] = v + _dot(act, wd_ref[...])


def _post(x_rows, oa, ob, wo, g, wg, wu, wd):
    n_rows, d_model = x_rows.shape
    row_blk = lambda width: pl.BlockSpec((POST_ROWS, width), lambda i: (i, 0))
    resident = lambda a: pl.BlockSpec(a.shape, lambda i: (0, 0), pipeline_mode=pl.Buffered(1))
    return pl.pallas_call(
        _post_kernel,
        grid=(n_rows // POST_ROWS,),
        in_specs=[row_blk(d_model), row_blk(A_WIDTH), row_blk(B_WIDTH),
                  resident(wo), resident(g),
                  resident(wg), resident(wu), resident(wd)],
        out_specs=row_blk(d_model),
        out_shape=jax.ShapeDtypeStruct((n_rows, d_model), F32),
        compiler_params=pltpu.CompilerParams(
            dimension_semantics=("arbitrary",), vmem_limit_bytes=VMEM_LIMIT_BYTES),
        name="post",
    )(x_rows, oa, ob, wo, g, wg, wu, wd)


def _pair_tables(ang):
    c, s = jnp.cos(ang), jnp.sin(ang)
    cos64 = jnp.repeat(c, 2, axis=-1)
    sin64 = jnp.stack([-s, s], axis=-1).reshape(ang.shape[0], HEAD_DIM)
    return jnp.tile(cos64, (1, 2)), jnp.tile(sin64, (1, 2))


def _rope_tables(seq):
    half = HEAD_DIM // 2
    t = jnp.arange(seq)
    inv_a = ROPE_THETA ** (-jnp.arange(0, half, 2, dtype=F32) / half)
    r = (t // GRID_W).astype(F32)
    c = (t % GRID_W).astype(F32)
    ang_a = jnp.concatenate([r[:, None] * inv_a[None, :], c[:, None] * inv_a[None, :]], axis=-1)
    inv_b = ROPE_THETA ** (-jnp.arange(0, HEAD_DIM, 2, dtype=F32) / HEAD_DIM)
    pos = jnp.arange(N_META + seq, dtype=F32)
    ang_b = pos[:, None] * inv_b[None, :]
    seq_tabs = _pair_tables(ang_a) + _pair_tables(ang_b[N_META:])
    meta_tabs = _pair_tables(jnp.zeros((N_META, half), F32)) + _pair_tables(ang_b[:N_META])
    return seq_tabs, tuple(_pad_rows(t, META_ROWS) for t in meta_tabs)


def _pad_rows(a, rows):
    return jnp.pad(a, ((0, rows - a.shape[0]), (0, 0)))


def kernel(x, meta_tokens, attn_norm_g, w_in, a_q_norm_g, a_k_norm_g, b_q_norm_g, b_k_norm_g,
           b_lambda_q1, b_lambda_k1, b_lambda_q2, b_lambda_k2, b_subln_g, w_out,
           ffn_norm_g, w_gate, w_up, w_down):
    bsz, seq, d_model = x.shape
    assert w_in.shape[0] == 1, "single-layer block"
    assert seq % PROJ_ROWS == 0 and seq % KV_CHUNK == 0 and seq % A_STEP_COLS == 0 and seq % B_STEP_COLS == 0
    layer = 0

    seq_tabs, meta_tabs = _rope_tables(seq)
    gains = jnp.stack([jnp.tile(g[layer].astype(F32), 2)
                       for g in (a_q_norm_g, a_k_norm_g, b_q_norm_g, b_k_norm_g)])
    idx = jnp.arange(2 * LANES) // HEAD_DIM
    seg = (idx[:, None] == idx[None, :]).astype(BF16)
    w_in_b = w_in[layer].astype(BF16)
    g_attn = attn_norm_g[layer].astype(F32)[None, :]

    qat, ka, vat, qbt, kb, vbt = _project(x, g_attn, w_in_b, seq_tabs, gains, seg, PROJ_ROWS)
    meta3 = _pad_rows(meta_tokens.astype(F32), META_ROWS)[None]
    _, kam, vamt, _, kbm, vbmt = _project(meta3, g_attn, w_in_b, meta_tabs, gains, seg, META_ROWS)
    kam, vamt, kbm, vbmt = kam[0], vamt[0], kbm[0], vbmt[0]
    bias = jnp.where(jnp.arange(META_ROWS) < N_META, 0.0, MASK_BIAS).astype(F32)
    bias = jnp.broadcast_to(bias[:, None], (META_ROWS, LANES))

    lam_params = jnp.stack([jnp.pad(p[layer].astype(F32), (0, LANES - HEAD_DIM))
                            for p in (b_lambda_q1, b_lambda_k1, b_lambda_q2, b_lambda_k2)])
    subln_col = b_subln_g[layer].astype(F32)[:, None]

    def attend(running_max):
        return (_attention_a(qat, ka, vat, kam, vamt, bias, running_max),
                _attention_b(qbt, kb, vbt, kbm, vbmt, bias, lam_params, subln_col, running_max))

    amax = lambda g: jnp.max(jnp.abs(g[layer].astype(F32)))
    score_bound = HEAD_DIM * Q_PRESCALE * jnp.maximum(amax(a_q_norm_g) * amax(a_k_norm_g),
                                                      amax(b_q_norm_g) * amax(b_k_norm_g))
    out_a, out_b = lax.cond(score_bound <= UNSHIFTED_SCORE_LIMIT,
                            lambda: attend(False), lambda: attend(True))

    w_out_b = w_out[layer].astype(BF16)
    y = _post(x.reshape(bsz * seq, d_model), out_a.reshape(bsz * seq, A_WIDTH),
              out_b.reshape(bsz * seq, B_WIDTH), w_out_b,
              ffn_norm_g[layer].astype(F32)[None, :],
              w_gate[layer].astype(BF16), w_up[layer].astype(BF16), w_down[layer].astype(BF16))
    return y.reshape(bsz, seq, d_model)
```

```python
import functools
import math

import jax
import jax.numpy as jnp
from jax import lax
from jax.experimental import pallas as pl
from jax.experimental.pallas import tpu as pltpu

N_META = 16
GRID_W = 64
HEAD_DIM = 64
ROPE_THETA = 10000.0
EPS = 1e-6
A_HEADS = 8
A_KV_HEADS = 2
A_GROUP = A_HEADS // A_KV_HEADS
A_WIDTH = A_HEADS * HEAD_DIM
A_KV_WIDTH = A_KV_HEADS * HEAD_DIM
B_HEADS = 4
B_VDIM = 2 * HEAD_DIM
B_WIDTH = B_HEADS * B_VDIM
LAM_INIT = 0.8 - 0.6 * math.exp(-0.3 * 0)

LANES = 128
VMEM_LIMIT_BYTES = 56 * 1024 * 1024

Q_PRESCALE = (HEAD_DIM ** -0.5) * math.log2(math.e)
MASK_BIAS = -1e30
UNSHIFTED_SCORE_LIMIT = 60.0

PROJ_ROWS = 1024
PROJ_ROW_BLOCK = 256
PROJ_GROUP_COLS = 4
META_ROWS = LANES
MXU_TILE = 256
Q_TILES_PER_STEP = 2
A_Q_COLS = MXU_TILE
A_KV_CHUNK = MXU_TILE
A_LOOKAHEAD = 8
B_Q_COLS = 512
B_HEADS_PER_STEP = 2
B_KV_CHUNK = MXU_TILE
B_LOOKAHEAD = 8
KV_CHUNK = 512
A_STEP_COLS = A_Q_COLS * Q_TILES_PER_STEP
B_STEP_COLS = B_Q_COLS * Q_TILES_PER_STEP
POST_ROWS = 512
POST_ROW_BLOCK = 256

BF16 = jnp.bfloat16
F32 = jnp.float32


def _dot(a, b):
    return jnp.dot(a, b, preferred_element_type=F32)


def _lane_index(shape):
    return lax.broadcasted_iota(jnp.int32, shape, len(shape) - 1)


def _row_index(shape):
    return lax.broadcasted_iota(jnp.int32, shape, 0)


def _proj_kernel(x_ref, g_ref, w_ref, cos_a_ref, sin_a_ref, cos_b_ref, sin_b_ref,
                 gains_ref, seg_ref,
                 qa_ref, ka_ref, va_ref, qb_ref, kb_ref, vb_ref):
    rows = x_ref.shape[0]
    blk = min(rows, PROJ_ROW_BLOCK)
    lane = _lane_index((blk, LANES))
    even = (lane % 2) == 0
    low_half = lane < HEAD_DIM
    seg_pair = seg_ref[...]
    seg_one = seg_ref[0:LANES, 0:LANES]
    g_aq = gains_ref[0:1, :] * Q_PRESCALE
    g_ak = gains_ref[1:2, :]
    g_bq = gains_ref[2:3, :] * Q_PRESCALE
    g_bk = gains_ref[3:4, :]

    def norm_rope(col, ss, gain, cos_ref, sin_ref, rb):
        y = col * lax.rsqrt(ss * (1.0 / HEAD_DIM) + EPS) * gain
        swapped = jnp.where(even, pltpu.roll(y, LANES - 1, 1), pltpu.roll(y, 1, 1))
        return y * cos_ref[rb, :] + swapped * sin_ref[rb, :]

    def emit_qa(j, col, ss, rb):
        col = norm_rope(col, ss, g_aq, cos_a_ref, sin_a_ref, rb)
        flipped = pltpu.roll(col, HEAD_DIM, 1)
        for half in range(2):
            head = 2 * j + half
            kv = head // A_GROUP
            src = col if half == kv else flipped
            keep = low_half if kv == 0 else jnp.logical_not(low_half)
            qa_ref[head, :, rb] = jnp.where(keep, src, 0.0).T.astype(BF16)

    def emit_ka(j, col, ss, rb):
        ka_ref[rb, :] = norm_rope(col, ss, g_ak, cos_a_ref, sin_a_ref, rb).astype(BF16)

    def emit_va(j, col, ss, rb):
        va_ref[:, rb] = col.T.astype(BF16)

    def emit_qb(j, col, ss, rb):
        qb_ref[j, :, rb] = norm_rope(col, ss, g_bq, cos_b_ref, sin_b_ref, rb).T.astype(BF16)

    def emit_kb(j, col, ss, rb):
        kb_ref[rb, j * LANES:(j + 1) * LANES] = norm_rope(col, ss, g_bk, cos_b_ref, sin_b_ref, rb).astype(BF16)

    def emit_vb(j, col, ss, rb):
        vb_ref[j * LANES:(j + 1) * LANES, rb] = col.T.astype(BF16)

    handlers = ([(emit_qa, j) for j in range(A_WIDTH // LANES)] + [(emit_ka, 0), (emit_va, 0)]
                + [(emit_qb, j) for j in range(B_HEADS)] + [(emit_kb, j) for j in range(B_HEADS)]
                + [(emit_vb, j) for j in range(B_HEADS)])
    assert len(handlers) * LANES == w_ref.shape[1]
    groups = [range(s, min(s + PROJ_GROUP_COLS, len(handlers)))
              for s in range(0, len(handlers), PROJ_GROUP_COLS)]

    row_blocks = [slice(r, r + blk) for r in range(0, rows, blk)]

    def normed(rb):
        x = x_ref[rb, :]
        ms = jnp.mean(x * x, axis=-1, keepdims=True)
        return (x * lax.rsqrt(ms + EPS) * g_ref[...]).astype(BF16)

    hs = [normed(rb) for rb in row_blocks]
    items = [(r, cols) for cols in groups for r in range(len(row_blocks))]

    def sums_of_squares(slab, cols):
        needs = [handlers[c][0] not in (emit_va, emit_vb) for c in cols]
        ss, n = [None] * len(cols), 0
        while n < len(cols):
            width = 2 if (needs[n] and n + 1 < len(cols) and needs[n + 1]) else 1
            if needs[n]:
                part = slab[:, n * LANES:(n + width) * LANES]
                tot = _dot((part * part).astype(BF16), seg_pair if width == 2 else seg_one)
                for w in range(width):
                    ss[n + w] = tot[:, w * LANES:(w + 1) * LANES]
            n += width
        return ss

    def project(item):
        r, cols = item
        return _dot(hs[r], w_ref[:, cols.start * LANES:cols.stop * LANES])

    slab_next = project(items[0])
    for pos, (r, cols) in enumerate(items):
        slab = slab_next
        if pos + 1 < len(items):
            slab_next = project(items[pos + 1])
        ss = sums_of_squares(slab, cols)
        for n, c in enumerate(cols):
            fn, j = handlers[c]
            fn(j, slab[:, n * LANES:(n + 1) * LANES], ss[n], row_blocks[r])


def _project(x3, g, w_bf16, tables, gains, seg, rows_per_step):
    bsz, n_rows, d_model = x3.shape
    in_width = w_bf16.shape[1]
    t = rows_per_step
    tab_blk = pl.BlockSpec((t, LANES), lambda b, i: (i, 0))
    const = lambda shape: pl.BlockSpec(shape, lambda b, i: (0, 0))
    out_specs = [
        pl.BlockSpec((None, A_HEADS, LANES, t), lambda b, i: (b, 0, 0, i)),
        pl.BlockSpec((None, t, LANES), lambda b, i: (b, i, 0)),
        pl.BlockSpec((None, LANES, t), lambda b, i: (b, 0, i)),
        pl.BlockSpec((None, B_HEADS, LANES, t), lambda b, i: (b, 0, 0, i)),
        pl.BlockSpec((None, t, B_WIDTH), lambda b, i: (b, i, 0)),
        pl.BlockSpec((None, B_WIDTH, t), lambda b, i: (b, 0, i)),
    ]
    out_shapes = [(bsz, A_HEADS, LANES, n_rows), (bsz, n_rows, LANES), (bsz, LANES, n_rows),
                  (bsz, B_HEADS, LANES, n_rows), (bsz, n_rows, B_WIDTH), (bsz, B_WIDTH, n_rows)]
    return pl.pallas_call(
        _proj_kernel,
        grid=(bsz, n_rows // t),
        in_specs=[pl.BlockSpec((None, t, d_model), lambda b, i: (b, i, 0)),
                  const((1, d_model)), const((d_model, in_width)),
                  tab_blk, tab_blk, tab_blk, tab_blk, const(gains.shape), const(seg.shape)],
        out_specs=out_specs,
        out_shape=[jax.ShapeDtypeStruct(s, BF16) for s in out_shapes],
        compiler_params=pltpu.CompilerParams(
            dimension_semantics=("arbitrary", "arbitrary"), vmem_limit_bytes=VMEM_LIMIT_BYTES),
        name="proj",
    )(x3, g, w_bf16, *tables, gains, seg)


def _attend_t(groups, bias_ref, running_max, finish, lookahead=1, kv_chunk=KV_CHUNK,
              value_row_split=None):
    streams = [s for group in groups for s in group]
    n_chunks = streams[0][1].shape[0] // kv_chunk
    items, group_end, base = [], {}, 0
    for g, group in enumerate(groups):
        ids = range(base, base + len(group))
        items += [(i, None) for i in ids] + [(i, c) for c in range(n_chunks) for i in ids]
        group_end[len(items) - 1] = (g, ids)
        base += len(group)

    def scores(item):
        i, c = item
        qt, k_ref, _, km, _ = streams[i]
        if c is None:
            bias = jnp.concatenate([bias_ref[...]] * (qt.shape[1] // LANES), axis=1)
            return _dot(km, qt) + bias
        return _dot(k_ref[c * kv_chunk:(c + 1) * kv_chunk, :], qt)

    def values(vt, p):
        step = value_row_split or vt.shape[0]
        outs = [_dot(vt[r:r + step, :], p) for r in range(0, vt.shape[0], step)]
        return outs[0] if len(outs) == 1 else jnp.concatenate(outs, axis=0)

    def update(i, c, s, state):
        _, _, vt_ref, _, vmt = streams[i]
        vt = vmt if c is None else vt_ref[:, c * kv_chunk:(c + 1) * kv_chunk]
        if not running_max:
            p = jnp.exp2(s)
            l, acc = jnp.sum(p, axis=0, keepdims=True), values(vt, p.astype(BF16))
            if state is not None:
                l, acc = state[1] + l, state[2] + acc
            return (None, l, acc)
        s_max = jnp.max(s, axis=0, keepdims=True)
        if state is None:
            p = jnp.exp2(s - s_max)
            return (s_max, jnp.sum(p, axis=0, keepdims=True), values(vt, p.astype(BF16)))
        m, l, acc = state
        m_new = jnp.maximum(m, s_max)
        alpha = jnp.exp2(m - m_new)
        p = jnp.exp2(s - m_new)
        return (m_new, alpha * l + jnp.sum(p, axis=0, keepdims=True),
                alpha * acc + values(vt, p.astype(BF16)))

    state = [None] * len(streams)
    pending = [scores(item) for item in items[:lookahead]]
    for pos, (i, c) in enumerate(items):
        s = pending.pop(0)
        if pos + lookahead < len(items):
            pending.append(scores(items[pos + lookahead]))
        state[i] = update(i, c, s, state[i])
        if pos in group_end:
            g, ids = group_end[pos]
            finish(g, [state[j][2] / state[j][1] for j in ids])


def _attn_a_kernel(q_ref, k_ref, vt_ref, km_ref, vmt_ref, bias_ref, o_ref, *, running_max):
    tq = A_Q_COLS
    km, vmt = km_ref[...], vmt_ref[...]
    groups = []
    for t in range(q_ref.shape[2] // tq):
        group = []
        for head in range(A_HEADS):
            rows = slice((head // A_GROUP) * HEAD_DIM, (head // A_GROUP + 1) * HEAD_DIM)
            group.append((q_ref[head, :, t * tq:(t + 1) * tq], k_ref, vt_ref.at[rows, :], km, vmt[rows, :]))
        groups.append(group)

    def finish(t, outs):
        o_ref[t * tq:(t + 1) * tq, :] = jnp.concatenate(outs, axis=0).T.astype(o_ref.dtype)

    _attend_t(groups, bias_ref, running_max, finish, lookahead=A_LOOKAHEAD, kv_chunk=A_KV_CHUNK)


def _attn_b_kernel(q_ref, k_ref, vt_ref, km_ref, vmt_ref, bias_ref, lam_ref, subln_ref, o_ref, *,
                   running_max):
    tq = B_Q_COLS
    parts = tq // MXU_TILE
    first = _row_index((LANES, MXU_TILE)) < HEAD_DIM
    zero = jnp.zeros((LANES, MXU_TILE), BF16)
    groups = []
    for t in range(q_ref.shape[2] // tq):
        group = []
        for h in range(B_HEADS_PER_STEP):
            cols = slice(h * LANES, (h + 1) * LANES)
            kv = (k_ref.at[:, cols], vt_ref.at[cols, :], km_ref[:, cols], vmt_ref[cols, :])
            for sub in range(2):
                for part in range(parts):
                    start = t * tq + part * MXU_TILE
                    q = q_ref[h, :, start:start + MXU_TILE]
                    qt = jnp.where(first, q, zero) if sub == 0 else jnp.where(first, zero, q)
                    group.append((qt,) + kv)
        groups.append(group)
    lam_p = lam_ref[...]
    lam = (jnp.exp(jnp.sum(lam_p[0:1] * lam_p[1:2], axis=-1, keepdims=True))
           - jnp.exp(jnp.sum(lam_p[2:3] * lam_p[3:4], axis=-1, keepdims=True)) + LAM_INIT)

    def finish(t, outs):
        ys = []
        for h in range(B_HEADS_PER_STEP):
            o1 = jnp.concatenate(outs[(2 * h) * parts:(2 * h + 1) * parts], axis=1)
            o2 = jnp.concatenate(outs[(2 * h + 1) * parts:(2 * h + 2) * parts], axis=1)
            d = o1 - lam * o2
            ms = jnp.mean(d * d, axis=0, keepdims=True)
            ys.append(d * lax.rsqrt(ms + EPS) * subln_ref[...] * (1.0 - LAM_INIT))
        o_ref[t * tq:(t + 1) * tq, :] = jnp.concatenate(ys, axis=0).T.astype(o_ref.dtype)

    _attend_t(groups, bias_ref, running_max, finish, lookahead=B_LOOKAHEAD, kv_chunk=B_KV_CHUNK,
              value_row_split=HEAD_DIM)


def _attention_a(qat, ka, vat, kam, vamt, bias, running_max):
    bsz, _, _, seq = qat.shape
    const = lambda shape: pl.BlockSpec(shape, lambda b, i: (0,) * len(shape))
    return pl.pallas_call(
        functools.partial(_attn_a_kernel, running_max=running_max),
        grid=(bsz, seq // A_STEP_COLS),
        in_specs=[pl.BlockSpec((None, A_HEADS, LANES, A_STEP_COLS), lambda b, i: (b, 0, 0, i)),
                  pl.BlockSpec((None, seq, LANES), lambda b, i: (b, 0, 0)),
                  pl.BlockSpec((None, LANES, seq), lambda b, i: (b, 0, 0)),
                  const(kam.shape), const(vamt.shape), const(bias.shape)],
        out_specs=pl.BlockSpec((None, A_STEP_COLS, A_WIDTH), lambda b, i: (b, i, 0)),
        out_shape=jax.ShapeDtypeStruct((bsz, seq, A_WIDTH), BF16),
        compiler_params=pltpu.CompilerParams(
            dimension_semantics=("arbitrary", "arbitrary"), vmem_limit_bytes=VMEM_LIMIT_BYTES),
        name="attn_a",
    )(qat, ka, vat, kam, vamt, bias)


def _attention_b(qbt, kb, vbt, kbm, vbmt, bias, lam_params, subln_col, running_max):
    bsz, _, _, seq = qbt.shape
    hp = B_HEADS_PER_STEP
    width = hp * LANES
    const = lambda shape: pl.BlockSpec(shape, lambda b, h, i: (0,) * len(shape))
    return pl.pallas_call(
        functools.partial(_attn_b_kernel, running_max=running_max),
        grid=(bsz, B_HEADS // hp, seq // B_STEP_COLS),
        in_specs=[pl.BlockSpec((None, hp, LANES, B_STEP_COLS), lambda b, h, i: (b, h, 0, i)),
                  pl.BlockSpec((None, seq, width), lambda b, h, i: (b, 0, h)),
                  pl.BlockSpec((None, width, seq), lambda b, h, i: (b, h, 0)),
                  pl.BlockSpec((LANES, width), lambda b, h, i: (0, h)),
                  pl.BlockSpec((width, LANES), lambda b, h, i: (h, 0)),
                  const(bias.shape), const(lam_params.shape), const(subln_col.shape)],
        out_specs=pl.BlockSpec((None, B_STEP_COLS, width), lambda b, h, i: (b, i, h)),
        out_shape=jax.ShapeDtypeStruct((bsz, seq, B_WIDTH), BF16),
        compiler_params=pltpu.CompilerParams(
            dimension_semantics=("arbitrary", "arbitrary", "arbitrary"),
            vmem_limit_bytes=VMEM_LIMIT_BYTES),
        name="attn_b",
    )(qbt, kb, vbt, kbm, vbmt, bias, lam_params, subln_col)


def _post_kernel(x_ref, oa_ref, ob_ref, wo_ref, g_ref, wg_ref, wu_ref, wd_ref, y_ref):
    woa_ref, wob_ref = wo_ref.at[0:A_WIDTH, :], wo_ref.at[A_WIDTH:A_WIDTH + B_WIDTH, :]
    blocks = [slice(r, r + POST_ROW_BLOCK) for r in range(0, x_ref.shape[0], POST_ROW_BLOCK)]
    x1 = [x_ref[rb, :] + _dot(oa_ref[rb, :], woa_ref[...]) + _dot(ob_ref[rb, :], wob_ref[...])
          for rb in blocks]

    def normed(v):
        ms = jnp.mean(v * v, axis=-1, keepdims=True)
        return (v * lax.rsqrt(ms + EPS) * g_ref[...]).astype(BF16)

    gate_up = []
    for v in x1:
        h = normed(v)
        gate_up.append((_dot(h, wg_ref[...]), _dot(h, wu_ref[...])))
    for rb, v, (gate, up) in zip(blocks, x1, gate_up):
        act = (gate * jax.nn.sigmoid(gate) * up).astype(BF16)
        y_ref[rb, :] = v + _dot(act, wd_ref[...])


def _post(x_rows, oa, ob, wo, g, wg, wu, wd):
    n_rows, d_model = x_rows.shape
    row_blk = lambda width: pl.BlockSpec((POST_ROWS, width), lambda i: (i, 0))
    resident = lambda a: pl.BlockSpec(a.shape, lambda i: (0, 0), pipeline_mode=pl.Buffered(1))
    return pl.pallas_call(
        _post_kernel,
        grid=(n_rows // POST_ROWS,),
        in_specs=[row_blk(d_model), row_blk(A_WIDTH), row_blk(B_WIDTH),
                  resident(wo), resident(g),
                  resident(wg), resident(wu), resident(wd)],
        out_specs=row_blk(d_model),
        out_shape=jax.ShapeDtypeStruct((n_rows, d_model), F32),
        compiler_params=pltpu.CompilerParams(
            dimension_semantics=("arbitrary",), vmem_limit_bytes=VMEM_LIMIT_BYTES),
        name="post",
    )(x_rows, oa, ob, wo, g, wg, wu, wd)


def _pair_tables(ang):
    c, s = jnp.cos(ang), jnp.sin(ang)
    cos64 = jnp.repeat(c, 2, axis=-1)
    sin64 = jnp.stack([-s, s], axis=-1).reshape(ang.shape[0], HEAD_DIM)
    return jnp.tile(cos64, (1, 2)), jnp.tile(sin64, (1, 2))


def _rope_tables(seq):
    half = HEAD_DIM // 2
    t = jnp.arange(seq)
    inv_a = ROPE_THETA ** (-jnp.arange(0, half, 2, dtype=F32) / half)
    r = (t // GRID_W).astype(F32)
    c = (t % GRID_W).astype(F32)
    ang_a = jnp.concatenate([r[:, None] * inv_a[None, :], c[:, None] * inv_a[None, :]], axis=-1)
    inv_b = ROPE_THETA ** (-jnp.arange(0, HEAD_DIM, 2, dtype=F32) / HEAD_DIM)
    pos = jnp.arange(N_META + seq, dtype=F32)
    ang_b = pos[:, None] * inv_b[None, :]
    seq_tabs = _pair_tables(ang_a) + _pair_tables(ang_b[N_META:])
    meta_tabs = _pair_tables(jnp.zeros((N_META, half), F32)) + _pair_tables(ang_b[:N_META])
    return seq_tabs, tuple(_pad_rows(t, META_ROWS) for t in meta_tabs)


def _pad_rows(a, rows):
    return jnp.pad(a, ((0, rows - a.shape[0]), (0, 0)))


def kernel(x, meta_tokens, attn_norm_g, w_in, a_q_norm_g, a_k_norm_g, b_q_norm_g, b_k_norm_g,
           b_lambda_q1, b_lambda_k1, b_lambda_q2, b_lambda_k2, b_subln_g, w_out,
           ffn_norm_g, w_gate, w_up, w_down):
    bsz, seq, d_model = x.shape
    assert w_in.shape[0] == 1, "single-layer block"
    assert seq % PROJ_ROWS == 0 and seq % KV_CHUNK == 0 and seq % A_STEP_COLS == 0 and seq % B_STEP_COLS == 0
    layer = 0

    seq_tabs, meta_tabs = _rope_tables(seq)
    gains = jnp.stack([jnp.tile(g[layer].astype(F32), 2)
                       for g in (a_q_norm_g, a_k_norm_g, b_q_norm_g, b_k_norm_g)])
    idx = jnp.arange(2 * LANES) // HEAD_DIM
    seg = (idx[:, None] == idx[None, :]).astype(BF16)
    w_in_b = w_in[layer].astype(BF16)
    g_attn = attn_norm_g[layer].astype(F32)[None, :]

    qat, ka, vat, qbt, kb, vbt = _project(x, g_attn, w_in_b, seq_tabs, gains, seg, PROJ_ROWS)
    meta3 = _pad_rows(meta_tokens.astype(F32), META_ROWS)[None]
    _, kam, vamt, _, kbm, vbmt = _project(meta3, g_attn, w_in_b, meta_tabs, gains, seg, META_ROWS)
    kam, vamt, kbm, vbmt = kam[0], vamt[0], kbm[0], vbmt[0]
    bias = jnp.where(jnp.arange(META_ROWS) < N_META, 0.0, MASK_BIAS).astype(F32)
    bias = jnp.broadcast_to(bias[:, None], (META_ROWS, LANES))

    lam_params = jnp.stack([jnp.pad(p[layer].astype(F32), (0, LANES - HEAD_DIM))
                            for p in (b_lambda_q1, b_lambda_k1, b_lambda_q2, b_lambda_k2)])
    subln_col = b_subln_g[layer].astype(F32)[:, None]

    def attend(running_max):
        return (_attention_a(qat, ka, vat, kam, vamt, bias, running_max),
                _attention_b(qbt, kb, vbt, kbm, vbmt, bias, lam_params, subln_col, running_max))

    amax = lambda g: jnp.max(jnp.abs(g[layer].astype(F32)))
    score_bound = HEAD_DIM * Q_PRESCALE * jnp.maximum(amax(a_q_norm_g) * amax(a_k_norm_g),
                                                      amax(b_q_norm_g) * amax(b_k_norm_g))
    out_a, out_b = lax.cond(score_bound <= UNSHIFTED_SCORE_LIMIT,
                            lambda: attend(False), lambda: attend(True))

    w_out_b = w_out[layer].astype(BF16)
    y = _post(x.reshape(bsz * seq, d_model), out_a.reshape(bsz * seq, A_WIDTH),
              out_b.reshape(bsz * seq, B_WIDTH), w_out_b,
              ffn_norm_g[layer].astype(F32)[None, :],
              w_gate[layer].astype(BF16), w_up[layer].astype(BF16), w_down[layer].astype(BF16))
    return y.reshape(bsz, seq, d_model)
```

```python
import functools
import math

import jax
import jax.numpy as jnp
from jax import lax
from jax.experimental import pallas as pl
from jax.experimental.pallas import tpu as pltpu

N_META = 16
GRID_W = 64
HEAD_DIM = 64
ROPE_THETA = 10000.0
EPS = 1e-6
A_HEADS = 8
A_KV_HEADS = 2
A_GROUP = A_HEADS // A_KV_HEADS
A_WIDTH = A_HEADS * HEAD_DIM
A_KV_WIDTH = A_KV_HEADS * HEAD_DIM
B_HEADS = 4
B_VDIM = 2 * HEAD_DIM
B_WIDTH = B_HEADS * B_VDIM
LAM_INIT = 0.8 - 0.6 * math.exp(-0.3 * 0)

LANES = 128
VMEM_LIMIT_BYTES = 56 * 1024 * 1024

Q_PRESCALE = (HEAD_DIM ** -0.5) * math.log2(math.e)
UNSHIFTED_SCORE_LIMIT = 60.0

PROJ_ROWS = 1024
PROJ_ROW_BLOCK = 256
PROJ_GROUP_COLS = 4
META_ROWS = LANES
MXU_TILE = 256
Q_TILES_PER_STEP = 2
A_Q_COLS = MXU_TILE
A_KV_CHUNK = MXU_TILE
A_LOOKAHEAD = 8
B_Q_COLS = 512
B_HEADS_PER_STEP = 2
B_KV_CHUNK = 512
B_LOOKAHEAD = 1
KV_CHUNK = 512
A_STEP_COLS = A_Q_COLS * Q_TILES_PER_STEP
B_STEP_COLS = B_Q_COLS * Q_TILES_PER_STEP
POST_ROWS = 512
POST_ROW_BLOCK = 256

BF16 = jnp.bfloat16
F32 = jnp.float32


def _dot(a, b):
    return jnp.dot(a, b, preferred_element_type=F32)


def _lane_index(shape):
    return lax.broadcasted_iota(jnp.int32, shape, len(shape) - 1)


def _row_index(shape):
    return lax.broadcasted_iota(jnp.int32, shape, 0)


def _proj_kernel(x_ref, g_ref, w_ref, cos_a_ref, sin_a_ref, cos_b_ref, sin_b_ref,
                 gains_ref, seg_ref,
                 qa_ref, ka_ref, va_ref, qb_ref, kb_ref, vb_ref):
    rows = x_ref.shape[0]
    blk = min(rows, PROJ_ROW_BLOCK)
    lane = _lane_index((blk, LANES))
    even = (lane % 2) == 0
    low_half = lane < HEAD_DIM
    seg_pair = seg_ref[...]
    seg_one = seg_ref[0:LANES, 0:LANES]
    g_aq = gains_ref[0:1, :] * Q_PRESCALE
    g_ak = gains_ref[1:2, :]
    g_bq = gains_ref[2:3, :] * Q_PRESCALE
    g_bk = gains_ref[3:4, :]

    def norm_rope(col, ss, gain, cos_ref, sin_ref, rb):
        y = col * lax.rsqrt(ss * (1.0 / HEAD_DIM) + EPS) * gain
        swapped = jnp.where(even, pltpu.roll(y, LANES - 1, 1), pltpu.roll(y, 1, 1))
        return y * cos_ref[rb, :] + swapped * sin_ref[rb, :]

    def emit_qa(j, col, ss, rb):
        col = norm_rope(col, ss, g_aq, cos_a_ref, sin_a_ref, rb)
        flipped = pltpu.roll(col, HEAD_DIM, 1)
        for half in range(2):
            head = 2 * j + half
            kv = head // A_GROUP
            src = col if half == kv else flipped
            keep = low_half if kv == 0 else jnp.logical_not(low_half)
            qa_ref[head, :, rb] = jnp.where(keep, src, 0.0).T.astype(BF16)

    def emit_ka(j, col, ss, rb):
        ka_ref[rb, :] = norm_rope(col, ss, g_ak, cos_a_ref, sin_a_ref, rb).astype(BF16)

    def emit_va(j, col, ss, rb):
        va_ref[:, rb] = col.T.astype(BF16)

    def emit_qb(j, col, ss, rb):
        qb_ref[j, :, rb] = norm_rope(col, ss, g_bq, cos_b_ref, sin_b_ref, rb).T.astype(BF16)

    def emit_kb(j, col, ss, rb):
        kb_ref[rb, j * LANES:(j + 1) * LANES] = norm_rope(col, ss, g_bk, cos_b_ref, sin_b_ref, rb).astype(BF16)

    def emit_vb(j, col, ss, rb):
        vb_ref[j * LANES:(j + 1) * LANES, rb] = col.T.astype(BF16)

    handlers = ([(emit_qa, j) for j in range(A_WIDTH // LANES)] + [(emit_ka, 0), (emit_va, 0)]
                + [(emit_qb, j) for j in range(B_HEADS)] + [(emit_kb, j) for j in range(B_HEADS)]
                + [(emit_vb, j) for j in range(B_HEADS)])
    assert len(handlers) * LANES == w_ref.shape[1]
    groups = [range(s, min(s + PROJ_GROUP_COLS, len(handlers)))
              for s in range(0, len(handlers), PROJ_GROUP_COLS)]

    row_blocks = [slice(r, r + blk) for r in range(0, rows, blk)]

    def normed(rb):
        x = x_ref[rb, :]
        ms = jnp.mean(x * x, axis=-1, keepdims=True)
        return (x * lax.rsqrt(ms + EPS) * g_ref[...]).astype(BF16)

    hs = [normed(rb) for rb in row_blocks]
    items = [(r, cols) for cols in groups for r in range(len(row_blocks))]

    def sums_of_squares(slab, cols):
        needs = [handlers[c][0] not in (emit_va, emit_vb) for c in cols]
        ss, n = [None] * len(cols), 0
        while n < len(cols):
            width = 2 if (needs[n] and n + 1 < len(cols) and needs[n + 1]) else 1
            if needs[n]:
                part = slab[:, n * LANES:(n + width) * LANES]
                tot = _dot((part * part).astype(BF16), seg_pair if width == 2 else seg_one)
                for w in range(width):
                    ss[n + w] = tot[:, w * LANES:(w + 1) * LANES]
            n += width
        return ss

    def project(item):
        r, cols = item
        return _dot(hs[r], w_ref[:, cols.start * LANES:cols.stop * LANES])

    slab_next = project(items[0])
    for pos, (r, cols) in enumerate(items):
        slab = slab_next
        if pos + 1 < len(items):
            slab_next = project(items[pos + 1])
        ss = sums_of_squares(slab, cols)
        for n, c in enumerate(cols):
            fn, j = handlers[c]
            fn(j, slab[:, n * LANES:(n + 1) * LANES], ss[n], row_blocks[r])


def _project(x3, g, w_bf16, tables, gains, seg, rows_per_step):
    bsz, n_rows, d_model = x3.shape
    in_width = w_bf16.shape[1]
    t = rows_per_step
    tab_blk = pl.BlockSpec((t, LANES), lambda b, i: (i, 0))
    const = lambda shape: pl.BlockSpec(shape, lambda b, i: (0, 0))
    out_specs = [
        pl.BlockSpec((None, A_HEADS, LANES, t), lambda b, i: (b, 0, 0, i)),
        pl.BlockSpec((None, t, LANES), lambda b, i: (b, i, 0)),
        pl.BlockSpec((None, LANES, t), lambda b, i: (b, 0, i)),
        pl.BlockSpec((None, B_HEADS, LANES, t), lambda b, i: (b, 0, 0, i)),
        pl.BlockSpec((None, t, B_WIDTH), lambda b, i: (b, i, 0)),
        pl.BlockSpec((None, B_WIDTH, t), lambda b, i: (b, 0, i)),
    ]
    out_shapes = [(bsz, A_HEADS, LANES, n_rows), (bsz, n_rows, LANES), (bsz, LANES, n_rows),
                  (bsz, B_HEADS, LANES, n_rows), (bsz, n_rows, B_WIDTH), (bsz, B_WIDTH, n_rows)]
    return pl.pallas_call(
        _proj_kernel,
        grid=(bsz, n_rows // t),
        in_specs=[pl.BlockSpec((None, t, d_model), lambda b, i: (b, i, 0)),
                  const((1, d_model)), const((d_model, in_width)),
                  tab_blk, tab_blk, tab_blk, tab_blk, const(gains.shape), const(seg.shape)],
        out_specs=out_specs,
        out_shape=[jax.ShapeDtypeStruct(s, BF16) for s in out_shapes],
        compiler_params=pltpu.CompilerParams(
            dimension_semantics=("arbitrary", "arbitrary"), vmem_limit_bytes=VMEM_LIMIT_BYTES),
        name="proj",
    )(x3, g, w_bf16, *tables, gains, seg)


def _attend_t(groups, running_max, finish, lookahead=1, kv_chunk=KV_CHUNK):
    streams = [s for group in groups for s in group]
    n_chunks = streams[0][1].shape[0] // kv_chunk
    items, group_end, base = [], {}, 0
    for g, group in enumerate(groups):
        ids = range(base, base + len(group))
        items += [(i, None) for i in ids] + [(i, c) for c in range(n_chunks) for i in ids]
        group_end[len(items) - 1] = (g, ids)
        base += len(group)

    def scores(item):
        i, c = item
        qt, k_ref, _, km, _ = streams[i]
        if c is None:
            return _dot(km, qt)
        return _dot(k_ref[c * kv_chunk:(c + 1) * kv_chunk, :], qt)

    def weighted_values(vt, p, c):
        p = p.astype(BF16)
        if c is None:
            p = jnp.concatenate([p, jnp.zeros((vt.shape[1] - p.shape[0], p.shape[1]), BF16)], axis=0)
        return _dot(vt, p)

    def update(i, c, s, state):
        _, _, vt_ref, _, vmt = streams[i]
        vt = vmt if c is None else vt_ref[:, c * kv_chunk:(c + 1) * kv_chunk]
        if not running_max:
            p = jnp.exp2(s)
            l, acc = jnp.sum(p, axis=0, keepdims=True), weighted_values(vt, p, c)
            if state is not None:
                l, acc = state[1] + l, state[2] + acc
            return (None, l, acc)
        s_max = jnp.max(s, axis=0, keepdims=True)
        if state is None:
            p = jnp.exp2(s - s_max)
            return (s_max, jnp.sum(p, axis=0, keepdims=True), weighted_values(vt, p, c))
        m, l, acc = state
        m_new = jnp.maximum(m, s_max)
        alpha = jnp.exp2(m - m_new)
        p = jnp.exp2(s - m_new)
        return (m_new, alpha * l + jnp.sum(p, axis=0, keepdims=True),
                alpha * acc + weighted_values(vt, p, c))

    state = [None] * len(streams)
    pending = [scores(item) for item in items[:lookahead]]
    for pos, (i, c) in enumerate(items):
        s = pending.pop(0)
        if pos + lookahead < len(items):
            pending.append(scores(items[pos + lookahead]))
        state[i] = update(i, c, s, state[i])
        if pos in group_end:
            g, ids = group_end[pos]
            finish(g, [state[j][2] / state[j][1] for j in ids])


def _attn_a_kernel(q_ref, k_ref, vt_ref, km_ref, vmt_ref, o_ref, *, running_max):
    tq = A_Q_COLS
    km, vmt = km_ref[...], vmt_ref[...]
    groups = []
    for t in range(q_ref.shape[2] // tq):
        group = []
        for head in range(A_HEADS):
            rows = slice((head // A_GROUP) * HEAD_DIM, (head // A_GROUP + 1) * HEAD_DIM)
            group.append((q_ref[head, :, t * tq:(t + 1) * tq], k_ref, vt_ref.at[rows, :], km, vmt[rows, :]))
        groups.append(group)

    def finish(t, outs):
        o_ref[t * tq:(t + 1) * tq, :] = jnp.concatenate(outs, axis=0).T.astype(o_ref.dtype)

    _attend_t(groups, running_max, finish, lookahead=A_LOOKAHEAD, kv_chunk=A_KV_CHUNK)


def _attn_b_kernel(q_ref, k_ref, vt_ref, km_ref, vmt_ref, lam_ref, subln_ref, o_ref, *,
                   running_max):
    tq = B_Q_COLS
    first = _row_index((LANES, tq)) < HEAD_DIM
    zero = jnp.zeros((LANES, tq), BF16)
    groups = []
    for t in range(q_ref.shape[2] // tq):
        group = []
        for h in range(B_HEADS_PER_STEP):
            q = q_ref[h, :, t * tq:(t + 1) * tq]
            cols = slice(h * LANES, (h + 1) * LANES)
            qt = jnp.concatenate([jnp.where(first, q, zero), jnp.where(first, zero, q)], axis=1)
            group.append((qt, k_ref.at[:, cols], vt_ref.at[cols, :], km_ref[:, cols], vmt_ref[cols, :]))
        groups.append(group)
    lam_p = lam_ref[...]
    lam = (jnp.exp(jnp.sum(lam_p[0:1] * lam_p[1:2], axis=-1, keepdims=True))
           - jnp.exp(jnp.sum(lam_p[2:3] * lam_p[3:4], axis=-1, keepdims=True)) + LAM_INIT)

    def finish(t, outs):
        ys = []
        for o in outs:
            d = o[:, :tq] - lam * o[:, tq:]
            ms = jnp.mean(d * d, axis=0, keepdims=True)
            ys.append(d * lax.rsqrt(ms + EPS) * subln_ref[...] * (1.0 - LAM_INIT))
        o_ref[t * tq:(t + 1) * tq, :] = jnp.concatenate(ys, axis=0).T.astype(o_ref.dtype)

    _attend_t(groups, running_max, finish, lookahead=B_LOOKAHEAD, kv_chunk=B_KV_CHUNK)


def _attention_a(qat, ka, vat, kam, vamt, running_max):
    bsz, _, _, seq = qat.shape
    const = lambda shape: pl.BlockSpec(shape, lambda b, i: (0,) * len(shape))
    return pl.pallas_call(
        functools.partial(_attn_a_kernel, running_max=running_max),
        grid=(bsz, seq // A_STEP_COLS),
        in_specs=[pl.BlockSpec((None, A_HEADS, LANES, A_STEP_COLS), lambda b, i: (b, 0, 0, i)),
                  pl.BlockSpec((None, seq, LANES), lambda b, i: (b, 0, 0)),
                  pl.BlockSpec((None, LANES, seq), lambda b, i: (b, 0, 0)),
                  const(kam.shape), const(vamt.shape)],
        out_specs=pl.BlockSpec((None, A_STEP_COLS, A_WIDTH), lambda b, i: (b, i, 0)),
        out_shape=jax.ShapeDtypeStruct((bsz, seq, A_WIDTH), BF16),
        compiler_params=pltpu.CompilerParams(
            dimension_semantics=("arbitrary", "arbitrary"), vmem_limit_bytes=VMEM_LIMIT_BYTES),
        name="attn_a",
    )(qat, ka, vat, kam, vamt)


def _attention_b(qbt, kb, vbt, kbm, vbmt, lam_params, subln_col, running_max):
    bsz, _, _, seq = qbt.shape
    hp = B_HEADS_PER_STEP
    width = hp * LANES
    const = lambda shape: pl.BlockSpec(shape, lambda b, h, i: (0,) * len(shape))
    return pl.pallas_call(
        functools.partial(_attn_b_kernel, running_max=running_max),
        grid=(bsz, B_HEADS // hp, seq // B_STEP_COLS),
        in_specs=[pl.BlockSpec((None, hp, LANES, B_STEP_COLS), lambda b, h, i: (b, h, 0, i)),
                  pl.BlockSpec((None, seq, width), lambda b, h, i: (b, 0, h)),
                  pl.BlockSpec((None, width, seq), lambda b, h, i: (b, h, 0)),
                  pl.BlockSpec((N_META, width), lambda b, h, i: (0, h)),
                  pl.BlockSpec((width, LANES), lambda b, h, i: (h, 0)),
                  const(lam_params.shape), const(subln_col.shape)],
        out_specs=pl.BlockSpec((None, B_STEP_COLS, width), lambda b, h, i: (b, i, h)),
        out_shape=jax.ShapeDtypeStruct((bsz, seq, B_WIDTH), BF16),
        compiler_params=pltpu.CompilerParams(
            dimension_semantics=("arbitrary", "arbitrary", "arbitrary"),
            vmem_limit_bytes=VMEM_LIMIT_BYTES),
        name="attn_b",
    )(qbt, kb, vbt, kbm, vbmt, lam_params, subln_col)


def _post_kernel(x_ref, oa_ref, ob_ref, wo_ref, g_ref, wg_ref, wu_ref, wd_ref, y_ref):
    woa_ref, wob_ref = wo_ref.at[0:A_WIDTH, :], wo_ref.at[A_WIDTH:A_WIDTH + B_WIDTH, :]
    blocks = [slice(r, r + POST_ROW_BLOCK) for r in range(0, x_ref.shape[0], POST_ROW_BLOCK)]
    x1 = [x_ref[rb, :] + _dot(oa_ref[rb, :], woa_ref[...]) + _dot(ob_ref[rb, :], wob_ref[...])
          for rb in blocks]

    def normed(v):
        ms = jnp.mean(v * v, axis=-1, keepdims=True)
        return (v * lax.rsqrt(ms + EPS) * g_ref[...]).astype(BF16)

    gate_up = []
    for v in x1:
        h = normed(v)
        gate_up.append((_dot(h, wg_ref[...]), _dot(h, wu_ref[...])))
    for rb, v, (gate, up) in zip(blocks, x1, gate_up):
        act = (gate * jax.nn.sigmoid(gate) * up).astype(BF16)
        y_ref[rb, :] = v + _dot(act, wd_ref[...])


def _post(x_rows, oa, ob, wo, g, wg, wu, wd):
    n_rows, d_model = x_rows.shape
    row_blk = lambda width: pl.BlockSpec((POST_ROWS, width), lambda i: (i, 0))
    resident = lambda a: pl.BlockSpec(a.shape, lambda i: (0, 0), pipeline_mode=pl.Buffered(1))
    return pl.pallas_call(
        _post_kernel,
        grid=(n_rows // POST_ROWS,),
        in_specs=[row_blk(d_model), row_blk(A_WIDTH), row_blk(B_WIDTH),
                  resident(wo), resident(g),
                  resident(wg), resident(wu), resident(wd)],
        out_specs=row_blk(d_model),
        out_shape=jax.ShapeDtypeStruct((n_rows, d_model), F32),
        compiler_params=pltpu.CompilerParams(
            dimension_semantics=("arbitrary",), vmem_limit_bytes=VMEM_LIMIT_BYTES),
        name="post",
    )(x_rows, oa, ob, wo, g, wg, wu, wd)


def _pair_tables(ang):
    c, s = jnp.cos(ang), jnp.sin(ang)
    cos64 = jnp.repeat(c, 2, axis=-1)
    sin64 = jnp.stack([-s, s], axis=-1).reshape(ang.shape[0], HEAD_DIM)
    return jnp.tile(cos64, (1, 2)), jnp.tile(sin64, (1, 2))


def _rope_tables(seq):
    half = HEAD_DIM // 2
    t = jnp.arange(seq)
    inv_a = ROPE_THETA ** (-jnp.arange(0, half, 2, dtype=F32) / half)
    r = (t // GRID_W).astype(F32)
    c = (t % GRID_W).astype(F32)
    ang_a = jnp.concatenate([r[:, None] * inv_a[None, :], c[:, None] * inv_a[None, :]], axis=-1)
    inv_b = ROPE_THETA ** (-jnp.arange(0, HEAD_DIM, 2, dtype=F32) / HEAD_DIM)
    pos = jnp.arange(N_META + seq, dtype=F32)
    ang_b = pos[:, None] * inv_b[None, :]
    seq_tabs = _pair_tables(ang_a) + _pair_tables(ang_b[N_META:])
    meta_tabs = _pair_tables(jnp.zeros((N_META, half), F32)) + _pair_tables(ang_b[:N_META])
    return seq_tabs, tuple(_pad_rows(t, META_ROWS) for t in meta_tabs)


def _pad_rows(a, rows):
    return jnp.pad(a, ((0, rows - a.shape[0]), (0, 0)))


def kernel(x, meta_tokens, attn_norm_g, w_in, a_q_norm_g, a_k_norm_g, b_q_norm_g, b_k_norm_g,
           b_lambda_q1, b_lambda_k1, b_lambda_q2, b_lambda_k2, b_subln_g, w_out,
           ffn_norm_g, w_gate, w_up, w_down):
    bsz, seq, d_model = x.shape
    assert w_in.shape[0] == 1, "single-layer block"
    assert seq % PROJ_ROWS == 0 and seq % KV_CHUNK == 0 and seq % A_STEP_COLS == 0 and seq % B_STEP_COLS == 0
    layer = 0

    seq_tabs, meta_tabs = _rope_tables(seq)
    gains = jnp.stack([jnp.tile(g[layer].astype(F32), 2)
                       for g in (a_q_norm_g, a_k_norm_g, b_q_norm_g, b_k_norm_g)])
    idx = jnp.arange(2 * LANES) // HEAD_DIM
    seg = (idx[:, None] == idx[None, :]).astype(BF16)
    w_in_b = w_in[layer].astype(BF16)
    g_attn = attn_norm_g[layer].astype(F32)[None, :]

    qat, ka, vat, qbt, kb, vbt = _project(x, g_attn, w_in_b, seq_tabs, gains, seg, PROJ_ROWS)
    meta3 = _pad_rows(meta_tokens.astype(F32), META_ROWS)[None]
    _, kam, vamt, _, kbm, vbmt = _project(meta3, g_attn, w_in_b, meta_tabs, gains, seg, META_ROWS)
    kam, vamt, kbm, vbmt = kam[0, :N_META], vamt[0], kbm[0, :N_META], vbmt[0]

    lam_params = jnp.stack([jnp.pad(p[layer].astype(F32), (0, LANES - HEAD_DIM))
                            for p in (b_lambda_q1, b_lambda_k1, b_lambda_q2, b_lambda_k2)])
    subln_col = b_subln_g[layer].astype(F32)[:, None]

    def attend(running_max):
        return (_attention_a(qat, ka, vat, kam, vamt, running_max),
                _attention_b(qbt, kb, vbt, kbm, vbmt, lam_params, subln_col, running_max))

    amax = lambda g: jnp.max(jnp.abs(g[layer].astype(F32)))
    score_bound = HEAD_DIM * Q_PRESCALE * jnp.maximum(amax(a_q_norm_g) * amax(a_k_norm_g),
                                                      amax(b_q_norm_g) * amax(b_k_norm_g))
    out_a, out_b = lax.cond(score_bound <= UNSHIFTED_SCORE_LIMIT,
                            lambda: attend(False), lambda: attend(True))

    w_out_b = w_out[layer].astype(BF16)
    y = _post(x.reshape(bsz * seq, d_model), out_a.reshape(bsz * seq, A_WIDTH),
              out_b.reshape(bsz * seq, B_WIDTH), w_out_b,
              ffn_norm_g[layer].astype(F32)[None, :],
              w_gate[layer].astype(BF16), w_up[layer].astype(BF16), w_down[layer].astype(BF16))
    return y.reshape(bsz, seq, d_model)
```

```python
import functools
import math

import jax
import jax.numpy as jnp
from jax import lax
from jax.experimental import pallas as pl
from jax.experimental.pallas import tpu as pltpu

N_META = 16
GRID_W = 64
HEAD_DIM = 64
ROPE_THETA = 10000.0
EPS = 1e-6
A_HEADS = 8
A_KV_HEADS = 2
A_GROUP = A_HEADS // A_KV_HEADS
A_WIDTH = A_HEADS * HEAD_DIM
A_KV_WIDTH = A_KV_HEADS * HEAD_DIM
B_HEADS = 4
B_VDIM = 2 * HEAD_DIM
B_WIDTH = B_HEADS * B_VDIM
LAM_INIT = 0.8 - 0.6 * math.exp(-0.3 * 0)

LANES = 128
VMEM_LIMIT_BYTES = 56 * 1024 * 1024

Q_PRESCALE = (HEAD_DIM ** -0.5) * math.log2(math.e)
MASK_BIAS = -1e30
UNSHIFTED_SCORE_LIMIT = 60.0

PROJ_ROWS = 1024
PROJ_ROW_BLOCK = 256
PROJ_GROUP_COLS = 4
META_ROWS = LANES
MXU_TILE = 256
Q_TILES_PER_STEP = 2
A_Q_COLS = MXU_TILE
A_KV_CHUNK = MXU_TILE
A_LOOKAHEAD = 8
B_Q_COLS = 512
B_HEADS_PER_STEP = 2
B_KV_CHUNK = 512
B_LOOKAHEAD = 1
KV_CHUNK = 512
A_STEP_COLS = A_Q_COLS * Q_TILES_PER_STEP
B_STEP_COLS = B_Q_COLS * Q_TILES_PER_STEP
POST_ROWS = 1024
POST_ROW_BLOCK = 256

BF16 = jnp.bfloat16
F32 = jnp.float32


def _dot(a, b):
    return jnp.dot(a, b, preferred_element_type=F32)


def _lane_index(shape):
    return lax.broadcasted_iota(jnp.int32, shape, len(shape) - 1)


def _row_index(shape):
    return lax.broadcasted_iota(jnp.int32, shape, 0)


def _proj_kernel(x_ref, g_ref, w_ref, cos_a_ref, sin_a_ref, cos_b_ref, sin_b_ref,
                 gains_ref, seg_ref,
                 qa_ref, ka_ref, va_ref, qb_ref, kb_ref, vb_ref):
    rows = x_ref.shape[0]
    blk = min(rows, PROJ_ROW_BLOCK)
    lane = _lane_index((blk, LANES))
    even = (lane % 2) == 0
    low_half = lane < HEAD_DIM
    seg_pair = seg_ref[...]
    seg_one = seg_ref[0:LANES, 0:LANES]
    g_aq = gains_ref[0:1, :] * Q_PRESCALE
    g_ak = gains_ref[1:2, :]
    g_bq = gains_ref[2:3, :] * Q_PRESCALE
    g_bk = gains_ref[3:4, :]

    def norm_rope(col, ss, gain, cos_ref, sin_ref, rb):
        y = col * lax.rsqrt(ss * (1.0 / HEAD_DIM) + EPS) * gain
        swapped = jnp.where(even, pltpu.roll(y, LANES - 1, 1), pltpu.roll(y, 1, 1))
        return y * cos_ref[rb, :] + swapped * sin_ref[rb, :]

    def emit_qa(j, col, ss, rb):
        col = norm_rope(col, ss, g_aq, cos_a_ref, sin_a_ref, rb)
        flipped = pltpu.roll(col, HEAD_DIM, 1)
        for half in range(2):
            head = 2 * j + half
            kv = head // A_GROUP
            src = col if half == kv else flipped
            keep = low_half if kv == 0 else jnp.logical_not(low_half)
            qa_ref[head, :, rb] = jnp.where(keep, src, 0.0).T.astype(BF16)

    def emit_ka(j, col, ss, rb):
        ka_ref[rb, :] = norm_rope(col, ss, g_ak, cos_a_ref, sin_a_ref, rb).astype(BF16)

    def emit_va(j, col, ss, rb):
        va_ref[:, rb] = col.T.astype(BF16)

    def emit_qb(j, col, ss, rb):
        qb_ref[j, :, rb] = norm_rope(col, ss, g_bq, cos_b_ref, sin_b_ref, rb).T.astype(BF16)

    def emit_kb(j, col, ss, rb):
        kb_ref[rb, j * LANES:(j + 1) * LANES] = norm_rope(col, ss, g_bk, cos_b_ref, sin_b_ref, rb).astype(BF16)

    def emit_vb(j, col, ss, rb):
        vb_ref[j * LANES:(j + 1) * LANES, rb] = col.T.astype(BF16)

    handlers = ([(emit_qa, j) for j in range(A_WIDTH // LANES)] + [(emit_ka, 0), (emit_va, 0)]
                + [(emit_qb, j) for j in range(B_HEADS)] + [(emit_kb, j) for j in range(B_HEADS)]
                + [(emit_vb, j) for j in range(B_HEADS)])
    assert len(handlers) * LANES == w_ref.shape[1]
    groups = [range(s, min(s + PROJ_GROUP_COLS, len(handlers)))
              for s in range(0, len(handlers), PROJ_GROUP_COLS)]

    row_blocks = [slice(r, r + blk) for r in range(0, rows, blk)]

    def normed(rb):
        x = x_ref[rb, :]
        ms = jnp.mean(x * x, axis=-1, keepdims=True)
        return (x * lax.rsqrt(ms + EPS) * g_ref[...]).astype(BF16)

    hs = [normed(rb) for rb in row_blocks]
    items = [(r, cols) for cols in groups for r in range(len(row_blocks))]

    def sums_of_squares(slab, cols):
        needs = [handlers[c][0] not in (emit_va, emit_vb) for c in cols]
        ss, n = [None] * len(cols), 0
        while n < len(cols):
            width = 2 if (needs[n] and n + 1 < len(cols) and needs[n + 1]) else 1
            if needs[n]:
                part = slab[:, n * LANES:(n + width) * LANES]
                tot = _dot((part * part).astype(BF16), seg_pair if width == 2 else seg_one)
                for w in range(width):
                    ss[n + w] = tot[:, w * LANES:(w + 1) * LANES]
            n += width
        return ss

    def project(item):
        r, cols = item
        return _dot(hs[r], w_ref[:, cols.start * LANES:cols.stop * LANES])

    slab_next = project(items[0])
    for pos, (r, cols) in enumerate(items):
        slab = slab_next
        if pos + 1 < len(items):
            slab_next = project(items[pos + 1])
        ss = sums_of_squares(slab, cols)
        for n, c in enumerate(cols):
            fn, j = handlers[c]
            fn(j, slab[:, n * LANES:(n + 1) * LANES], ss[n], row_blocks[r])


def _project(x3, g, w_bf16, tables, gains, seg, rows_per_step):
    bsz, n_rows, d_model = x3.shape
    in_width = w_bf16.shape[1]
    t = rows_per_step
    tab_blk = pl.BlockSpec((t, LANES), lambda b, i: (i, 0))
    const = lambda shape: pl.BlockSpec(shape, lambda b, i: (0, 0))
    out_specs = [
        pl.BlockSpec((None, A_HEADS, LANES, t), lambda b, i: (b, 0, 0, i)),
        pl.BlockSpec((None, t, LANES), lambda b, i: (b, i, 0)),
        pl.BlockSpec((None, LANES, t), lambda b, i: (b, 0, i)),
        pl.BlockSpec((None, B_HEADS, LANES, t), lambda b, i: (b, 0, 0, i)),
        pl.BlockSpec((None, t, B_WIDTH), lambda b, i: (b, i, 0)),
        pl.BlockSpec((None, B_WIDTH, t), lambda b, i: (b, 0, i)),
    ]
    out_shapes = [(bsz, A_HEADS, LANES, n_rows), (bsz, n_rows, LANES), (bsz, LANES, n_rows),
                  (bsz, B_HEADS, LANES, n_rows), (bsz, n_rows, B_WIDTH), (bsz, B_WIDTH, n_rows)]
    return pl.pallas_call(
        _proj_kernel,
        grid=(bsz, n_rows // t),
        in_specs=[pl.BlockSpec((None, t, d_model), lambda b, i: (b, i, 0)),
                  const((1, d_model)), const((d_model, in_width)),
                  tab_blk, tab_blk, tab_blk, tab_blk, const(gains.shape), const(seg.shape)],
        out_specs=out_specs,
        out_shape=[jax.ShapeDtypeStruct(s, BF16) for s in out_shapes],
        compiler_params=pltpu.CompilerParams(
            dimension_semantics=("arbitrary", "arbitrary"), vmem_limit_bytes=VMEM_LIMIT_BYTES),
        name="proj",
    )(x3, g, w_bf16, *tables, gains, seg)


def _attend_t(groups, bias_ref, running_max, finish, lookahead=1, kv_chunk=KV_CHUNK):
    streams = [s for group in groups for s in group]
    n_chunks = streams[0][1].shape[0] // kv_chunk
    items, group_end, base = [], {}, 0
    for g, group in enumerate(groups):
        ids = range(base, base + len(group))
        items += [(i, None) for i in ids] + [(i, c) for c in range(n_chunks) for i in ids]
        group_end[len(items) - 1] = (g, ids)
        base += len(group)

    def scores(item):
        i, c = item
        qt, k_ref, _, km, _ = streams[i]
        if c is None:
            bias = jnp.concatenate([bias_ref[...]] * (qt.shape[1] // LANES), axis=1)
            return _dot(km, qt) + bias
        return _dot(k_ref[c * kv_chunk:(c + 1) * kv_chunk, :], qt)

    def update(i, c, s, state):
        _, _, vt_ref, _, vmt = streams[i]
        vt = vmt if c is None else vt_ref[:, c * kv_chunk:(c + 1) * kv_chunk]
        if not running_max:
            p = jnp.exp2(s)
            l, acc = jnp.sum(p, axis=0, keepdims=True), _dot(vt, p.astype(BF16))
            if state is not None:
                l, acc = state[1] + l, state[2] + acc
            return (None, l, acc)
        s_max = jnp.max(s, axis=0, keepdims=True)
        if state is None:
            p = jnp.exp2(s - s_max)
            return (s_max, jnp.sum(p, axis=0, keepdims=True), _dot(vt, p.astype(BF16)))
        m, l, acc = state
        m_new = jnp.maximum(m, s_max)
        alpha = jnp.exp2(m - m_new)
        p = jnp.exp2(s - m_new)
        return (m_new, alpha * l + jnp.sum(p, axis=0, keepdims=True),
                alpha * acc + _dot(vt, p.astype(BF16)))

    state = [None] * len(streams)
    pending = [scores(item) for item in items[:lookahead]]
    for pos, (i, c) in enumerate(items):
        s = pending.pop(0)
        if pos + lookahead < len(items):
            pending.append(scores(items[pos + lookahead]))
        state[i] = update(i, c, s, state[i])
        if pos in group_end:
            g, ids = group_end[pos]
            finish(g, [state[j][2] / state[j][1] for j in ids])


def _attn_a_kernel(q_ref, k_ref, vt_ref, km_ref, vmt_ref, bias_ref, o_ref, *, running_max):
    tq = A_Q_COLS
    km, vmt = km_ref[...], vmt_ref[...]
    groups = []
    for t in range(q_ref.shape[2] // tq):
        group = []
        for head in range(A_HEADS):
            rows = slice((head // A_GROUP) * HEAD_DIM, (head // A_GROUP + 1) * HEAD_DIM)
            group.append((q_ref[head, :, t * tq:(t + 1) * tq], k_ref, vt_ref.at[rows, :], km, vmt[rows, :]))
        groups.append(group)

    def finish(t, outs):
        o_ref[t * tq:(t + 1) * tq, :] = jnp.concatenate(outs, axis=0).T.astype(o_ref.dtype)

    _attend_t(groups, bias_ref, running_max, finish, lookahead=A_LOOKAHEAD, kv_chunk=A_KV_CHUNK)


def _attn_b_kernel(q_ref, k_ref, vt_ref, km_ref, vmt_ref, bias_ref, lam_ref, subln_ref, o_ref, *,
                   running_max):
    tq = B_Q_COLS
    first = _row_index((LANES, tq)) < HEAD_DIM
    zero = jnp.zeros((LANES, tq), BF16)
    groups = []
    for t in range(q_ref.shape[2] // tq):
        group = []
        for h in range(B_HEADS_PER_STEP):
            q = q_ref[h, :, t * tq:(t + 1) * tq]
            cols = slice(h * LANES, (h + 1) * LANES)
            qt = jnp.concatenate([jnp.where(first, q, zero), jnp.where(first, zero, q)], axis=1)
            group.append((qt, k_ref.at[:, cols], vt_ref.at[cols, :], km_ref[:, cols], vmt_ref[cols, :]))
        groups.append(group)
    lam_p = lam_ref[...]
    lam = (jnp.exp(jnp.sum(lam_p[0:1] * lam_p[1:2], axis=-1, keepdims=True))
           - jnp.exp(jnp.sum(lam_p[2:3] * lam_p[3:4], axis=-1, keepdims=True)) + LAM_INIT)

    def finish(t, outs):
        ys = []
        for o in outs:
            d = o[:, :tq] - lam * o[:, tq:]
            ms = jnp.mean(d * d, axis=0, keepdims=True)
            ys.append(d * lax.rsqrt(ms + EPS) * subln_ref[...] * (1.0 - LAM_INIT))
        o_ref[t * tq:(t + 1) * tq, :] = jnp.concatenate(ys, axis=0).T.astype(o_ref.dtype)

    _attend_t(groups, bias_ref, running_max, finish, lookahead=B_LOOKAHEAD, kv_chunk=B_KV_CHUNK)


def _attention_a(qat, ka, vat, kam, vamt, bias, running_max):
    bsz, _, _, seq = qat.shape
    const = lambda shape: pl.BlockSpec(shape, lambda b, i: (0,) * len(shape))
    return pl.pallas_call(
        functools.partial(_attn_a_kernel, running_max=running_max),
        grid=(bsz, seq // A_STEP_COLS),
        in_specs=[pl.BlockSpec((None, A_HEADS, LANES, A_STEP_COLS), lambda b, i: (b, 0, 0, i)),
                  pl.BlockSpec((None, seq, LANES), lambda b, i: (b, 0, 0)),
                  pl.BlockSpec((None, LANES, seq), lambda b, i: (b, 0, 0)),
                  const(kam.shape), const(vamt.shape), const(bias.shape)],
        out_specs=pl.BlockSpec((None, A_STEP_COLS, A_WIDTH), lambda b, i: (b, i, 0)),
        out_shape=jax.ShapeDtypeStruct((bsz, seq, A_WIDTH), BF16),
        compiler_params=pltpu.CompilerParams(
            dimension_semantics=("arbitrary", "arbitrary"), vmem_limit_bytes=VMEM_LIMIT_BYTES),
        name="attn_a",
    )(qat, ka, vat, kam, vamt, bias)


def _attention_b(qbt, kb, vbt, kbm, vbmt, bias, lam_params, subln_col, running_max):
    bsz, _, _, seq = qbt.shape
    hp = B_HEADS_PER_STEP
    width = hp * LANES
    const = lambda shape: pl.BlockSpec(shape, lambda b, h, i: (0,) * len(shape))
    return pl.pallas_call(
        functools.partial(_attn_b_kernel, running_max=running_max),
        grid=(bsz, B_HEADS // hp, seq // B_STEP_COLS),
        in_specs=[pl.BlockSpec((None, hp, LANES, B_STEP_COLS), lambda b, h, i: (b, h, 0, i)),
                  pl.BlockSpec((None, seq, width), lambda b, h, i: (b, 0, h)),
                  pl.BlockSpec((None, width, seq), lambda b, h, i: (b, h, 0)),
                  pl.BlockSpec((LANES, width), lambda b, h, i: (0, h)),
                  pl.BlockSpec((width, LANES), lambda b, h, i: (h, 0)),
                  const(bias.shape), const(lam_params.shape), const(subln_col.shape)],
        out_specs=pl.BlockSpec((None, B_STEP_COLS, width), lambda b, h, i: (b, i, h)),
        out_shape=jax.ShapeDtypeStruct((bsz, seq, B_WIDTH), BF16),
        compiler_params=pltpu.CompilerParams(
            dimension_semantics=("arbitrary", "arbitrary", "arbitrary"),
            vmem_limit_bytes=VMEM_LIMIT_BYTES),
        name="attn_b",
    )(qbt, kb, vbt, kbm, vbmt, bias, lam_params, subln_col)


def _post_kernel(x_ref, oa_ref, ob_ref, wo_ref, g_ref, wg_ref, wu_ref, wd_ref, y_ref):
    woa_ref, wob_ref = wo_ref.at[0:A_WIDTH, :], wo_ref.at[A_WIDTH:A_WIDTH + B_WIDTH, :]
    blocks = [slice(r, r + POST_ROW_BLOCK) for r in range(0, x_ref.shape[0], POST_ROW_BLOCK)]
    x1 = [x_ref[rb, :] + _dot(oa_ref[rb, :], woa_ref[...]) + _dot(ob_ref[rb, :], wob_ref[...])
          for rb in blocks]

    def normed(v):
        ms = jnp.mean(v * v, axis=-1, keepdims=True)
        return (v * lax.rsqrt(ms + EPS) * g_ref[...]).astype(BF16)

    gate_up = []
    for v in x1:
        h = normed(v)
        gate_up.append((_dot(h, wg_ref[...]), _dot(h, wu_ref[...])))
    for rb, v, (gate, up) in zip(blocks, x1, gate_up):
        act = (gate * jax.nn.sigmoid(gate) * up).astype(BF16)
        y_ref[rb, :] = v + _dot(act, wd_ref[...])


def _post(x_rows, oa, ob, wo, g, wg, wu, wd):
    n_rows, d_model = x_rows.shape
    row_blk = lambda width: pl.BlockSpec((POST_ROWS, width), lambda i: (i, 0))
    resident = lambda a: pl.BlockSpec(a.shape, lambda i: (0, 0), pipeline_mode=pl.Buffered(1))
    return pl.pallas_call(
        _post_kernel,
        grid=(n_rows // POST_ROWS,),
        in_specs=[row_blk(d_model), row_blk(A_WIDTH), row_blk(B_WIDTH),
                  resident(wo), resident(g),
                  resident(wg), resident(wu), resident(wd)],
        out_specs=row_blk(d_model),
        out_shape=jax.ShapeDtypeStruct((n_rows, d_model), F32),
        compiler_params=pltpu.CompilerParams(
            dimension_semantics=("arbitrary",), vmem_limit_bytes=VMEM_LIMIT_BYTES),
        name="post",
    )(x_rows, oa, ob, wo, g, wg, wu, wd)


def _pair_tables(ang):
    c, s = jnp.cos(ang), jnp.sin(ang)
    cos64 = jnp.repeat(c, 2, axis=-1)
    sin64 = jnp.stack([-s, s], axis=-1).reshape(ang.shape[0], HEAD_DIM)
    return jnp.tile(cos64, (1, 2)), jnp.tile(sin64, (1, 2))


def _rope_tables(seq):
    half = HEAD_DIM // 2
    t = jnp.arange(seq)
    inv_a = ROPE_THETA ** (-jnp.arange(0, half, 2, dtype=F32) / half)
    r = (t // GRID_W).astype(F32)
    c = (t % GRID_W).astype(F32)
    ang_a = jnp.concatenate([r[:, None] * inv_a[None, :], c[:, None] * inv_a[None, :]], axis=-1)
    inv_b = ROPE_THETA ** (-jnp.arange(0, HEAD_DIM, 2, dtype=F32) / HEAD_DIM)
    pos = jnp.arange(N_META + seq, dtype=F32)
    ang_b = pos[:, None] * inv_b[None, :]
    seq_tabs = _pair_tables(ang_a) + _pair_tables(ang_b[N_META:])
    meta_tabs = _pair_tables(jnp.zeros((N_META, half), F32)) + _pair_tables(ang_b[:N_META])
    return seq_tabs, tuple(_pad_rows(t, META_ROWS) for t in meta_tabs)


def _pad_rows(a, rows):
    return jnp.pad(a, ((0, rows - a.shape[0]), (0, 0)))


def kernel(x, meta_tokens, attn_norm_g, w_in, a_q_norm_g, a_k_norm_g, b_q_norm_g, b_k_norm_g,
           b_lambda_q1, b_lambda_k1, b_lambda_q2, b_lambda_k2, b_subln_g, w_out,
           ffn_norm_g, w_gate, w_up, w_down):
    bsz, seq, d_model = x.shape
    assert w_in.shape[0] == 1, "single-layer block"
    assert seq % PROJ_ROWS == 0 and seq % KV_CHUNK == 0 and seq % A_STEP_COLS == 0 and seq % B_STEP_COLS == 0
    layer = 0

    seq_tabs, meta_tabs = _rope_tables(seq)
    gains = jnp.stack([jnp.tile(g[layer].astype(F32), 2)
                       for g in (a_q_norm_g, a_k_norm_g, b_q_norm_g, b_k_norm_g)])
    idx = jnp.arange(2 * LANES) // HEAD_DIM
    seg = (idx[:, None] == idx[None, :]).astype(BF16)
    w_in_b = w_in[layer].astype(BF16)
    g_attn = attn_norm_g[layer].astype(F32)[None, :]

    qat, ka, vat, qbt, kb, vbt = _project(x, g_attn, w_in_b, seq_tabs, gains, seg, PROJ_ROWS)
    meta3 = _pad_rows(meta_tokens.astype(F32), META_ROWS)[None]
    _, kam, vamt, _, kbm, vbmt = _project(meta3, g_attn, w_in_b, meta_tabs, gains, seg, META_ROWS)
    kam, vamt, kbm, vbmt = kam[0], vamt[0], kbm[0], vbmt[0]
    bias = jnp.where(jnp.arange(META_ROWS) < N_META, 0.0, MASK_BIAS).astype(F32)
    bias = jnp.broadcast_to(bias[:, None], (META_ROWS, LANES))

    lam_params = jnp.stack([jnp.pad(p[layer].astype(F32), (0, LANES - HEAD_DIM))
                            for p in (b_lambda_q1, b_lambda_k1, b_lambda_q2, b_lambda_k2)])
    subln_col = b_subln_g[layer].astype(F32)[:, None]

    def attend(running_max):
        return (_attention_a(qat, ka, vat, kam, vamt, bias, running_max),
                _attention_b(qbt, kb, vbt, kbm, vbmt, bias, lam_params, subln_col, running_max))

    amax = lambda g: jnp.max(jnp.abs(g[layer].astype(F32)))
    score_bound = HEAD_DIM * Q_PRESCALE * jnp.maximum(amax(a_q_norm_g) * amax(a_k_norm_g),
                                                      amax(b_q_norm_g) * amax(b_k_norm_g))
    out_a, out_b = lax.cond(score_bound <= UNSHIFTED_SCORE_LIMIT,
                            lambda: attend(False), lambda: attend(True))

    w_out_b = w_out[layer].astype(BF16)
    y = _post(x.reshape(bsz * seq, d_model), out_a.reshape(bsz * seq, A_WIDTH),
              out_b.reshape(bsz * seq, B_WIDTH), w_out_b,
              ffn_norm_g[layer].astype(F32)[None, :],
              w_gate[layer].astype(BF16), w_up[layer].astype(BF16), w_down[layer].astype(BF16))
    return y.reshape(bsz, seq, d_model)
```

```python
import functools
import math

import jax
import jax.numpy as jnp
from jax import lax
from jax.experimental import pallas as pl
from jax.experimental.pallas import tpu as pltpu

N_META = 16
GRID_W = 64
HEAD_DIM = 64
ROPE_THETA = 10000.0
EPS = 1e-6
A_HEADS = 8
A_KV_HEADS = 2
A_GROUP = A_HEADS // A_KV_HEADS
A_WIDTH = A_HEADS * HEAD_DIM
A_KV_WIDTH = A_KV_HEADS * HEAD_DIM
B_HEADS = 4
B_VDIM = 2 * HEAD_DIM
B_WIDTH = B_HEADS * B_VDIM
LAM_INIT = 0.8 - 0.6 * math.exp(-0.3 * 0)

LANES = 128
VMEM_LIMIT_BYTES = 56 * 1024 * 1024

Q_PRESCALE = (HEAD_DIM ** -0.5) * math.log2(math.e)
MASK_BIAS = -1e30
UNSHIFTED_SCORE_LIMIT = 60.0

PROJ_ROWS = 1024
PROJ_ROW_BLOCK = 256
PROJ_GROUP_COLS = 4
META_ROWS = LANES
MXU_TILE = 256
Q_TILES_PER_STEP = 2
A_Q_COLS = MXU_TILE
A_KV_CHUNK = MXU_TILE
A_LOOKAHEAD = 8
B_Q_COLS = 512
B_HEADS_PER_STEP = 2
B_KV_CHUNK = 1024
B_LOOKAHEAD = 1
KV_CHUNK = 512
A_STEP_COLS = A_Q_COLS * Q_TILES_PER_STEP
B_STEP_COLS = B_Q_COLS * Q_TILES_PER_STEP
POST_ROWS = 1024
POST_ROW_BLOCK = 256

BF16 = jnp.bfloat16
F32 = jnp.float32


def _dot(a, b):
    return jnp.dot(a, b, preferred_element_type=F32)


def _lane_index(shape):
    return lax.broadcasted_iota(jnp.int32, shape, len(shape) - 1)


def _row_index(shape):
    return lax.broadcasted_iota(jnp.int32, shape, 0)


def _proj_kernel(x_ref, g_ref, w_ref, cos_a_ref, sin_a_ref, cos_b_ref, sin_b_ref,
                 gains_ref, seg_ref,
                 qa_ref, ka_ref, va_ref, qb_ref, kb_ref, vb_ref):
    rows = x_ref.shape[0]
    blk = min(rows, PROJ_ROW_BLOCK)
    lane = _lane_index((blk, LANES))
    even = (lane % 2) == 0
    low_half = lane < HEAD_DIM
    seg_pair = seg_ref[...]
    seg_one = seg_ref[0:LANES, 0:LANES]
    g_aq = gains_ref[0:1, :] * Q_PRESCALE
    g_ak = gains_ref[1:2, :]
    g_bq = gains_ref[2:3, :] * Q_PRESCALE
    g_bk = gains_ref[3:4, :]

    def norm_rope(col, ss, gain, cos_ref, sin_ref, rb):
        y = col * lax.rsqrt(ss * (1.0 / HEAD_DIM) + EPS) * gain
        swapped = jnp.where(even, pltpu.roll(y, LANES - 1, 1), pltpu.roll(y, 1, 1))
        return y * cos_ref[rb, :] + swapped * sin_ref[rb, :]

    def emit_qa(j, col, ss, rb):
        col = norm_rope(col, ss, g_aq, cos_a_ref, sin_a_ref, rb)
        flipped = pltpu.roll(col, HEAD_DIM, 1)
        for half in range(2):
            head = 2 * j + half
            kv = head // A_GROUP
            src = col if half == kv else flipped
            keep = low_half if kv == 0 else jnp.logical_not(low_half)
            qa_ref[head, :, rb] = jnp.where(keep, src, 0.0).T.astype(BF16)

    def emit_ka(j, col, ss, rb):
        ka_ref[rb, :] = norm_rope(col, ss, g_ak, cos_a_ref, sin_a_ref, rb).astype(BF16)

    def emit_va(j, col, ss, rb):
        va_ref[:, rb] = col.T.astype(BF16)

    def emit_qb(j, col, ss, rb):
        qb_ref[j, :, rb] = norm_rope(col, ss, g_bq, cos_b_ref, sin_b_ref, rb).T.astype(BF16)

    def emit_kb(j, col, ss, rb):
        kb_ref[rb, j * LANES:(j + 1) * LANES] = norm_rope(col, ss, g_bk, cos_b_ref, sin_b_ref, rb).astype(BF16)

    def emit_vb(j, col, ss, rb):
        vb_ref[j * LANES:(j + 1) * LANES, rb] = col.T.astype(BF16)

    handlers = ([(emit_qa, j) for j in range(A_WIDTH // LANES)] + [(emit_ka, 0), (emit_va, 0)]
                + [(emit_qb, j) for j in range(B_HEADS)] + [(emit_kb, j) for j in range(B_HEADS)]
                + [(emit_vb, j) for j in range(B_HEADS)])
    assert len(handlers) * LANES == w_ref.shape[1]
    groups = [range(s, min(s + PROJ_GROUP_COLS, len(handlers)))
              for s in range(0, len(handlers), PROJ_GROUP_COLS)]

    row_blocks = [slice(r, r + blk) for r in range(0, rows, blk)]

    def normed(rb):
        x = x_ref[rb, :]
        ms = jnp.mean(x * x, axis=-1, keepdims=True)
        return (x * lax.rsqrt(ms + EPS) * g_ref[...]).astype(BF16)

    hs = [normed(rb) for rb in row_blocks]
    items = [(r, cols) for cols in groups for r in range(len(row_blocks))]

    def sums_of_squares(slab, cols):
        needs = [handlers[c][0] not in (emit_va, emit_vb) for c in cols]
        ss, n = [None] * len(cols), 0
        while n < len(cols):
            width = 2 if (needs[n] and n + 1 < len(cols) and needs[n + 1]) else 1
            if needs[n]:
                part = slab[:, n * LANES:(n + width) * LANES]
                tot = _dot((part * part).astype(BF16), seg_pair if width == 2 else seg_one)
                for w in range(width):
                    ss[n + w] = tot[:, w * LANES:(w + 1) * LANES]
            n += width
        return ss

    def project(item):
        r, cols = item
        return _dot(hs[r], w_ref[:, cols.start * LANES:cols.stop * LANES])

    slab_next = project(items[0])
    for pos, (r, cols) in enumerate(items):
        slab = slab_next
        if pos + 1 < len(items):
            slab_next = project(items[pos + 1])
        ss = sums_of_squares(slab, cols)
        for n, c in enumerate(cols):
            fn, j = handlers[c]
            fn(j, slab[:, n * LANES:(n + 1) * LANES], ss[n], row_blocks[r])


def _project(x3, g, w_bf16, tables, gains, seg, rows_per_step):
    bsz, n_rows, d_model = x3.shape
    in_width = w_bf16.shape[1]
    t = rows_per_step
    tab_blk = pl.BlockSpec((t, LANES), lambda b, i: (i, 0))
    const = lambda shape: pl.BlockSpec(shape, lambda b, i: (0, 0))
    out_specs = [
        pl.BlockSpec((None, A_HEADS, LANES, t), lambda b, i: (b, 0, 0, i)),
        pl.BlockSpec((None, t, LANES), lambda b, i: (b, i, 0)),
        pl.BlockSpec((None, LANES, t), lambda b, i: (b, 0, i)),
        pl.BlockSpec((None, B_HEADS, LANES, t), lambda b, i: (b, 0, 0, i)),
        pl.BlockSpec((None, t, B_WIDTH), lambda b, i: (b, i, 0)),
        pl.BlockSpec((None, B_WIDTH, t), lambda b, i: (b, 0, i)),
    ]
    out_shapes = [(bsz, A_HEADS, LANES, n_rows), (bsz, n_rows, LANES), (bsz, LANES, n_rows),
                  (bsz, B_HEADS, LANES, n_rows), (bsz, n_rows, B_WIDTH), (bsz, B_WIDTH, n_rows)]
    return pl.pallas_call(
        _proj_kernel,
        grid=(bsz, n_rows // t),
        in_specs=[pl.BlockSpec((None, t, d_model), lambda b, i: (b, i, 0)),
                  const((1, d_model)), const((d_model, in_width)),
                  tab_blk, tab_blk, tab_blk, tab_blk, const(gains.shape), const(seg.shape)],
        out_specs=out_specs,
        out_shape=[jax.ShapeDtypeStruct(s, BF16) for s in out_shapes],
        compiler_params=pltpu.CompilerParams(
            dimension_semantics=("arbitrary", "arbitrary"), vmem_limit_bytes=VMEM_LIMIT_BYTES),
        name="proj",
    )(x3, g, w_bf16, *tables, gains, seg)


def _attend_t(groups, bias_ref, running_max, finish, lookahead=1, kv_chunk=KV_CHUNK):
    streams = [s for group in groups for s in group]
    n_chunks = streams[0][1].shape[0] // kv_chunk
    items, group_end, base = [], {}, 0
    for g, group in enumerate(groups):
        ids = range(base, base + len(group))
        items += [(i, None) for i in ids] + [(i, c) for c in range(n_chunks) for i in ids]
        group_end[len(items) - 1] = (g, ids)
        base += len(group)

    def scores(item):
        i, c = item
        qt, k_ref, _, km, _ = streams[i]
        if c is None:
            bias = jnp.concatenate([bias_ref[...]] * (qt.shape[1] // LANES), axis=1)
            return _dot(km, qt) + bias
        return _dot(k_ref[c * kv_chunk:(c + 1) * kv_chunk, :], qt)

    def update(i, c, s, state):
        _, _, vt_ref, _, vmt = streams[i]
        vt = vmt if c is None else vt_ref[:, c * kv_chunk:(c + 1) * kv_chunk]
        if not running_max:
            p = jnp.exp2(s)
            l, acc = jnp.sum(p, axis=0, keepdims=True), _dot(vt, p.astype(BF16))
            if state is not None:
                l, acc = state[1] + l, state[2] + acc
            return (None, l, acc)
        s_max = jnp.max(s, axis=0, keepdims=True)
        if state is None:
            p = jnp.exp2(s - s_max)
            return (s_max, jnp.sum(p, axis=0, keepdims=True), _dot(vt, p.astype(BF16)))
        m, l, acc = state
        m_new = jnp.maximum(m, s_max)
        alpha = jnp.exp2(m - m_new)
        p = jnp.exp2(s - m_new)
        return (m_new, alpha * l + jnp.sum(p, axis=0, keepdims=True),
                alpha * acc + _dot(vt, p.astype(BF16)))

    state = [None] * len(streams)
    pending = [scores(item) for item in items[:lookahead]]
    for pos, (i, c) in enumerate(items):
        s = pending.pop(0)
        if pos + lookahead < len(items):
            pending.append(scores(items[pos + lookahead]))
        state[i] = update(i, c, s, state[i])
        if pos in group_end:
            g, ids = group_end[pos]
            finish(g, [state[j][2] / state[j][1] for j in ids])


def _attn_a_kernel(q_ref, k_ref, vt_ref, km_ref, vmt_ref, bias_ref, o_ref, *, running_max):
    tq = A_Q_COLS
    km, vmt = km_ref[...], vmt_ref[...]
    groups = []
    for t in range(q_ref.shape[2] // tq):
        group = []
        for head in range(A_HEADS):
            rows = slice((head // A_GROUP) * HEAD_DIM, (head // A_GROUP + 1) * HEAD_DIM)
            group.append((q_ref[head, :, t * tq:(t + 1) * tq], k_ref, vt_ref.at[rows, :], km, vmt[rows, :]))
        groups.append(group)

    def finish(t, outs):
        o_ref[t * tq:(t + 1) * tq, :] = jnp.concatenate(outs, axis=0).T.astype(o_ref.dtype)

    _attend_t(groups, bias_ref, running_max, finish, lookahead=A_LOOKAHEAD, kv_chunk=A_KV_CHUNK)


def _attn_b_kernel(q_ref, k_ref, vt_ref, km_ref, vmt_ref, bias_ref, lam_ref, subln_ref, o_ref, *,
                   running_max):
    tq = B_Q_COLS
    first = _row_index((LANES, tq)) < HEAD_DIM
    zero = jnp.zeros((LANES, tq), BF16)
    groups = []
    for t in range(q_ref.shape[2] // tq):
        group = []
        for h in range(B_HEADS_PER_STEP):
            q = q_ref[h, :, t * tq:(t + 1) * tq]
            cols = slice(h * LANES, (h + 1) * LANES)
            qt = jnp.concatenate([jnp.where(first, q, zero), jnp.where(first, zero, q)], axis=1)
            group.append((qt, k_ref.at[:, cols], vt_ref.at[cols, :], km_ref[:, cols], vmt_ref[cols, :]))
        groups.append(group)
    lam_p = lam_ref[...]
    lam = (jnp.exp(jnp.sum(lam_p[0:1] * lam_p[1:2], axis=-1, keepdims=True))
           - jnp.exp(jnp.sum(lam_p[2:3] * lam_p[3:4], axis=-1, keepdims=True)) + LAM_INIT)

    def finish(t, outs):
        ys = []
        for o in outs:
            d = o[:, :tq] - lam * o[:, tq:]
            ms = jnp.mean(d * d, axis=0, keepdims=True)
            ys.append(d * lax.rsqrt(ms + EPS) * subln_ref[...] * (1.0 - LAM_INIT))
        o_ref[t * tq:(t + 1) * tq, :] = jnp.concatenate(ys, axis=0).T.astype(o_ref.dtype)

    _attend_t(groups, bias_ref, running_max, finish, lookahead=B_LOOKAHEAD, kv_chunk=B_KV_CHUNK)


def _attention_a(qat, ka, vat, kam, vamt, bias, running_max):
    bsz, _, _, seq = qat.shape
    const = lambda shape: pl.BlockSpec(shape, lambda b, i: (0,) * len(shape))
    return pl.pallas_call(
        functools.partial(_attn_a_kernel, running_max=running_max),
        grid=(bsz, seq // A_STEP_COLS),
        in_specs=[pl.BlockSpec((None, A_HEADS, LANES, A_STEP_COLS), lambda b, i: (b, 0, 0, i)),
                  pl.BlockSpec((None, seq, LANES), lambda b, i: (b, 0, 0)),
                  pl.BlockSpec((None, LANES, seq), lambda b, i: (b, 0, 0)),
                  const(kam.shape), const(vamt.shape), const(bias.shape)],
        out_specs=pl.BlockSpec((None, A_STEP_COLS, A_WIDTH), lambda b, i: (b, i, 0)),
        out_shape=jax.ShapeDtypeStruct((bsz, seq, A_WIDTH), BF16),
        compiler_params=pltpu.CompilerParams(
            dimension_semantics=("arbitrary", "arbitrary"), vmem_limit_bytes=VMEM_LIMIT_BYTES),
        name="attn_a",
    )(qat, ka, vat, kam, vamt, bias)


def _attention_b(qbt, kb, vbt, kbm, vbmt, bias, lam_params, subln_col, running_max):
    bsz, _, _, seq = qbt.shape
    hp = B_HEADS_PER_STEP
    width = hp * LANES
    const = lambda shape: pl.BlockSpec(shape, lambda b, h, i: (0,) * len(shape))
    return pl.pallas_call(
        functools.partial(_attn_b_kernel, running_max=running_max),
        grid=(bsz, B_HEADS // hp, seq // B_STEP_COLS),
        in_specs=[pl.BlockSpec((None, hp, LANES, B_STEP_COLS), lambda b, h, i: (b, h, 0, i)),
                  pl.BlockSpec((None, seq, width), lambda b, h, i: (b, 0, h)),
                  pl.BlockSpec((None, width, seq), lambda b, h, i: (b, h, 0)),
                  pl.BlockSpec((LANES, width), lambda b, h, i: (0, h)),
                  pl.BlockSpec((width, LANES), lambda b, h, i: (h, 0)),
                  const(bias.shape), const(lam_params.shape), const(subln_col.shape)],
        out_specs=pl.BlockSpec((None, B_STEP_COLS, width), lambda b, h, i: (b, i, h)),
        out_shape=jax.ShapeDtypeStruct((bsz, seq, B_WIDTH), BF16),
        compiler_params=pltpu.CompilerParams(
            dimension_semantics=("arbitrary", "arbitrary", "arbitrary"),
            vmem_limit_bytes=VMEM_LIMIT_BYTES),
        name="attn_b",
    )(qbt, kb, vbt, kbm, vbmt, bias, lam_params, subln_col)


def _post_kernel(x_ref, oa_ref, ob_ref, wo_ref, g_ref, wg_ref, wu_ref, wd_ref, y_ref):
    woa_ref, wob_ref = wo_ref.at[0:A_WIDTH, :], wo_ref.at[A_WIDTH:A_WIDTH + B_WIDTH, :]
    blocks = [slice(r, r + POST_ROW_BLOCK) for r in range(0, x_ref.shape[0], POST_ROW_BLOCK)]
    x1 = [x_ref[rb, :] + _dot(oa_ref[rb, :], woa_ref[...]) + _dot(ob_ref[rb, :], wob_ref[...])
          for rb in blocks]

    def normed(v):
        ms = jnp.mean(v * v, axis=-1, keepdims=True)
        return (v * lax.rsqrt(ms + EPS) * g_ref[...]).astype(BF16)

    gate_up = []
    for v in x1:
        h = normed(v)
        gate_up.append((_dot(h, wg_ref[...]), _dot(h, wu_ref[...])))
    for rb, v, (gate, up) in zip(blocks, x1, gate_up):
        act = (gate * jax.nn.sigmoid(gate) * up).astype(BF16)
        y_ref[rb, :] = v + _dot(act, wd_ref[...])


def _post(x_rows, oa, ob, wo, g, wg, wu, wd):
    n_rows, d_model = x_rows.shape
    row_blk = lambda width: pl.BlockSpec((POST_ROWS, width), lambda i: (i, 0))
    resident = lambda a: pl.BlockSpec(a.shape, lambda i: (0, 0), pipeline_mode=pl.Buffered(1))
    return pl.pallas_call(
        _post_kernel,
        grid=(n_rows // POST_ROWS,),
        in_specs=[row_blk(d_model), row_blk(A_WIDTH), row_blk(B_WIDTH),
                  resident(wo), resident(g),
                  resident(wg), resident(wu), resident(wd)],
        out_specs=row_blk(d_model),
        out_shape=jax.ShapeDtypeStruct((n_rows, d_model), F32),
        compiler_params=pltpu.CompilerParams(
            dimension_semantics=("arbitrary",), vmem_limit_bytes=VMEM_LIMIT_BYTES),
        name="post",
    )(x_rows, oa, ob, wo, g, wg, wu, wd)


def _pair_tables(ang):
    c, s = jnp.cos(ang), jnp.sin(ang)
    cos64 = jnp.repeat(c, 2, axis=-1)
    sin64 = jnp.stack([-s, s], axis=-1).reshape(ang.shape[0], HEAD_DIM)
    return jnp.tile(cos64, (1, 2)), jnp.tile(sin64, (1, 2))


def _rope_tables(seq):
    half = HEAD_DIM // 2
    t = jnp.arange(seq)
    inv_a = ROPE_THETA ** (-jnp.arange(0, half, 2, dtype=F32) / half)
    r = (t // GRID_W).astype(F32)
    c = (t % GRID_W).astype(F32)
    ang_a = jnp.concatenate([r[:, None] * inv_a[None, :], c[:, None] * inv_a[None, :]], axis=-1)
    inv_b = ROPE_THETA ** (-jnp.arange(0, HEAD_DIM, 2, dtype=F32) / HEAD_DIM)
    pos = jnp.arange(N_META + seq, dtype=F32)
    ang_b = pos[:, None] * inv_b[None, :]
    seq_tabs = _pair_tables(ang_a) + _pair_tables(ang_b[N_META:])
    meta_tabs = _pair_tables(jnp.zeros((N_META, half), F32)) + _pair_tables(ang_b[:N_META])
    return seq_tabs, tuple(_pad_rows(t, META_ROWS) for t in meta_tabs)


def _pad_rows(a, rows):
    return jnp.pad(a, ((0, rows - a.shape[0]), (0, 0)))


def kernel(x, meta_tokens, attn_norm_g, w_in, a_q_norm_g, a_k_norm_g, b_q_norm_g, b_k_norm_g,
           b_lambda_q1, b_lambda_k1, b_lambda_q2, b_lambda_k2, b_subln_g, w_out,
           ffn_norm_g, w_gate, w_up, w_down):
    bsz, seq, d_model = x.shape
    assert w_in.shape[0] == 1, "single-layer block"
    assert seq % PROJ_ROWS == 0 and seq % KV_CHUNK == 0 and seq % A_STEP_COLS == 0 and seq % B_STEP_COLS == 0
    layer = 0

    seq_tabs, meta_tabs = _rope_tables(seq)
    gains = jnp.stack([jnp.tile(g[layer].astype(F32), 2)
                       for g in (a_q_norm_g, a_k_norm_g, b_q_norm_g, b_k_norm_g)])
    idx = jnp.arange(2 * LANES) // HEAD_DIM
    seg = (idx[:, None] == idx[None, :]).astype(BF16)
    w_in_b = w_in[layer].astype(BF16)
    g_attn = attn_norm_g[layer].astype(F32)[None, :]

    qat, ka, vat, qbt, kb, vbt = _project(x, g_attn, w_in_b, seq_tabs, gains, seg, PROJ_ROWS)
    meta3 = _pad_rows(meta_tokens.astype(F32), META_ROWS)[None]
    _, kam, vamt, _, kbm, vbmt = _project(meta3, g_attn, w_in_b, meta_tabs, gains, seg, META_ROWS)
    kam, vamt, kbm, vbmt = kam[0], vamt[0], kbm[0], vbmt[0]
    bias = jnp.where(jnp.arange(META_ROWS) < N_META, 0.0, MASK_BIAS).astype(F32)
    bias = jnp.broadcast_to(bias[:, None], (META_ROWS, LANES))

    lam_params = jnp.stack([jnp.pad(p[layer].astype(F32), (0, LANES - HEAD_DIM))
                            for p in (b_lambda_q1, b_lambda_k1, b_lambda_q2, b_lambda_k2)])
    subln_col = b_subln_g[layer].astype(F32)[:, None]

    def attend(running_max):
        return (_attention_a(qat, ka, vat, kam, vamt, bias, running_max),
                _attention_b(qbt, kb, vbt, kbm, vbmt, bias, lam_params, subln_col, running_max))

    amax = lambda g: jnp.max(jnp.abs(g[layer].astype(F32)))
    score_bound = HEAD_DIM * Q_PRESCALE * jnp.maximum(amax(a_q_norm_g) * amax(a_k_norm_g),
                                                      amax(b_q_norm_g) * amax(b_k_norm_g))
    out_a, out_b = lax.cond(score_bound <= UNSHIFTED_SCORE_LIMIT,
                            lambda: attend(False), lambda: attend(True))

    w_out_b = w_out[layer].astype(BF16)
    y = _post(x.reshape(bsz * seq, d_model), out_a.reshape(bsz * seq, A_WIDTH),
              out_b.reshape(bsz * seq, B_WIDTH), w_out_b,
              ffn_norm_g[layer].astype(F32)[None, :],
              w_gate[layer].astype(BF16), w_up[layer].astype(BF16), w_down[layer].astype(BF16))
    return y.reshape(bsz, seq, d_model)
```

```python
import functools
import math

import jax
import jax.numpy as jnp
from jax import lax
from jax.experimental import pallas as pl
from jax.experimental.pallas import tpu as pltpu

N_META = 16
GRID_W = 64
HEAD_DIM = 64
ROPE_THETA = 10000.0
EPS = 1e-6
A_HEADS = 8
A_KV_HEADS = 2
A_GROUP = A_HEADS // A_KV_HEADS
A_WIDTH = A_HEADS * HEAD_DIM
A_KV_WIDTH = A_KV_HEADS * HEAD_DIM
B_HEADS = 4
B_VDIM = 2 * HEAD_DIM
B_WIDTH = B_HEADS * B_VDIM
LAM_INIT = 0.8 - 0.6 * math.exp(-0.3 * 0)

LANES = 128
VMEM_LIMIT_BYTES = 56 * 1024 * 1024

Q_PRESCALE = (HEAD_DIM ** -0.5) * math.log2(math.e)
MASK_BIAS = -1e30
UNSHIFTED_SCORE_LIMIT = 60.0

PROJ_ROWS = 1024
PROJ_ROW_BLOCK = 256
PROJ_GROUP_COLS = 4
META_ROWS = LANES
MXU_TILE = 256
Q_TILES_PER_STEP = 2
A_Q_COLS = MXU_TILE
A_KV_CHUNK = MXU_TILE
A_LOOKAHEAD = 8
B_Q_COLS = 512
B_HEADS_PER_STEP = 2
B_KV_CHUNK = 2048
B_LOOKAHEAD = 1
KV_CHUNK = 512
A_STEP_COLS = A_Q_COLS * Q_TILES_PER_STEP
B_STEP_COLS = B_Q_COLS * Q_TILES_PER_STEP
POST_ROWS = 1024
POST_ROW_BLOCK = 256

BF16 = jnp.bfloat16
F32 = jnp.float32


def _dot(a, b):
    return jnp.dot(a, b, preferred_element_type=F32)


def _lane_index(shape):
    return lax.broadcasted_iota(jnp.int32, shape, len(shape) - 1)


def _row_index(shape):
    return lax.broadcasted_iota(jnp.int32, shape, 0)


def _proj_kernel(x_ref, g_ref, w_ref, cos_a_ref, sin_a_ref, cos_b_ref, sin_b_ref,
                 gains_ref, seg_ref,
                 qa_ref, ka_ref, va_ref, qb_ref, kb_ref, vb_ref):
    rows = x_ref.shape[0]
    blk = min(rows, PROJ_ROW_BLOCK)
    lane = _lane_index((blk, LANES))
    even = (lane % 2) == 0
    low_half = lane < HEAD_DIM
    seg_pair = seg_ref[...]
    seg_one = seg_ref[0:LANES, 0:LANES]
    g_aq = gains_ref[0:1, :] * Q_PRESCALE
    g_ak = gains_ref[1:2, :]
    g_bq = gains_ref[2:3, :] * Q_PRESCALE
    g_bk = gains_ref[3:4, :]

    def norm_rope(col, ss, gain, cos_ref, sin_ref, rb):
        y = col * lax.rsqrt(ss * (1.0 / HEAD_DIM) + EPS) * gain
        swapped = jnp.where(even, pltpu.roll(y, LANES - 1, 1), pltpu.roll(y, 1, 1))
        return y * cos_ref[rb, :] + swapped * sin_ref[rb, :]

    def emit_qa(j, col, ss, rb):
        col = norm_rope(col, ss, g_aq, cos_a_ref, sin_a_ref, rb)
        flipped = pltpu.roll(col, HEAD_DIM, 1)
        for half in range(2):
            head = 2 * j + half
            kv = head // A_GROUP
            src = col if half == kv else flipped
            keep = low_half if kv == 0 else jnp.logical_not(low_half)
            qa_ref[head, :, rb] = jnp.where(keep, src, 0.0).T.astype(BF16)

    def emit_ka(j, col, ss, rb):
        ka_ref[rb, :] = norm_rope(col, ss, g_ak, cos_a_ref, sin_a_ref, rb).astype(BF16)

    def emit_va(j, col, ss, rb):
        va_ref[:, rb] = col.T.astype(BF16)

    def emit_qb(j, col, ss, rb):
        qb_ref[j, :, rb] = norm_rope(col, ss, g_bq, cos_b_ref, sin_b_ref, rb).T.astype(BF16)

    def emit_kb(j, col, ss, rb):
        kb_ref[rb, j * LANES:(j + 1) * LANES] = norm_rope(col, ss, g_bk, cos_b_ref, sin_b_ref, rb).astype(BF16)

    def emit_vb(j, col, ss, rb):
        vb_ref[j * LANES:(j + 1) * LANES, rb] = col.T.astype(BF16)

    handlers = ([(emit_qa, j) for j in range(A_WIDTH // LANES)] + [(emit_ka, 0), (emit_va, 0)]
                + [(emit_qb, j) for j in range(B_HEADS)] + [(emit_kb, j) for j in range(B_HEADS)]
                + [(emit_vb, j) for j in range(B_HEADS)])
    assert len(handlers) * LANES == w_ref.shape[1]
    groups = [range(s, min(s + PROJ_GROUP_COLS, len(handlers)))
              for s in range(0, len(handlers), PROJ_GROUP_COLS)]

    row_blocks = [slice(r, r + blk) for r in range(0, rows, blk)]

    def normed(rb):
        x = x_ref[rb, :]
        ms = jnp.mean(x * x, axis=-1, keepdims=True)
        return (x * lax.rsqrt(ms + EPS) * g_ref[...]).astype(BF16)

    hs = [normed(rb) for rb in row_blocks]
    items = [(r, cols) for cols in groups for r in range(len(row_blocks))]

    def sums_of_squares(slab, cols):
        needs = [handlers[c][0] not in (emit_va, emit_vb) for c in cols]
        ss, n = [None] * len(cols), 0
        while n < len(cols):
            width = 2 if (needs[n] and n + 1 < len(cols) and needs[n + 1]) else 1
            if needs[n]:
                part = slab[:, n * LANES:(n + width) * LANES]
                tot = _dot((part * part).astype(BF16), seg_pair if width == 2 else seg_one)
                for w in range(width):
                    ss[n + w] = tot[:, w * LANES:(w + 1) * LANES]
            n += width
        return ss

    def project(item):
        r, cols = item
        return _dot(hs[r], w_ref[:, cols.start * LANES:cols.stop * LANES])

    slab_next = project(items[0])
    for pos, (r, cols) in enumerate(items):
        slab = slab_next
        if pos + 1 < len(items):
            slab_next = project(items[pos + 1])
        ss = sums_of_squares(slab, cols)
        for n, c in enumerate(cols):
            fn, j = handlers[c]
            fn(j, slab[:, n * LANES:(n + 1) * LANES], ss[n], row_blocks[r])


def _project(x3, g, w_bf16, tables, gains, seg, rows_per_step):
    bsz, n_rows, d_model = x3.shape
    in_width = w_bf16.shape[1]
    t = rows_per_step
    tab_blk = pl.BlockSpec((t, LANES), lambda b, i: (i, 0))
    const = lambda shape: pl.BlockSpec(shape, lambda b, i: (0, 0))
    out_specs = [
        pl.BlockSpec((None, A_HEADS, LANES, t), lambda b, i: (b, 0, 0, i)),
        pl.BlockSpec((None, t, LANES), lambda b, i: (b, i, 0)),
        pl.BlockSpec((None, LANES, t), lambda b, i: (b, 0, i)),
        pl.BlockSpec((None, B_HEADS, LANES, t), lambda b, i: (b, 0, 0, i)),
        pl.BlockSpec((None, t, B_WIDTH), lambda b, i: (b, i, 0)),
        pl.BlockSpec((None, B_WIDTH, t), lambda b, i: (b, 0, i)),
    ]
    out_shapes = [(bsz, A_HEADS, LANES, n_rows), (bsz, n_rows, LANES), (bsz, LANES, n_rows),
                  (bsz, B_HEADS, LANES, n_rows), (bsz, n_rows, B_WIDTH), (bsz, B_WIDTH, n_rows)]
    return pl.pallas_call(
        _proj_kernel,
        grid=(bsz, n_rows // t),
        in_specs=[pl.BlockSpec((None, t, d_model), lambda b, i: (b, i, 0)),
                  const((1, d_model)), const((d_model, in_width)),
                  tab_blk, tab_blk, tab_blk, tab_blk, const(gains.shape), const(seg.shape)],
        out_specs=out_specs,
        out_shape=[jax.ShapeDtypeStruct(s, BF16) for s in out_shapes],
        compiler_params=pltpu.CompilerParams(
            dimension_semantics=("arbitrary", "arbitrary"), vmem_limit_bytes=VMEM_LIMIT_BYTES),
        name="proj",
    )(x3, g, w_bf16, *tables, gains, seg)


def _attend_t(groups, bias_ref, running_max, finish, lookahead=1, kv_chunk=KV_CHUNK):
    streams = [s for group in groups for s in group]
    n_chunks = streams[0][1].shape[0] // kv_chunk
    items, group_end, base = [], {}, 0
    for g, group in enumerate(groups):
        ids = range(base, base + len(group))
        items += [(i, None) for i in ids] + [(i, c) for c in range(n_chunks) for i in ids]
        group_end[len(items) - 1] = (g, ids)
        base += len(group)

    def scores(item):
        i, c = item
        qt, k_ref, _, km, _ = streams[i]
        if c is None:
            bias = jnp.concatenate([bias_ref[...]] * (qt.shape[1] // LANES), axis=1)
            return _dot(km, qt) + bias
        return _dot(k_ref[c * kv_chunk:(c + 1) * kv_chunk, :], qt)

    def update(i, c, s, state):
        _, _, vt_ref, _, vmt = streams[i]
        vt = vmt if c is None else vt_ref[:, c * kv_chunk:(c + 1) * kv_chunk]
        if not running_max:
            p = jnp.exp2(s)
            l, acc = jnp.sum(p, axis=0, keepdims=True), _dot(vt, p.astype(BF16))
            if state is not None:
                l, acc = state[1] + l, state[2] + acc
            return (None, l, acc)
        s_max = jnp.max(s, axis=0, keepdims=True)
        if state is None:
            p = jnp.exp2(s - s_max)
            return (s_max, jnp.sum(p, axis=0, keepdims=True), _dot(vt, p.astype(BF16)))
        m, l, acc = state
        m_new = jnp.maximum(m, s_max)
        alpha = jnp.exp2(m - m_new)
        p = jnp.exp2(s - m_new)
        return (m_new, alpha * l + jnp.sum(p, axis=0, keepdims=True),
                alpha * acc + _dot(vt, p.astype(BF16)))

    state = [None] * len(streams)
    pending = [scores(item) for item in items[:lookahead]]
    for pos, (i, c) in enumerate(items):
        s = pending.pop(0)
        if pos + lookahead < len(items):
            pending.append(scores(items[pos + lookahead]))
        state[i] = update(i, c, s, state[i])
        if pos in group_end:
            g, ids = group_end[pos]
            finish(g, [state[j][2] / state[j][1] for j in ids])


def _attn_a_kernel(q_ref, k_ref, vt_ref, km_ref, vmt_ref, bias_ref, o_ref, *, running_max):
    tq = A_Q_COLS
    km, vmt = km_ref[...], vmt_ref[...]
    groups = []
    for t in range(q_ref.shape[2] // tq):
        group = []
        for head in range(A_HEADS):
            rows = slice((head // A_GROUP) * HEAD_DIM, (head // A_GROUP + 1) * HEAD_DIM)
            group.append((q_ref[head, :, t * tq:(t + 1) * tq], k_ref, vt_ref.at[rows, :], km, vmt[rows, :]))
        groups.append(group)

    def finish(t, outs):
        o_ref[t * tq:(t + 1) * tq, :] = jnp.concatenate(outs, axis=0).T.astype(o_ref.dtype)

    _attend_t(groups, bias_ref, running_max, finish, lookahead=A_LOOKAHEAD, kv_chunk=A_KV_CHUNK)


def _attn_b_kernel(q_ref, k_ref, vt_ref, km_ref, vmt_ref, bias_ref, lam_ref, subln_ref, o_ref, *,
                   running_max):
    tq = B_Q_COLS
    first = _row_index((LANES, tq)) < HEAD_DIM
    zero = jnp.zeros((LANES, tq), BF16)
    groups = []
    for t in range(q_ref.shape[2] // tq):
        group = []
        for h in range(B_HEADS_PER_STEP):
            q = q_ref[h, :, t * tq:(t + 1) * tq]
            cols = slice(h * LANES, (h + 1) * LANES)
            qt = jnp.concatenate([jnp.where(first, q, zero), jnp.where(first, zero, q)], axis=1)
            group.append((qt, k_ref.at[:, cols], vt_ref.at[cols, :], km_ref[:, cols], vmt_ref[cols, :]))
        groups.append(group)
    lam_p = lam_ref[...]
    lam = (jnp.exp(jnp.sum(lam_p[0:1] * lam_p[1:2], axis=-1, keepdims=True))
           - jnp.exp(jnp.sum(lam_p[2:3] * lam_p[3:4], axis=-1, keepdims=True)) + LAM_INIT)

    def finish(t, outs):
        ys = []
        for o in outs:
            d = o[:, :tq] - lam * o[:, tq:]
            ms = jnp.mean(d * d, axis=0, keepdims=True)
            ys.append(d * lax.rsqrt(ms + EPS) * subln_ref[...] * (1.0 - LAM_INIT))
        o_ref[t * tq:(t + 1) * tq, :] = jnp.concatenate(ys, axis=0).T.astype(o_ref.dtype)

    _attend_t(groups, bias_ref, running_max, finish, lookahead=B_LOOKAHEAD, kv_chunk=B_KV_CHUNK)


def _attention_a(qat, ka, vat, kam, vamt, bias, running_max):
    bsz, _, _, seq = qat.shape
    const = lambda shape: pl.BlockSpec(shape, lambda b, i: (0,) * len(shape))
    return pl.pallas_call(
        functools.partial(_attn_a_kernel, running_max=running_max),
        grid=(bsz, seq // A_STEP_COLS),
        in_specs=[pl.BlockSpec((None, A_HEADS, LANES, A_STEP_COLS), lambda b, i: (b, 0, 0, i)),
                  pl.BlockSpec((None, seq, LANES), lambda b, i: (b, 0, 0)),
                  pl.BlockSpec((None, LANES, seq), lambda b, i: (b, 0, 0)),
                  const(kam.shape), const(vamt.shape), const(bias.shape)],
        out_specs=pl.BlockSpec((None, A_STEP_COLS, A_WIDTH), lambda b, i: (b, i, 0)),
        out_shape=jax.ShapeDtypeStruct((bsz, seq, A_WIDTH), BF16),
        compiler_params=pltpu.CompilerParams(
            dimension_semantics=("arbitrary", "arbitrary"), vmem_limit_bytes=VMEM_LIMIT_BYTES),
        name="attn_a",
    )(qat, ka, vat, kam, vamt, bias)


def _attention_b(qbt, kb, vbt, kbm, vbmt, bias, lam_params, subln_col, running_max):
    bsz, _, _, seq = qbt.shape
    hp = B_HEADS_PER_STEP
    width = hp * LANES
    const = lambda shape: pl.BlockSpec(shape, lambda b, h, i: (0,) * len(shape))
    return pl.pallas_call(
        functools.partial(_attn_b_kernel, running_max=running_max),
        grid=(bsz, B_HEADS // hp, seq // B_STEP_COLS),
        in_specs=[pl.BlockSpec((None, hp, LANES, B_STEP_COLS), lambda b, h, i: (b, h, 0, i)),
                  pl.BlockSpec((None, seq, width), lambda b, h, i: (b, 0, h)),
                  pl.BlockSpec((None, width, seq), lambda b, h, i: (b, h, 0)),
                  pl.BlockSpec((LANES, width), lambda b, h, i: (0, h)),
                  pl.BlockSpec((width, LANES), lambda b, h, i: (h, 0)),
                  const(bias.shape), const(lam_params.shape), const(subln_col.shape)],
        out_specs=pl.BlockSpec((None, B_STEP_COLS, width), lambda b, h, i: (b, i, h)),
        out_shape=jax.ShapeDtypeStruct((bsz, seq, B_WIDTH), BF16),
        compiler_params=pltpu.CompilerParams(
            dimension_semantics=("arbitrary", "arbitrary", "arbitrary"),
            vmem_limit_bytes=VMEM_LIMIT_BYTES),
        name="attn_b",
    )(qbt, kb, vbt, kbm, vbmt, bias, lam_params, subln_col)


def _post_kernel(x_ref, oa_ref, ob_ref, wo_ref, g_ref, wg_ref, wu_ref, wd_ref, y_ref):
    woa_ref, wob_ref = wo_ref.at[0:A_WIDTH, :], wo_ref.at[A_WIDTH:A_WIDTH + B_WIDTH, :]
    blocks = [slice(r, r + POST_ROW_BLOCK) for r in range(0, x_ref.shape[0], POST_ROW_BLOCK)]
    x1 = [x_ref[rb, :] + _dot(oa_ref[rb, :], woa_ref[...]) + _dot(ob_ref[rb, :], wob_ref[...])
          for rb in blocks]

    def normed(v):
        ms = jnp.mean(v * v, axis=-1, keepdims=True)
        return (v * lax.rsqrt(ms + EPS) * g_ref[...]).astype(BF16)

    gate_up = []
    for v in x1:
        h = normed(v)
        gate_up.append((_dot(h, wg_ref[...]), _dot(h, wu_ref[...])))
    for rb, v, (gate, up) in zip(blocks, x1, gate_up):
        act = (gate * jax.nn.sigmoid(gate) * up).astype(BF16)
        y_ref[rb, :] = v + _dot(act, wd_ref[...])


def _post(x_rows, oa, ob, wo, g, wg, wu, wd):
    n_rows, d_model = x_rows.shape
    row_blk = lambda width: pl.BlockSpec((POST_ROWS, width), lambda i: (i, 0))
    resident = lambda a: pl.BlockSpec(a.shape, lambda i: (0, 0), pipeline_mode=pl.Buffered(1))
    return pl.pallas_call(
        _post_kernel,
        grid=(n_rows // POST_ROWS,),
        in_specs=[row_blk(d_model), row_blk(A_WIDTH), row_blk(B_WIDTH),
                  resident(wo), resident(g),
                  resident(wg), resident(wu), resident(wd)],
        out_specs=row_blk(d_model),
        out_shape=jax.ShapeDtypeStruct((n_rows, d_model), F32),
        compiler_params=pltpu.CompilerParams(
            dimension_semantics=("arbitrary",), vmem_limit_bytes=VMEM_LIMIT_BYTES),
        name="post",
    )(x_rows, oa, ob, wo, g, wg, wu, wd)


def _pair_tables(ang):
    c, s = jnp.cos(ang), jnp.sin(ang)
    cos64 = jnp.repeat(c, 2, axis=-1)
    sin64 = jnp.stack([-s, s], axis=-1).reshape(ang.shape[0], HEAD_DIM)
    return jnp.tile(cos64, (1, 2)), jnp.tile(sin64, (1, 2))


def _rope_tables(seq):
    half = HEAD_DIM // 2
    t = jnp.arange(seq)
    inv_a = ROPE_THETA ** (-jnp.arange(0, half, 2, dtype=F32) / half)
    r = (t // GRID_W).astype(F32)
    c = (t % GRID_W).astype(F32)
    ang_a = jnp.concatenate([r[:, None] * inv_a[None, :], c[:, None] * inv_a[None, :]], axis=-1)
    inv_b = ROPE_THETA ** (-jnp.arange(0, HEAD_DIM, 2, dtype=F32) / HEAD_DIM)
    pos = jnp.arange(N_META + seq, dtype=F32)
    ang_b = pos[:, None] * inv_b[None, :]
    seq_tabs = _pair_tables(ang_a) + _pair_tables(ang_b[N_META:])
    meta_tabs = _pair_tables(jnp.zeros((N_META, half), F32)) + _pair_tables(ang_b[:N_META])
    return seq_tabs, tuple(_pad_rows(t, META_ROWS) for t in meta_tabs)


def _pad_rows(a, rows):
    return jnp.pad(a, ((0, rows - a.shape[0]), (0, 0)))


def kernel(x, meta_tokens, attn_norm_g, w_in, a_q_norm_g, a_k_norm_g, b_q_norm_g, b_k_norm_g,
           b_lambda_q1, b_lambda_k1, b_lambda_q2, b_lambda_k2, b_subln_g, w_out,
           ffn_norm_g, w_gate, w_up, w_down):
    bsz, seq, d_model = x.shape
    assert w_in.shape[0] == 1, "single-layer block"
    assert seq % PROJ_ROWS == 0 and seq % KV_CHUNK == 0 and seq % A_STEP_COLS == 0 and seq % B_STEP_COLS == 0
    layer = 0

    seq_tabs, meta_tabs = _rope_tables(seq)
    gains = jnp.stack([jnp.tile(g[layer].astype(F32), 2)
                       for g in (a_q_norm_g, a_k_norm_g, b_q_norm_g, b_k_norm_g)])
    idx = jnp.arange(2 * LANES) // HEAD_DIM
    seg = (idx[:, None] == idx[None, :]).astype(BF16)
    w_in_b = w_in[layer].astype(BF16)
    g_attn = attn_norm_g[layer].astype(F32)[None, :]

    qat, ka, vat, qbt, kb, vbt = _project(x, g_attn, w_in_b, seq_tabs, gains, seg, PROJ_ROWS)
    meta3 = _pad_rows(meta_tokens.astype(F32), META_ROWS)[None]
    _, kam, vamt, _, kbm, vbmt = _project(meta3, g_attn, w_in_b, meta_tabs, gains, seg, META_ROWS)
    kam, vamt, kbm, vbmt = kam[0], vamt[0], kbm[0], vbmt[0]
    bias = jnp.where(jnp.arange(META_ROWS) < N_META, 0.0, MASK_BIAS).astype(F32)
    bias = jnp.broadcast_to(bias[:, None], (META_ROWS, LANES))

    lam_params = jnp.stack([jnp.pad(p[layer].astype(F32), (0, LANES - HEAD_DIM))
                            for p in (b_lambda_q1, b_lambda_k1, b_lambda_q2, b_lambda_k2)])
    subln_col = b_subln_g[layer].astype(F32)[:, None]

    def attend(running_max):
        return (_attention_a(qat, ka, vat, kam, vamt, bias, running_max),
                _attention_b(qbt, kb, vbt, kbm, vbmt, bias, lam_params, subln_col, running_max))

    amax = lambda g: jnp.max(jnp.abs(g[layer].astype(F32)))
    score_bound = HEAD_DIM * Q_PRESCALE * jnp.maximum(amax(a_q_norm_g) * amax(a_k_norm_g),
                                                      amax(b_q_norm_g) * amax(b_k_norm_g))
    out_a, out_b = lax.cond(score_bound <= UNSHIFTED_SCORE_LIMIT,
                            lambda: attend(False), lambda: attend(True))

    w_out_b = w_out[layer].astype(BF16)
    y = _post(x.reshape(bsz * seq, d_model), out_a.reshape(bsz * seq, A_WIDTH),
              out_b.reshape(bsz * seq, B_WIDTH), w_out_b,
              ffn_norm_g[layer].astype(F32)[None, :],
              w_gate[layer].astype(BF16), w_up[layer].astype(BF16), w_down[layer].astype(BF16))
    return y.reshape(bsz, seq, d_model)
```

```python
import functools
import math

import jax
import jax.numpy as jnp
from jax import lax
from jax.experimental import pallas as pl
from jax.experimental.pallas import tpu as pltpu

N_META = 16
GRID_W = 64
HEAD_DIM = 64
ROPE_THETA = 10000.0
EPS = 1e-6
A_HEADS = 8
A_KV_HEADS = 2
A_GROUP = A_HEADS // A_KV_HEADS
A_WIDTH = A_HEADS * HEAD_DIM
A_KV_WIDTH = A_KV_HEADS * HEAD_DIM
B_HEADS = 4
B_VDIM = 2 * HEAD_DIM
B_WIDTH = B_HEADS * B_VDIM
LAM_INIT = 0.8 - 0.6 * math.exp(-0.3 * 0)

LANES = 128
VMEM_LIMIT_BYTES = 56 * 1024 * 1024

Q_PRESCALE = (HEAD_DIM ** -0.5) * math.log2(math.e)
MASK_BIAS = -1e30
UNSHIFTED_SCORE_LIMIT = 60.0

PROJ_ROWS = 1024
PROJ_ROW_BLOCK = 256
PROJ_GROUP_COLS = 4
META_ROWS = LANES
MXU_TILE = 256
Q_TILES_PER_STEP = 2
A_Q_COLS = MXU_TILE
A_KV_CHUNK = MXU_TILE
A_LOOKAHEAD = 8
B_Q_COLS = 512
B_HEADS_PER_STEP = 2
B_KV_CHUNK = 4096
B_LOOKAHEAD = 1
KV_CHUNK = 512
A_STEP_COLS = A_Q_COLS * Q_TILES_PER_STEP
B_STEP_COLS = B_Q_COLS * Q_TILES_PER_STEP
POST_ROWS = 1024
POST_ROW_BLOCK = 256

BF16 = jnp.bfloat16
F32 = jnp.float32


def _dot(a, b):
    return jnp.dot(a, b, preferred_element_type=F32)


def _lane_index(shape):
    return lax.broadcasted_iota(jnp.int32, shape, len(shape) - 1)


def _row_index(shape):
    return lax.broadcasted_iota(jnp.int32, shape, 0)


def _proj_kernel(x_ref, g_ref, w_ref, cos_a_ref, sin_a_ref, cos_b_ref, sin_b_ref,
                 gains_ref, seg_ref,
                 qa_ref, ka_ref, va_ref, qb_ref, kb_ref, vb_ref):
    rows = x_ref.shape[0]
    blk = min(rows, PROJ_ROW_BLOCK)
    lane = _lane_index((blk, LANES))
    even = (lane % 2) == 0
    low_half = lane < HEAD_DIM
    seg_pair = seg_ref[...]
    seg_one = seg_ref[0:LANES, 0:LANES]
    g_aq = gains_ref[0:1, :] * Q_PRESCALE
    g_ak = gains_ref[1:2, :]
    g_bq = gains_ref[2:3, :] * Q_PRESCALE
    g_bk = gains_ref[3:4, :]

    def norm_rope(col, ss, gain, cos_ref, sin_ref, rb):
        y = col * lax.rsqrt(ss * (1.0 / HEAD_DIM) + EPS) * gain
        swapped = jnp.where(even, pltpu.roll(y, LANES - 1, 1), pltpu.roll(y, 1, 1))
        return y * cos_ref[rb, :] + swapped * sin_ref[rb, :]

    def emit_qa(j, col, ss, rb):
        col = norm_rope(col, ss, g_aq, cos_a_ref, sin_a_ref, rb)
        flipped = pltpu.roll(col, HEAD_DIM, 1)
        for half in range(2):
            head = 2 * j + half
            kv = head // A_GROUP
            src = col if half == kv else flipped
            keep = low_half if kv == 0 else jnp.logical_not(low_half)
            qa_ref[head, :, rb] = jnp.where(keep, src, 0.0).T.astype(BF16)

    def emit_ka(j, col, ss, rb):
        ka_ref[rb, :] = norm_rope(col, ss, g_ak, cos_a_ref, sin_a_ref, rb).astype(BF16)

    def emit_va(j, col, ss, rb):
        va_ref[:, rb] = col.T.astype(BF16)

    def emit_qb(j, col, ss, rb):
        qb_ref[j, :, rb] = norm_rope(col, ss, g_bq, cos_b_ref, sin_b_ref, rb).T.astype(BF16)

    def emit_kb(j, col, ss, rb):
        kb_ref[rb, j * LANES:(j + 1) * LANES] = norm_rope(col, ss, g_bk, cos_b_ref, sin_b_ref, rb).astype(BF16)

    def emit_vb(j, col, ss, rb):
        vb_ref[j * LANES:(j + 1) * LANES, rb] = col.T.astype(BF16)

    handlers = ([(emit_qa, j) for j in range(A_WIDTH // LANES)] + [(emit_ka, 0), (emit_va, 0)]
                + [(emit_qb, j) for j in range(B_HEADS)] + [(emit_kb, j) for j in range(B_HEADS)]
                + [(emit_vb, j) for j in range(B_HEADS)])
    assert len(handlers) * LANES == w_ref.shape[1]
    groups = [range(s, min(s + PROJ_GROUP_COLS, len(handlers)))
              for s in range(0, len(handlers), PROJ_GROUP_COLS)]

    row_blocks = [slice(r, r + blk) for r in range(0, rows, blk)]

    def normed(rb):
        x = x_ref[rb, :]
        ms = jnp.mean(x * x, axis=-1, keepdims=True)
        return (x * lax.rsqrt(ms + EPS) * g_ref[...]).astype(BF16)

    hs = [normed(rb) for rb in row_blocks]
    items = [(r, cols) for cols in groups for r in range(len(row_blocks))]

    def sums_of_squares(slab, cols):
        needs = [handlers[c][0] not in (emit_va, emit_vb) for c in cols]
        ss, n = [None] * len(cols), 0
        while n < len(cols):
            width = 2 if (needs[n] and n + 1 < len(cols) and needs[n + 1]) else 1
            if needs[n]:
                part = slab[:, n * LANES:(n + width) * LANES]
                tot = _dot((part * part).astype(BF16), seg_pair if width == 2 else seg_one)
                for w in range(width):
                    ss[n + w] = tot[:, w * LANES:(w + 1) * LANES]
            n += width
        return ss

    def project(item):
        r, cols = item
        return _dot(hs[r], w_ref[:, cols.start * LANES:cols.stop * LANES])

    slab_next = project(items[0])
    for pos, (r, cols) in enumerate(items):
        slab = slab_next
        if pos + 1 < len(items):
            slab_next = project(items[pos + 1])
        ss = sums_of_squares(slab, cols)
        for n, c in enumerate(cols):
            fn, j = handlers[c]
            fn(j, slab[:, n * LANES:(n + 1) * LANES], ss[n], row_blocks[r])


def _project(x3, g, w_bf16, tables, gains, seg, rows_per_step):
    bsz, n_rows, d_model = x3.shape
    in_width = w_bf16.shape[1]
    t = rows_per_step
    tab_blk = pl.BlockSpec((t, LANES), lambda b, i: (i, 0))
    const = lambda shape: pl.BlockSpec(shape, lambda b, i: (0, 0))
    out_specs = [
        pl.BlockSpec((None, A_HEADS, LANES, t), lambda b, i: (b, 0, 0, i)),
        pl.BlockSpec((None, t, LANES), lambda b, i: (b, i, 0)),
        pl.BlockSpec((None, LANES, t), lambda b, i: (b, 0, i)),
        pl.BlockSpec((None, B_HEADS, LANES, t), lambda b, i: (b, 0, 0, i)),
        pl.BlockSpec((None, t, B_WIDTH), lambda b, i: (b, i, 0)),
        pl.BlockSpec((None, B_WIDTH, t), lambda b, i: (b, 0, i)),
    ]
    out_shapes = [(bsz, A_HEADS, LANES, n_rows), (bsz, n_rows, LANES), (bsz, LANES, n_rows),
                  (bsz, B_HEADS, LANES, n_rows), (bsz, n_rows, B_WIDTH), (bsz, B_WIDTH, n_rows)]
    return pl.pallas_call(
        _proj_kernel,
        grid=(bsz, n_rows // t),
        in_specs=[pl.BlockSpec((None, t, d_model), lambda b, i: (b, i, 0)),
                  const((1, d_model)), const((d_model, in_width)),
                  tab_blk, tab_blk, tab_blk, tab_blk, const(gains.shape), const(seg.shape)],
        out_specs=out_specs,
        out_shape=[jax.ShapeDtypeStruct(s, BF16) for s in out_shapes],
        compiler_params=pltpu.CompilerParams(
            dimension_semantics=("arbitrary", "arbitrary"), vmem_limit_bytes=VMEM_LIMIT_BYTES),
        name="proj",
    )(x3, g, w_bf16, *tables, gains, seg)


def _attend_t(groups, bias_ref, running_max, finish, lookahead=1, kv_chunk=KV_CHUNK):
    streams = [s for group in groups for s in group]
    n_chunks = streams[0][1].shape[0] // kv_chunk
    items, group_end, base = [], {}, 0
    for g, group in enumerate(groups):
        ids = range(base, base + len(group))
        items += [(i, None) for i in ids] + [(i, c) for c in range(n_chunks) for i in ids]
        group_end[len(items) - 1] = (g, ids)
        base += len(group)

    def scores(item):
        i, c = item
        qt, k_ref, _, km, _ = streams[i]
        if c is None:
            bias = jnp.concatenate([bias_ref[...]] * (qt.shape[1] // LANES), axis=1)
            return _dot(km, qt) + bias
        return _dot(k_ref[c * kv_chunk:(c + 1) * kv_chunk, :], qt)

    def update(i, c, s, state):
        _, _, vt_ref, _, vmt = streams[i]
        vt = vmt if c is None else vt_ref[:, c * kv_chunk:(c + 1) * kv_chunk]
        if not running_max:
            p = jnp.exp2(s)
            l, acc = jnp.sum(p, axis=0, keepdims=True), _dot(vt, p.astype(BF16))
            if state is not None:
                l, acc = state[1] + l, state[2] + acc
            return (None, l, acc)
        s_max = jnp.max(s, axis=0, keepdims=True)
        if state is None:
            p = jnp.exp2(s - s_max)
            return (s_max, jnp.sum(p, axis=0, keepdims=True), _dot(vt, p.astype(BF16)))
        m, l, acc = state
        m_new = jnp.maximum(m, s_max)
        alpha = jnp.exp2(m - m_new)
        p = jnp.exp2(s - m_new)
        return (m_new, alpha * l + jnp.sum(p, axis=0, keepdims=True),
                alpha * acc + _dot(vt, p.astype(BF16)))

    state = [None] * len(streams)
    pending = [scores(item) for item in items[:lookahead]]
    for pos, (i, c) in enumerate(items):
        s = pending.pop(0)
        if pos + lookahead < len(items):
            pending.append(scores(items[pos + lookahead]))
        state[i] = update(i, c, s, state[i])
        if pos in group_end:
            g, ids = group_end[pos]
            finish(g, [state[j][2] / state[j][1] for j in ids])


def _attn_a_kernel(q_ref, k_ref, vt_ref, km_ref, vmt_ref, bias_ref, o_ref, *, running_max):
    tq = A_Q_COLS
    km, vmt = km_ref[...], vmt_ref[...]
    groups = []
    for t in range(q_ref.shape[2] // tq):
        group = []
        for head in range(A_HEADS):
            rows = slice((head // A_GROUP) * HEAD_DIM, (head // A_GROUP + 1) * HEAD_DIM)
            group.append((q_ref[head, :, t * tq:(t + 1) * tq], k_ref, vt_ref.at[rows, :], km, vmt[rows, :]))
        groups.append(group)

    def finish(t, outs):
        o_ref[t * tq:(t + 1) * tq, :] = jnp.concatenate(outs, axis=0).T.astype(o_ref.dtype)

    _attend_t(groups, bias_ref, running_max, finish, lookahead=A_LOOKAHEAD, kv_chunk=A_KV_CHUNK)


def _attn_b_kernel(q_ref, k_ref, vt_ref, km_ref, vmt_ref, bias_ref, lam_ref, subln_ref, o_ref, *,
                   running_max):
    tq = B_Q_COLS
    first = _row_index((LANES, tq)) < HEAD_DIM
    zero = jnp.zeros((LANES, tq), BF16)
    groups = []
    for t in range(q_ref.shape[2] // tq):
        group = []
        for h in range(B_HEADS_PER_STEP):
            q = q_ref[h, :, t * tq:(t + 1) * tq]
            cols = slice(h * LANES, (h + 1) * LANES)
            qt = jnp.concatenate([jnp.where(first, q, zero), jnp.where(first, zero, q)], axis=1)
            group.append((qt, k_ref.at[:, cols], vt_ref.at[cols, :], km_ref[:, cols], vmt_ref[cols, :]))
        groups.append(group)
    lam_p = lam_ref[...]
    lam = (jnp.exp(jnp.sum(lam_p[0:1] * lam_p[1:2], axis=-1, keepdims=True))
           - jnp.exp(jnp.sum(lam_p[2:3] * lam_p[3:4], axis=-1, keepdims=True)) + LAM_INIT)

    def finish(t, outs):
        ys = []
        for o in outs:
            d = o[:, :tq] - lam * o[:, tq:]
            ms = jnp.mean(d * d, axis=0, keepdims=True)
            ys.append(d * lax.rsqrt(ms + EPS) * subln_ref[...] * (1.0 - LAM_INIT))
        o_ref[t * tq:(t + 1) * tq, :] = jnp.concatenate(ys, axis=0).T.astype(o_ref.dtype)

    _attend_t(groups, bias_ref, running_max, finish, lookahead=B_LOOKAHEAD, kv_chunk=B_KV_CHUNK)


def _attention_a(qat, ka, vat, kam, vamt, bias, running_max):
    bsz, _, _, seq = qat.shape
    const = lambda shape: pl.BlockSpec(shape, lambda b, i: (0,) * len(shape))
    return pl.pallas_call(
        functools.partial(_attn_a_kernel, running_max=running_max),
        grid=(bsz, seq // A_STEP_COLS),
        in_specs=[pl.BlockSpec((None, A_HEADS, LANES, A_STEP_COLS), lambda b, i: (b, 0, 0, i)),
                  pl.BlockSpec((None, seq, LANES), lambda b, i: (b, 0, 0)),
                  pl.BlockSpec((None, LANES, seq), lambda b, i: (b, 0, 0)),
                  const(kam.shape), const(vamt.shape), const(bias.shape)],
        out_specs=pl.BlockSpec((None, A_STEP_COLS, A_WIDTH), lambda b, i: (b, i, 0)),
        out_shape=jax.ShapeDtypeStruct((bsz, seq, A_WIDTH), BF16),
        compiler_params=pltpu.CompilerParams(
            dimension_semantics=("arbitrary", "arbitrary"), vmem_limit_bytes=VMEM_LIMIT_BYTES),
        name="attn_a",
    )(qat, ka, vat, kam, vamt, bias)


def _attention_b(qbt, kb, vbt, kbm, vbmt, bias, lam_params, subln_col, running_max):
    bsz, _, _, seq = qbt.shape
    hp = B_HEADS_PER_STEP
    width = hp * LANES
    const = lambda shape: pl.BlockSpec(shape, lambda b, h, i: (0,) * len(shape))
    return pl.pallas_call(
        functools.partial(_attn_b_kernel, running_max=running_max),
        grid=(bsz, B_HEADS // hp, seq // B_STEP_COLS),
        in_specs=[pl.BlockSpec((None, hp, LANES, B_STEP_COLS), lambda b, h, i: (b, h, 0, i)),
                  pl.BlockSpec((None, seq, width), lambda b, h, i: (b, 0, h)),
                  pl.BlockSpec((None, width, seq), lambda b, h, i: (b, h, 0)),
                  pl.BlockSpec((LANES, width), lambda b, h, i: (0, h)),
                  pl.BlockSpec((width, LANES), lambda b, h, i: (h, 0)),
                  const(bias.shape), const(lam_params.shape), const(subln_col.shape)],
        out_specs=pl.BlockSpec((None, B_STEP_COLS, width), lambda b, h, i: (b, i, h)),
        out_shape=jax.ShapeDtypeStruct((bsz, seq, B_WIDTH), BF16),
        compiler_params=pltpu.CompilerParams(
            dimension_semantics=("arbitrary", "arbitrary", "arbitrary"),
            vmem_limit_bytes=VMEM_LIMIT_BYTES),
        name="attn_b",
    )(qbt, kb, vbt, kbm, vbmt, bias, lam_params, subln_col)


def _post_kernel(x_ref, oa_ref, ob_ref, wo_ref, g_ref, wg_ref, wu_ref, wd_ref, y_ref):
    woa_ref, wob_ref = wo_ref.at[0:A_WIDTH, :], wo_ref.at[A_WIDTH:A_WIDTH + B_WIDTH, :]
    blocks = [slice(r, r + POST_ROW_BLOCK) for r in range(0, x_ref.shape[0], POST_ROW_BLOCK)]
    x1 = [x_ref[rb, :] + _dot(oa_ref[rb, :], woa_ref[...]) + _dot(ob_ref[rb, :], wob_ref[...])
          for rb in blocks]

    def normed(v):
        ms = jnp.mean(v * v, axis=-1, keepdims=True)
        return (v * lax.rsqrt(ms + EPS) * g_ref[...]).astype(BF16)

    gate_up = []
    for v in x1:
        h = normed(v)
        gate_up.append((_dot(h, wg_ref[...]), _dot(h, wu_ref[...])))
    for rb, v, (gate, up) in zip(blocks, x1, gate_up):
        act = (gate * jax.nn.sigmoid(gate) * up).astype(BF16)
        y_ref[rb, :] = v + _dot(act, wd_ref[...])


def _post(x_rows, oa, ob, wo, g, wg, wu, wd):
    n_rows, d_model = x_rows.shape
    row_blk = lambda width: pl.BlockSpec((POST_ROWS, width), lambda i: (i, 0))
    resident = lambda a: pl.BlockSpec(a.shape, lambda i: (0, 0), pipeline_mode=pl.Buffered(1))
    return pl.pallas_call(
        _post_kernel,
        grid=(n_rows // POST_ROWS,),
        in_specs=[row_blk(d_model), row_blk(A_WIDTH), row_blk(B_WIDTH),
                  resident(wo), resident(g),
                  resident(wg), resident(wu), resident(wd)],
        out_specs=row_blk(d_model),
        out_shape=jax.ShapeDtypeStruct((n_rows, d_model), F32),
        compiler_params=pltpu.CompilerParams(
            dimension_semantics=("arbitrary",), vmem_limit_bytes=VMEM_LIMIT_BYTES),
        name="post",
    )(x_rows, oa, ob, wo, g, wg, wu, wd)


def _pair_tables(ang):
    c, s = jnp.cos(ang), jnp.sin(ang)
    cos64 = jnp.repeat(c, 2, axis=-1)
    sin64 = jnp.stack([-s, s], axis=-1).reshape(ang.shape[0], HEAD_DIM)
    return jnp.tile(cos64, (1, 2)), jnp.tile(sin64, (1, 2))


def _rope_tables(seq):
    half = HEAD_DIM // 2
    t = jnp.arange(seq)
    inv_a = ROPE_THETA ** (-jnp.arange(0, half, 2, dtype=F32) / half)
    r = (t // GRID_W).astype(F32)
    c = (t % GRID_W).astype(F32)
    ang_a = jnp.concatenate([r[:, None] * inv_a[None, :], c[:, None] * inv_a[None, :]], axis=-1)
    inv_b = ROPE_THETA ** (-jnp.arange(0, HEAD_DIM, 2, dtype=F32) / HEAD_DIM)
    pos = jnp.arange(N_META + seq, dtype=F32)
    ang_b = pos[:, None] * inv_b[None, :]
    seq_tabs = _pair_tables(ang_a) + _pair_tables(ang_b[N_META:])
    meta_tabs = _pair_tables(jnp.zeros((N_META, half), F32)) + _pair_tables(ang_b[:N_META])
    return seq_tabs, tuple(_pad_rows(t, META_ROWS) for t in meta_tabs)


def _pad_rows(a, rows):
    return jnp.pad(a, ((0, rows - a.shape[0]), (0, 0)))


def kernel(x, meta_tokens, attn_norm_g, w_in, a_q_norm_g, a_k_norm_g, b_q_norm_g, b_k_norm_g,
           b_lambda_q1, b_lambda_k1, b_lambda_q2, b_lambda_k2, b_subln_g, w_out,
           ffn_norm_g, w_gate, w_up, w_down):
    bsz, seq, d_model = x.shape
    assert w_in.shape[0] == 1, "single-layer block"
    assert seq % PROJ_ROWS == 0 and seq % KV_CHUNK == 0 and seq % A_STEP_COLS == 0 and seq % B_STEP_COLS == 0
    layer = 0

    seq_tabs, meta_tabs = _rope_tables(seq)
    gains = jnp.stack([jnp.tile(g[layer].astype(F32), 2)
                       for g in (a_q_norm_g, a_k_norm_g, b_q_norm_g, b_k_norm_g)])
    idx = jnp.arange(2 * LANES) // HEAD_DIM
    seg = (idx[:, None] == idx[None, :]).astype(BF16)
    w_in_b = w_in[layer].astype(BF16)
    g_attn = attn_norm_g[layer].astype(F32)[None, :]

    qat, ka, vat, qbt, kb, vbt = _project(x, g_attn, w_in_b, seq_tabs, gains, seg, PROJ_ROWS)
    meta3 = _pad_rows(meta_tokens.astype(F32), META_ROWS)[None]
    _, kam, vamt, _, kbm, vbmt = _project(meta3, g_attn, w_in_b, meta_tabs, gains, seg, META_ROWS)
    kam, vamt, kbm, vbmt = kam[0], vamt[0], kbm[0], vbmt[0]
    bias = jnp.where(jnp.arange(META_ROWS) < N_META, 0.0, MASK_BIAS).astype(F32)
    bias = jnp.broadcast_to(bias[:, None], (META_ROWS, LANES))

    lam_params = jnp.stack([jnp.pad(p[layer].astype(F32), (0, LANES - HEAD_DIM))
                            for p in (b_lambda_q1, b_lambda_k1, b_lambda_q2, b_lambda_k2)])
    subln_col = b_subln_g[layer].astype(F32)[:, None]

    def attend(running_max):
        return (_attention_a(qat, ka, vat, kam, vamt, bias, running_max),
                _attention_b(qbt, kb, vbt, kbm, vbmt, bias, lam_params, subln_col, running_max))

    amax = lambda g: jnp.max(jnp.abs(g[layer].astype(F32)))
    score_bound = HEAD_DIM * Q_PRESCALE * jnp.maximum(amax(a_q_norm_g) * amax(a_k_norm_g),
                                                      amax(b_q_norm_g) * amax(b_k_norm_g))
    out_a, out_b = lax.cond(score_bound <= UNSHIFTED_SCORE_LIMIT,
                            lambda: attend(False), lambda: attend(True))

    w_out_b = w_out[layer].astype(BF16)
    y = _post(x.reshape(bsz * seq, d_model), out_a.reshape(bsz * seq, A_WIDTH),
              out_b.reshape(bsz * seq, B_WIDTH), w_out_b,
              ffn_norm_g[layer].astype(F32)[None, :],
              w_gate[layer].astype(BF16), w_up[layer].astype(BF16), w_down[layer].astype(BF16))
    return y.reshape(bsz, seq, d_model)
```

```python
import functools
import math

import jax
import jax.numpy as jnp
from jax import lax
from jax.experimental import pallas as pl
from jax.experimental.pallas import tpu as pltpu

N_META = 16
GRID_W = 64
HEAD_DIM = 64
ROPE_THETA = 10000.0
EPS = 1e-6
A_HEADS = 8
A_KV_HEADS = 2
A_GROUP = A_HEADS // A_KV_HEADS
A_WIDTH = A_HEADS * HEAD_DIM
A_KV_WIDTH = A_KV_HEADS * HEAD_DIM
B_HEADS = 4
B_VDIM = 2 * HEAD_DIM
B_WIDTH = B_HEADS * B_VDIM
LAM_INIT = 0.8 - 0.6 * math.exp(-0.3 * 0)

LANES = 128
VMEM_LIMIT_BYTES = 56 * 1024 * 1024

Q_PRESCALE = (HEAD_DIM ** -0.5) * math.log2(math.e)
MASK_BIAS = -1e30
UNSHIFTED_SCORE_LIMIT = 60.0

PROJ_ROWS = 1024
PROJ_ROW_BLOCK = 256
PROJ_GROUP_COLS = 4
META_ROWS = LANES
MXU_TILE = 256
Q_TILES_PER_STEP = 2
A_Q_COLS = MXU_TILE
A_KV_CHUNK = MXU_TILE
A_LOOKAHEAD = 8
B_Q_COLS = 512
B_HEADS_PER_STEP = 2
B_KV_CHUNK = 4096
B_LOOKAHEAD = 1
KV_CHUNK = 512
A_STEP_COLS = A_Q_COLS * Q_TILES_PER_STEP
B_STEP_COLS = B_Q_COLS * Q_TILES_PER_STEP
POST_ROWS = 1024
POST_ROW_BLOCK = 256

BF16 = jnp.bfloat16
F32 = jnp.float32


def _dot(a, b):
    return jnp.dot(a, b, preferred_element_type=F32)


def _lane_index(shape):
    return lax.broadcasted_iota(jnp.int32, shape, len(shape) - 1)


def _row_index(shape):
    return lax.broadcasted_iota(jnp.int32, shape, 0)


def _proj_kernel(x_ref, g_ref, w_ref, cos_a_ref, sin_a_ref, cos_b_ref, sin_b_ref,
                 gains_ref, seg_ref,
                 qa_ref, ka_ref, va_ref, qb_ref, kb_ref, vb_ref):
    rows = x_ref.shape[0]
    blk = min(rows, PROJ_ROW_BLOCK)
    lane = _lane_index((blk, LANES))
    even = (lane % 2) == 0
    low_half = lane < HEAD_DIM
    seg_pair = seg_ref[...]
    seg_one = seg_ref[0:LANES, 0:LANES]
    g_aq = gains_ref[0:1, :] * Q_PRESCALE
    g_ak = gains_ref[1:2, :]
    g_bq = gains_ref[2:3, :] * Q_PRESCALE
    g_bk = gains_ref[3:4, :]

    def norm_rope(col, ss, gain, cos_ref, sin_ref, rb):
        y = col * lax.rsqrt(ss * (1.0 / HEAD_DIM) + EPS) * gain
        swapped = jnp.where(even, pltpu.roll(y, LANES - 1, 1), pltpu.roll(y, 1, 1))
        return y * cos_ref[rb, :] + swapped * sin_ref[rb, :]

    def emit_qa(j, col, ss, rb):
        col = norm_rope(col, ss, g_aq, cos_a_ref, sin_a_ref, rb)
        flipped = pltpu.roll(col, HEAD_DIM, 1)
        for half in range(2):
            head = 2 * j + half
            kv = head // A_GROUP
            src = col if half == kv else flipped
            keep = low_half if kv == 0 else jnp.logical_not(low_half)
            qa_ref[head, :, rb] = jnp.where(keep, src, 0.0).T.astype(BF16)

    def emit_ka(j, col, ss, rb):
        ka_ref[rb, :] = norm_rope(col, ss, g_ak, cos_a_ref, sin_a_ref, rb).astype(BF16)

    def emit_va(j, col, ss, rb):
        va_ref[:, rb] = col.T.astype(BF16)

    def emit_qb(j, col, ss, rb):
        qb_ref[j, :, rb] = norm_rope(col, ss, g_bq, cos_b_ref, sin_b_ref, rb).T.astype(BF16)

    def emit_kb(j, col, ss, rb):
        kb_ref[rb, j * LANES:(j + 1) * LANES] = norm_rope(col, ss, g_bk, cos_b_ref, sin_b_ref, rb).astype(BF16)

    def emit_vb(j, col, ss, rb):
        vb_ref[j * LANES:(j + 1) * LANES, rb] = col.T.astype(BF16)

    handlers = ([(emit_qa, j) for j in range(A_WIDTH // LANES)] + [(emit_ka, 0), (emit_va, 0)]
                + [(emit_qb, j) for j in range(B_HEADS)] + [(emit_kb, j) for j in range(B_HEADS)]
                + [(emit_vb, j) for j in range(B_HEADS)])
    assert len(handlers) * LANES == w_ref.shape[1]
    groups = [range(s, min(s + PROJ_GROUP_COLS, len(handlers)))
              for s in range(0, len(handlers), PROJ_GROUP_COLS)]

    row_blocks = [slice(r, r + blk) for r in range(0, rows, blk)]

    def normed(rb):
        x = x_ref[rb, :]
        ms = jnp.mean(x * x, axis=-1, keepdims=True)
        return (x * lax.rsqrt(ms + EPS) * g_ref[...]).astype(BF16)

    hs = [normed(rb) for rb in row_blocks]
    items = [(r, cols) for cols in groups for r in range(len(row_blocks))]

    def sums_of_squares(slab, cols):
        needs = [handlers[c][0] not in (emit_va, emit_vb) for c in cols]
        ss, n = [None] * len(cols), 0
        while n < len(cols):
            width = 2 if (needs[n] and n + 1 < len(cols) and needs[n + 1]) else 1
            if needs[n]:
                part = slab[:, n * LANES:(n + width) * LANES]
                tot = _dot((part * part).astype(BF16), seg_pair if width == 2 else seg_one)
                for w in range(width):
                    ss[n + w] = tot[:, w * LANES:(w + 1) * LANES]
            n += width
        return ss

    w_groups = {}

    def project(item):
        r, cols = item
        if cols.start not in w_groups:
            w_groups[cols.start] = w_ref[:, cols.start * LANES:cols.stop * LANES].astype(BF16)
        return _dot(hs[r], w_groups[cols.start])

    slab_next = project(items[0])
    for pos, (r, cols) in enumerate(items):
        slab = slab_next
        if pos + 1 < len(items):
            slab_next = project(items[pos + 1])
        ss = sums_of_squares(slab, cols)
        for n, c in enumerate(cols):
            fn, j = handlers[c]
            fn(j, slab[:, n * LANES:(n + 1) * LANES], ss[n], row_blocks[r])


def _project(x3, g, w_bf16, tables, gains, seg, rows_per_step):
    bsz, n_rows, d_model = x3.shape
    in_width = w_bf16.shape[1]
    t = rows_per_step
    tab_blk = pl.BlockSpec((t, LANES), lambda b, i: (i, 0))
    const = lambda shape: pl.BlockSpec(shape, lambda b, i: (0, 0))
    out_specs = [
        pl.BlockSpec((None, A_HEADS, LANES, t), lambda b, i: (b, 0, 0, i)),
        pl.BlockSpec((None, t, LANES), lambda b, i: (b, i, 0)),
        pl.BlockSpec((None, LANES, t), lambda b, i: (b, 0, i)),
        pl.BlockSpec((None, B_HEADS, LANES, t), lambda b, i: (b, 0, 0, i)),
        pl.BlockSpec((None, t, B_WIDTH), lambda b, i: (b, i, 0)),
        pl.BlockSpec((None, B_WIDTH, t), lambda b, i: (b, 0, i)),
    ]
    out_shapes = [(bsz, A_HEADS, LANES, n_rows), (bsz, n_rows, LANES), (bsz, LANES, n_rows),
                  (bsz, B_HEADS, LANES, n_rows), (bsz, n_rows, B_WIDTH), (bsz, B_WIDTH, n_rows)]
    return pl.pallas_call(
        _proj_kernel,
        grid=(bsz, n_rows // t),
        in_specs=[pl.BlockSpec((None, t, d_model), lambda b, i: (b, i, 0)),
                  const((1, d_model)),
                  pl.BlockSpec((d_model, in_width), lambda b, i: (0, 0), pipeline_mode=pl.Buffered(1)),
                  tab_blk, tab_blk, tab_blk, tab_blk, const(gains.shape), const(seg.shape)],
        out_specs=out_specs,
        out_shape=[jax.ShapeDtypeStruct(s, BF16) for s in out_shapes],
        compiler_params=pltpu.CompilerParams(
            dimension_semantics=("arbitrary", "arbitrary"), vmem_limit_bytes=VMEM_LIMIT_BYTES),
        name="proj",
    )(x3, g, w_bf16, *tables, gains, seg)


def _attend_t(groups, bias_ref, running_max, finish, lookahead=1, kv_chunk=KV_CHUNK):
    streams = [s for group in groups for s in group]
    n_chunks = streams[0][1].shape[0] // kv_chunk
    items, group_end, base = [], {}, 0
    for g, group in enumerate(groups):
        ids = range(base, base + len(group))
        items += [(i, None) for i in ids] + [(i, c) for c in range(n_chunks) for i in ids]
        group_end[len(items) - 1] = (g, ids)
        base += len(group)

    def scores(item):
        i, c = item
        qt, k_ref, _, km, _ = streams[i]
        if c is None:
            bias = jnp.concatenate([bias_ref[...]] * (qt.shape[1] // LANES), axis=1)
            return _dot(km, qt) + bias
        return _dot(k_ref[c * kv_chunk:(c + 1) * kv_chunk, :], qt)

    def update(i, c, s, state):
        _, _, vt_ref, _, vmt = streams[i]
        vt = vmt if c is None else vt_ref[:, c * kv_chunk:(c + 1) * kv_chunk]
        if not running_max:
            p = jnp.exp2(s)
            l, acc = jnp.sum(p, axis=0, keepdims=True), _dot(vt, p.astype(BF16))
            if state is not None:
                l, acc = state[1] + l, state[2] + acc
            return (None, l, acc)
        s_max = jnp.max(s, axis=0, keepdims=True)
        if state is None:
            p = jnp.exp2(s - s_max)
            return (s_max, jnp.sum(p, axis=0, keepdims=True), _dot(vt, p.astype(BF16)))
        m, l, acc = state
        m_new = jnp.maximum(m, s_max)
        alpha = jnp.exp2(m - m_new)
        p = jnp.exp2(s - m_new)
        return (m_new, alpha * l + jnp.sum(p, axis=0, keepdims=True),
                alpha * acc + _dot(vt, p.astype(BF16)))

    state = [None] * len(streams)
    pending = [scores(item) for item in items[:lookahead]]
    for pos, (i, c) in enumerate(items):
        s = pending.pop(0)
        if pos + lookahead < len(items):
            pending.append(scores(items[pos + lookahead]))
        state[i] = update(i, c, s, state[i])
        if pos in group_end:
            g, ids = group_end[pos]
            finish(g, [state[j][2] / state[j][1] for j in ids])


def _attn_a_kernel(q_ref, k_ref, vt_ref, km_ref, vmt_ref, bias_ref, o_ref, *, running_max):
    tq = A_Q_COLS
    km, vmt = km_ref[...], vmt_ref[...]
    groups = []
    for t in range(q_ref.shape[2] // tq):
        group = []
        for head in range(A_HEADS):
            rows = slice((head // A_GROUP) * HEAD_DIM, (head // A_GROUP + 1) * HEAD_DIM)
            group.append((q_ref[head, :, t * tq:(t + 1) * tq], k_ref, vt_ref.at[rows, :], km, vmt[rows, :]))
        groups.append(group)

    def finish(t, outs):
        o_ref[t * tq:(t + 1) * tq, :] = jnp.concatenate(outs, axis=0).T.astype(o_ref.dtype)

    _attend_t(groups, bias_ref, running_max, finish, lookahead=A_LOOKAHEAD, kv_chunk=A_KV_CHUNK)


def _attn_b_kernel(q_ref, k_ref, vt_ref, km_ref, vmt_ref, bias_ref, lam_ref, subln_ref, o_ref, *,
                   running_max):
    tq = B_Q_COLS
    first = _row_index((LANES, tq)) < HEAD_DIM
    zero = jnp.zeros((LANES, tq), BF16)
    groups = []
    for t in range(q_ref.shape[2] // tq):
        group = []
        for h in range(B_HEADS_PER_STEP):
            q = q_ref[h, :, t * tq:(t + 1) * tq]
            cols = slice(h * LANES, (h + 1) * LANES)
            qt = jnp.concatenate([jnp.where(first, q, zero), jnp.where(first, zero, q)], axis=1)
            group.append((qt, k_ref.at[:, cols], vt_ref.at[cols, :], km_ref[:, cols], vmt_ref[cols, :]))
        groups.append(group)
    lam_p = lam_ref[...]
    lam = (jnp.exp(jnp.sum(lam_p[0:1] * lam_p[1:2], axis=-1, keepdims=True))
           - jnp.exp(jnp.sum(lam_p[2:3] * lam_p[3:4], axis=-1, keepdims=True)) + LAM_INIT)

    def finish(t, outs):
        ys = []
        for o in outs:
            d = o[:, :tq] - lam * o[:, tq:]
            ms = jnp.mean(d * d, axis=0, keepdims=True)
            ys.append(d * lax.rsqrt(ms + EPS) * subln_ref[...] * (1.0 - LAM_INIT))
        o_ref[t * tq:(t + 1) * tq, :] = jnp.concatenate(ys, axis=0).T.astype(o_ref.dtype)

    _attend_t(groups, bias_ref, running_max, finish, lookahead=B_LOOKAHEAD, kv_chunk=B_KV_CHUNK)


def _attention_a(qat, ka, vat, kam, vamt, bias, running_max):
    bsz, _, _, seq = qat.shape
    const = lambda shape: pl.BlockSpec(shape, lambda b, i: (0,) * len(shape))
    return pl.pallas_call(
        functools.partial(_attn_a_kernel, running_max=running_max),
        grid=(bsz, seq // A_STEP_COLS),
        in_specs=[pl.BlockSpec((None, A_HEADS, LANES, A_STEP_COLS), lambda b, i: (b, 0, 0, i)),
                  pl.BlockSpec((None, seq, LANES), lambda b, i: (b, 0, 0)),
                  pl.BlockSpec((None, LANES, seq), lambda b, i: (b, 0, 0)),
                  const(kam.shape), const(vamt.shape), const(bias.shape)],
        out_specs=pl.BlockSpec((None, A_STEP_COLS, A_WIDTH), lambda b, i: (b, i, 0)),
        out_shape=jax.ShapeDtypeStruct((bsz, seq, A_WIDTH), BF16),
        compiler_params=pltpu.CompilerParams(
            dimension_semantics=("arbitrary", "arbitrary"), vmem_limit_bytes=VMEM_LIMIT_BYTES),
        name="attn_a",
    )(qat, ka, vat, kam, vamt, bias)


def _attention_b(qbt, kb, vbt, kbm, vbmt, bias, lam_params, subln_col, running_max):
    bsz, _, _, seq = qbt.shape
    hp = B_HEADS_PER_STEP
    width = hp * LANES
    const = lambda shape: pl.BlockSpec(shape, lambda b, h, i: (0,) * len(shape))
    return pl.pallas_call(
        functools.partial(_attn_b_kernel, running_max=running_max),
        grid=(bsz, B_HEADS // hp, seq // B_STEP_COLS),
        in_specs=[pl.BlockSpec((None, hp, LANES, B_STEP_COLS), lambda b, h, i: (b, h, 0, i)),
                  pl.BlockSpec((None, seq, width), lambda b, h, i: (b, 0, h)),
                  pl.BlockSpec((None, width, seq), lambda b, h, i: (b, h, 0)),
                  pl.BlockSpec((LANES, width), lambda b, h, i: (0, h)),
                  pl.BlockSpec((width, LANES), lambda b, h, i: (h, 0)),
                  const(bias.shape), const(lam_params.shape), const(subln_col.shape)],
        out_specs=pl.BlockSpec((None, B_STEP_COLS, width), lambda b, h, i: (b, i, h)),
        out_shape=jax.ShapeDtypeStruct((bsz, seq, B_WIDTH), BF16),
        compiler_params=pltpu.CompilerParams(
            dimension_semantics=("arbitrary", "arbitrary", "arbitrary"),
            vmem_limit_bytes=VMEM_LIMIT_BYTES),
        name="attn_b",
    )(qbt, kb, vbt, kbm, vbmt, bias, lam_params, subln_col)


def _post_kernel(x_ref, oa_ref, ob_ref, wo_ref, g_ref, wg_ref, wu_ref, wd_ref, y_ref):
    woa_ref, wob_ref = wo_ref.at[0:A_WIDTH, :], wo_ref.at[A_WIDTH:A_WIDTH + B_WIDTH, :]
    blocks = [slice(r, r + POST_ROW_BLOCK) for r in range(0, x_ref.shape[0], POST_ROW_BLOCK)]
    x1 = [x_ref[rb, :] + _dot(oa_ref[rb, :], woa_ref[...]) + _dot(ob_ref[rb, :], wob_ref[...])
          for rb in blocks]

    def normed(v):
        ms = jnp.mean(v * v, axis=-1, keepdims=True)
        return (v * lax.rsqrt(ms + EPS) * g_ref[...]).astype(BF16)

    gate_up = []
    for v in x1:
        h = normed(v)
        gate_up.append((_dot(h, wg_ref[...]), _dot(h, wu_ref[...])))
    for rb, v, (gate, up) in zip(blocks, x1, gate_up):
        act = (gate * jax.nn.sigmoid(gate) * up).astype(BF16)
        y_ref[rb, :] = v + _dot(act, wd_ref[...])


def _post(x_rows, oa, ob, wo, g, wg, wu, wd):
    n_rows, d_model = x_rows.shape
    row_blk = lambda width: pl.BlockSpec((POST_ROWS, width), lambda i: (i, 0))
    resident = lambda a: pl.BlockSpec(a.shape, lambda i: (0, 0), pipeline_mode=pl.Buffered(1))
    return pl.pallas_call(
        _post_kernel,
        grid=(n_rows // POST_ROWS,),
        in_specs=[row_blk(d_model), row_blk(A_WIDTH), row_blk(B_WIDTH),
                  resident(wo), resident(g),
                  resident(wg), resident(wu), resident(wd)],
        out_specs=row_blk(d_model),
        out_shape=jax.ShapeDtypeStruct((n_rows, d_model), F32),
        compiler_params=pltpu.CompilerParams(
            dimension_semantics=("arbitrary",), vmem_limit_bytes=VMEM_LIMIT_BYTES),
        name="post",
    )(x_rows, oa, ob, wo, g, wg, wu, wd)


def _pair_tables(ang):
    c, s = jnp.cos(ang), jnp.sin(ang)
    cos64 = jnp.repeat(c, 2, axis=-1)
    sin64 = jnp.stack([-s, s], axis=-1).reshape(ang.shape[0], HEAD_DIM)
    return jnp.tile(cos64, (1, 2)), jnp.tile(sin64, (1, 2))


def _rope_tables(seq):
    half = HEAD_DIM // 2
    t = jnp.arange(seq)
    inv_a = ROPE_THETA ** (-jnp.arange(0, half, 2, dtype=F32) / half)
    r = (t // GRID_W).astype(F32)
    c = (t % GRID_W).astype(F32)
    ang_a = jnp.concatenate([r[:, None] * inv_a[None, :], c[:, None] * inv_a[None, :]], axis=-1)
    inv_b = ROPE_THETA ** (-jnp.arange(0, HEAD_DIM, 2, dtype=F32) / HEAD_DIM)
    pos = jnp.arange(N_META + seq, dtype=F32)
    ang_b = pos[:, None] * inv_b[None, :]
    seq_tabs = _pair_tables(ang_a) + _pair_tables(ang_b[N_META:])
    meta_tabs = _pair_tables(jnp.zeros((N_META, half), F32)) + _pair_tables(ang_b[:N_META])
    return seq_tabs, tuple(_pad_rows(t, META_ROWS) for t in meta_tabs)


def _pad_rows(a, rows):
    return jnp.pad(a, ((0, rows - a.shape[0]), (0, 0)))


def kernel(x, meta_tokens, attn_norm_g, w_in, a_q_norm_g, a_k_norm_g, b_q_norm_g, b_k_norm_g,
           b_lambda_q1, b_lambda_k1, b_lambda_q2, b_lambda_k2, b_subln_g, w_out,
           ffn_norm_g, w_gate, w_up, w_down):
    bsz, seq, d_model = x.shape
    assert w_in.shape[0] == 1, "single-layer block"
    assert seq % PROJ_ROWS == 0 and seq % KV_CHUNK == 0 and seq % A_STEP_COLS == 0 and seq % B_STEP_COLS == 0
    layer = 0

    seq_tabs, meta_tabs = _rope_tables(seq)
    gains = jnp.stack([jnp.tile(g[layer].astype(F32), 2)
                       for g in (a_q_norm_g, a_k_norm_g, b_q_norm_g, b_k_norm_g)])
    idx = jnp.arange(2 * LANES) // HEAD_DIM
    seg = (idx[:, None] == idx[None, :]).astype(BF16)
    w_in_b = w_in[layer].astype(F32)
    g_attn = attn_norm_g[layer].astype(F32)[None, :]

    qat, ka, vat, qbt, kb, vbt = _project(x, g_attn, w_in_b, seq_tabs, gains, seg, PROJ_ROWS)
    meta3 = _pad_rows(meta_tokens.astype(F32), META_ROWS)[None]
    _, kam, vamt, _, kbm, vbmt = _project(meta3, g_attn, w_in_b, meta_tabs, gains, seg, META_ROWS)
    kam, vamt, kbm, vbmt = kam[0], vamt[0], kbm[0], vbmt[0]
    bias = jnp.where(jnp.arange(META_ROWS) < N_META, 0.0, MASK_BIAS).astype(F32)
    bias = jnp.broadcast_to(bias[:, None], (META_ROWS, LANES))

    lam_params = jnp.stack([jnp.pad(p[layer].astype(F32), (0, LANES - HEAD_DIM))
                            for p in (b_lambda_q1, b_lambda_k1, b_lambda_q2, b_lambda_k2)])
    subln_col = b_subln_g[layer].astype(F32)[:, None]

    def attend(running_max):
        return (_attention_a(qat, ka, vat, kam, vamt, bias, running_max),
                _attention_b(qbt, kb, vbt, kbm, vbmt, bias, lam_params, subln_col, running_max))

    amax = lambda g: jnp.max(jnp.abs(g[layer].astype(F32)))
    score_bound = HEAD_DIM * Q_PRESCALE * jnp.maximum(amax(a_q_norm_g) * amax(a_k_norm_g),
                                                      amax(b_q_norm_g) * amax(b_k_norm_g))
    out_a, out_b = lax.cond(score_bound <= UNSHIFTED_SCORE_LIMIT,
                            lambda: attend(False), lambda: attend(True))

    w_out_b = w_out[layer].astype(BF16)
    y = _post(x.reshape(bsz * seq, d_model), out_a.reshape(bsz * seq, A_WIDTH),
              out_b.reshape(bsz * seq, B_WIDTH), w_out_b,
              ffn_norm_g[layer].astype(F32)[None, :],
              w_gate[layer].astype(BF16), w_up[layer].astype(BF16), w_down[layer].astype(BF16))
    return y.reshape(bsz, seq, d_model)
```
